```python
import math
import jax, jax.numpy as jnp
from jax import lax
import numpy as np

D_MODEL = 1024
BATCH = 8
SEQ = 2048
DEPTH = 1

RMS_EPS = 1e-6
NEG_INF = -1e30

S5_WIDTH = D_MODEL // 2
S5_GROUP = 16
S5_GROUPS = S5_WIDTH // S5_GROUP
S5_STATE = 64

N_Q_HEADS = 16
N_KV_HEADS = 4
HEAD_DIM = 64
Q_PER_KV = N_Q_HEADS // N_KV_HEADS
WINDOW = 128
ATTN_BLOCK = 128
N_BUCKETS = 32
MAX_DISTANCE = 128

PEER_HEADS = 8
PEER_KEY_DIM = 128
N_SUB_KEYS = 128
N_EXPERTS = N_SUB_KEYS * N_SUB_KEYS
PEER_TOPK = 16
PEER_TOKEN_BLOCK = 128

Q_WIDTH = N_Q_HEADS * HEAD_DIM
KV_WIDTH = N_KV_HEADS * HEAD_DIM
GATE_WIDTH = 2 * D_MODEL
IN_WIDTH = S5_WIDTH + Q_WIDTH + 2 * KV_WIDTH + GATE_WIDTH

kernel_name = "hybrid_s5_swa_peer_block"


def rms_norm(x, g):
    xf = x.astype(jnp.float32)
    r = lax.rsqrt(jnp.mean(xf * xf, axis=-1, keepdims=True) + RMS_EPS)
    return (xf * r).astype(x.dtype) * g


def _complex_scan_combine(c1, c2):
    a1r, a1i, b1r, b1i = c1
    a2r, a2i, b2r, b2i = c2
    ar = a2r * a1r - a2i * a1i
    ai = a2r * a1i + a2i * a1r
    br = a2r * b1r - a2i * b1i + b2r
    bi = a2r * b1i + a2i * b1r + b2i
    return (ar, ai, br, bi)


def s5_mixer(u, lam_re, lam_im, log_step, b_re, b_im, c_re, c_im, d_skip, w_glu):
    bsz, seq, _ = u.shape
    f32 = jnp.float32
    uf = u.astype(f32).reshape(bsz, seq, S5_GROUPS, S5_GROUP)
    lr, li = lam_re.astype(f32), lam_im.astype(f32)
    step = jnp.exp(log_step.astype(f32))[:, None]
    mag = jnp.exp(lr * step)
    abar_re = mag * jnp.cos(li * step)
    abar_im = mag * jnp.sin(li * step)
    den = lr * lr + li * li
    nr, ni = abar_re - 1.0, abar_im
    coef_re = ((nr * lr + ni * li) / den)[..., None]
    coef_im = ((ni * lr - nr * li) / den)[..., None]
    br, bi = b_re.astype(f32), b_im.astype(f32)
    bbar_re = coef_re * br - coef_im * bi
    bbar_im = coef_re * bi + coef_im * br
    bu_re = jnp.einsum('blgc,gpc->blgp', uf, bbar_re)
    bu_im = jnp.einsum('blgc,gpc->blgp', uf, bbar_im)
    a_re = jnp.broadcast_to(abar_re, bu_re.shape)
    a_im = jnp.broadcast_to(abar_im, bu_im.shape)
    _, _, h_re, h_im = lax.associative_scan(
        _complex_scan_combine, (a_re, a_im, bu_re, bu_im), axis=1)
    y = (jnp.einsum('blgp,gcp->blgc', h_re, c_re.astype(f32))
         - jnp.einsum('blgp,gcp->blgc', h_im, c_im.astype(f32))
         + d_skip.astype(f32) * uf)
    y = jax.nn.gelu(y.reshape(bsz, seq, S5_WIDTH), approximate=False).astype(u.dtype)
    ab = y @ w_glu
    a, b = jnp.split(ab, 2, axis=-1)
    return a * jax.nn.sigmoid(b)


def t5_bucket(dist):
    max_exact = N_BUCKETS // 2
    d_f = jnp.maximum(dist, 1).astype(jnp.float32)
    large = max_exact + (jnp.log(d_f / max_exact) / math.log(MAX_DISTANCE / max_exact)
                         * (N_BUCKETS - max_exact)).astype(jnp.int32)
    large = jnp.minimum(large, N_BUCKETS - 1)
    return jnp.where(dist < max_exact, dist, large)


def sliding_window_attention(q, k, v, q_norm_g, k_norm_g, rel_bias_table, sinks):
    bsz, seq, _ = q.shape
    nb = seq // ATTN_BLOCK
    q = rms_norm(q.reshape(bsz, seq, N_KV_HEADS, Q_PER_KV, HEAD_DIM), q_norm_g)
    k = rms_norm(k.reshape(bsz, seq, N_KV_HEADS, HEAD_DIM), k_norm_g)
    v = v.reshape(bsz, seq, N_KV_HEADS, HEAD_DIM)
    qb = q.reshape(bsz, nb, ATTN_BLOCK, N_KV_HEADS, Q_PER_KV, HEAD_DIM)

    def band(t):
        pad = jnp.pad(t, ((0, 0), (ATTN_BLOCK, 0), (0, 0), (0, 0)))
        prev = pad[:, :seq].reshape(bsz, nb, ATTN_BLOCK, N_KV_HEADS, HEAD_DIM)
        cur = t.reshape(bsz, nb, ATTN_BLOCK, N_KV_HEADS, HEAD_DIM)
        return jnp.concatenate([prev, cur], axis=2)

    kb, vb = band(k), band(v)
    scale = HEAD_DIM ** -0.5
    scores = jnp.einsum('bnqkgd,bnskd->bnkgqs', qb, kb).astype(jnp.float32) * scale

    qi = jnp.arange(ATTN_BLOCK)[:, None]
    si = jnp.arange(2 * ATTN_BLOCK)[None, :]
    dist = ATTN_BLOCK + qi - si
    band_ok = (dist >= 0) & (dist < WINDOW)
    bias = rel_bias_table.astype(jnp.float32)[t5_bucket(jnp.maximum(dist, 0))]
    bias = bias.transpose(2, 0, 1).reshape(N_KV_HEADS, Q_PER_KV, ATTN_BLOCK, 2 * ATTN_BLOCK)
    blk = jnp.arange(nb)[:, None, None]
    key_ok = (blk - 1) * ATTN_BLOCK + si[None] >= 0
    valid = (band_ok[None] & key_ok)[None, :, None, None]

    scores = jnp.where(valid, scores + bias, NEG_INF)
    sink = sinks.astype(jnp.float32).reshape(1, 1, N_KV_HEADS, Q_PER_KV, 1, 1)
    m = jnp.maximum(jnp.max(scores, axis=-1, keepdims=True), sink)
    p = jnp.exp(scores - m)
    p = p / (jnp.sum(p, axis=-1, keepdims=True) + jnp.exp(sink - m))
    out = jnp.einsum('bnkgqs,bnskd->bnqkgd', p.astype(vb.dtype), vb)
    return out.reshape(bsz, seq, Q_WIDTH)


def peer_mixer(xn, w_query, sub_keys, u_tab, v_tab):
    bsz, seq, d = xn.shape
    t = bsz * seq
    xt = xn.reshape(t, d)
    q = (xt @ w_query).reshape(t, PEER_HEADS, 2, PEER_KEY_DIM // 2)
    s = jnp.einsum('thkd,hknd->thkn', q, sub_keys).astype(jnp.float32)
    vals, idx = lax.top_k(s, PEER_TOPK)
    cand = (vals[:, :, 0, :, None] + vals[:, :, 1, None, :]).reshape(t, PEER_HEADS, PEER_TOPK * PEER_TOPK)
    best, pos = lax.top_k(cand, PEER_TOPK)
    i1 = jnp.take_along_axis(idx[:, :, 0], pos // PEER_TOPK, axis=-1)
    i2 = jnp.take_along_axis(idx[:, :, 1], pos % PEER_TOPK, axis=-1)
    expert = (i1 * N_SUB_KEYS + i2).reshape(t, PEER_HEADS * PEER_TOPK)
    gate = jax.nn.softmax(best, axis=-1).reshape(t, PEER_HEADS * PEER_TOPK)

    def block_fn(args):
        xb, eb, gb = args
        a = jnp.einsum('cd,ced->ce', xb, u_tab[eb]).astype(jnp.float32)
        w = (gb * jax.nn.gelu(a, approximate=False)).astype(xb.dtype)
        return jnp.einsum('ce,ced->cd', w, v_tab[eb])

    nblk = t // PEER_TOKEN_BLOCK
    out = lax.map(block_fn, (xt.reshape(nblk, PEER_TOKEN_BLOCK, d),
                             expert.reshape(nblk, PEER_TOKEN_BLOCK, -1),
                             gate.reshape(nblk, PEER_TOKEN_BLOCK, -1)))
    return out.reshape(bsz, seq, d)


def setup_inputs(seed: int = 0) -> dict:
    key = jax.random.key(seed)
    ks = jax.random.split(key, 26)
    f32 = jnp.float32
    L, D, G, P, C = DEPTH, D_MODEL, S5_GROUPS, S5_STATE, S5_GROUP
    nrm = lambda k, shape, s: jax.random.normal(k, shape, f32) * s
    inp = {}
    inp['x'] = nrm(ks[0], (BATCH, SEQ, D), 1.0)
    inp['ln1_g'] = 1.0 + nrm(ks[1], (L, D), 0.02)
    inp['w_in'] = nrm(ks[2], (L, D, IN_WIDTH), D ** -0.5)
    inp['b_gate'] = nrm(ks[3], (L, GATE_WIDTH), 0.02)
    inp['s5_lambda_re'] = -0.5 + nrm(ks[4], (L, G, P), 0.01)
    inp['s5_lambda_im'] = jnp.pi * jnp.arange(P, dtype=f32) + nrm(ks[5], (L, G, P), 0.01)
    inp['s5_log_step'] = jax.random.uniform(ks[6], (L, G), f32, math.log(1e-3), math.log(1e-1))
    inp['s5_b_re'] = nrm(ks[7], (L, G, P, C), (2 * C) ** -0.5)
    inp['s5_b_im'] = nrm(ks[8], (L, G, P, C), (2 * C) ** -0.5)
    inp['s5_c_re'] = nrm(ks[9], (L, G, C, P), P ** -0.5)
    inp['s5_c_im'] = nrm(ks[10], (L, G, C, P), P ** -0.5)
    inp['s5_d'] = nrm(ks[11], (L, G, C), 1.0)
    inp['s5_w_glu'] = nrm(ks[12], (L, S5_WIDTH, 2 * S5_WIDTH), S5_WIDTH ** -0.5)
    inp['w_s5_branch'] = nrm(ks[13], (L, S5_WIDTH, D), S5_WIDTH ** -0.5)
    inp['q_norm_g'] = 1.0 + nrm(ks[14], (L, HEAD_DIM), 0.02)
    inp['k_norm_g'] = 1.0 + nrm(ks[15], (L, HEAD_DIM), 0.02)
    inp['rel_bias_table'] = nrm(ks[16], (N_BUCKETS, N_Q_HEADS), 0.1)
    inp['attn_sinks'] = nrm(ks[17], (L, N_Q_HEADS), 0.5)
    inp['w_attn_branch'] = nrm(ks[18], (L, Q_WIDTH, D), Q_WIDTH ** -0.5)
    inp['w_out'] = nrm(ks[19], (L, D, D), D ** -0.5)
    inp['ln2_g'] = 1.0 + nrm(ks[20], (L, D), 0.02)
    inp['peer_w_query'] = nrm(ks[21], (L, D, PEER_HEADS * PEER_KEY_DIM), D ** -0.5)
    inp['peer_sub_keys'] = nrm(ks[22], (L, PEER_HEADS, 2, N_SUB_KEYS, PEER_KEY_DIM // 2), (PEER_KEY_DIM // 2) ** -0.5)
    inp['peer_u'] = nrm(ks[23], (L, N_EXPERTS, D), D ** -0.5)
    inp['peer_v'] = nrm(ks[24], (L, N_EXPERTS, D), D ** -0.5)
    return inp


def reference(x, ln1_g, w_in, b_gate, s5_lambda_re, s5_lambda_im, s5_log_step,
              s5_b_re, s5_b_im, s5_c_re, s5_c_im, s5_d, s5_w_glu, w_s5_branch,
              q_norm_g, k_norm_g, rel_bias_table, attn_sinks, w_attn_branch, w_out,
              ln2_g, peer_w_query, peer_sub_keys, peer_u, peer_v):
    splits = [S5_WIDTH, S5_WIDTH + Q_WIDTH, S5_WIDTH + Q_WIDTH + KV_WIDTH,
              S5_WIDTH + Q_WIDTH + 2 * KV_WIDTH]
    h = x
    for layer in range(DEPTH):
        xn = rms_norm(h, ln1_g[layer])
        proj = xn @ w_in[layer]
        u_s5, q, k, v, gate_logits = jnp.split(proj, splits, axis=-1)
        gates = jax.nn.sigmoid(gate_logits + b_gate[layer])
        g_s5, g_attn = jnp.split(gates, 2, axis=-1)
        y_s5 = s5_mixer(u_s5, s5_lambda_re[layer], s5_lambda_im[layer], s5_log_step[layer],
                        s5_b_re[layer], s5_b_im[layer], s5_c_re[layer], s5_c_im[layer],
                        s5_d[layer], s5_w_glu[layer]) @ w_s5_branch[layer]
        y_attn = sliding_window_attention(q, k, v, q_norm_g[layer], k_norm_g[layer],
                                          rel_bias_table, attn_sinks[layer]) @ w_attn_branch[layer]
        h = h + (g_s5 * y_s5 + g_attn * y_attn) @ w_out[layer]
        xn2 = rms_norm(h, ln2_g[layer])
        h = h + peer_mixer(xn2, peer_w_query[layer], peer_sub_keys[layer],
                           peer_u[layer], peer_v[layer])
    return h
```

```python
import functools
import math

import jax
import jax.numpy as jnp
import numpy as np
from jax import lax
from jax.experimental import pallas as pl
from jax.experimental.pallas import tpu as pltpu

F32 = jnp.float32
BF16 = jnp.bfloat16

D_MODEL = 1024
RMS_EPS = 1e-6
NEG_INF = -1e30

S5_WIDTH = 512
S5_GROUP = 16
S5_GROUPS = 32
S5_STATE = 64
S5_LANES = S5_GROUPS * S5_STATE

N_Q_HEADS = 16
N_KV_HEADS = 4
HEAD_DIM = 64
Q_PER_KV = 4
ATTN_BLOCK = 128
N_BUCKETS = 32
MAX_DISTANCE = 128
Q_WIDTH = 1024
KV_WIDTH = 256
UQKV_WIDTH = S5_WIDTH + Q_WIDTH + 2 * KV_WIDTH

PEER_HEADS = 8
PEER_HALF = 64
N_SUB_KEYS = 128
PEER_TOPK = 16
N_PICKS = PEER_HEADS * PEER_TOPK

VMEM_LIMIT = 56 * 1024 * 1024

SQRT_HALF = 0.7071067811865476


def _rms(x, g):
    r = lax.rsqrt(jnp.mean(x * x, axis=-1, keepdims=True) + RMS_EPS)
    return (x * r) * g


def _gelu(x):
    return 0.5 * x * (1.0 + lax.erf(x * SQRT_HALF))


def _dot(a, b):
    return jnp.dot(a, b, preferred_element_type=F32)


def _dot_nt(a, b):
    return lax.dot_general(a, b, (((1,), (1,)), ((), ())), preferred_element_type=F32)


IN_TL = 512


def _in_proj_kernel(x_ref, g_ref, w_ref, u_ref, q_ref, k_ref, v_ref):
    xn = _rms(x_ref[0], g_ref[...]).astype(BF16)
    p = _dot(xn, w_ref[...])
    u_ref[...] = p[:, :S5_WIDTH]
    q_ref[0] = p[:, S5_WIDTH:S5_WIDTH + Q_WIDTH].astype(BF16)
    k_ref[0] = p[:, S5_WIDTH + Q_WIDTH:S5_WIDTH + Q_WIDTH + KV_WIDTH].astype(BF16)
    v_ref[0] = p[:, S5_WIDTH + Q_WIDTH + KV_WIDTH:].astype(BF16)


def _in_proj(x, ln1_g, w_uqkv):
    bsz, seq, d = x.shape
    grid = (bsz, seq // IN_TL)
    return pl.pallas_call(
        _in_proj_kernel,
        grid=grid,
        in_specs=[
            pl.BlockSpec((1, IN_TL, d), lambda b, i: (b, i, 0)),
            pl.BlockSpec((1, d), lambda b, i: (0, 0)),
            pl.BlockSpec((d, UQKV_WIDTH), lambda b, i: (0, 0)),
        ],
        out_specs=[
            pl.BlockSpec((IN_TL, S5_WIDTH), lambda b, i: (i, b)),
            pl.BlockSpec((1, IN_TL, Q_WIDTH), lambda b, i: (b, i, 0)),
            pl.BlockSpec((1, IN_TL, KV_WIDTH), lambda b, i: (b, i, 0)),
            pl.BlockSpec((1, IN_TL, KV_WIDTH), lambda b, i: (b, i, 0)),
        ],
        out_shape=[
            jax.ShapeDtypeStruct((seq, bsz * S5_WIDTH), F32),
            jax.ShapeDtypeStruct((bsz, seq, Q_WIDTH), BF16),
            jax.ShapeDtypeStruct((bsz, seq, KV_WIDTH), BF16),
            jax.ShapeDtypeStruct((bsz, seq, KV_WIDTH), BF16),
        ],
        compiler_params=pltpu.CompilerParams(
            dimension_semantics=("arbitrary", "arbitrary"), vmem_limit_bytes=VMEM_LIMIT),
        name="in_proj",
    )(x, ln1_g, w_uqkv)


S5_TC = 64
S5_LC = 512
S5_UNROLL = 8


def _s5_kernel(bsz, u_ref, a_ref, bblk_ref, cblk_ref, d_ref, wglu_ref, wbr_ref, o_ref,
               h_ref, bu_ref):
    @pl.when(pl.program_id(0) == 0)
    def _():
        h_ref[...] = jnp.zeros_like(h_ref)

    u = u_ref[...]
    bu_ref[...] = _dot(u.astype(BF16), bblk_ref[...])

    for lc in range(S5_LANES // S5_LC):
        re = pl.ds(lc * S5_LC, S5_LC)
        im = pl.ds(S5_LANES + lc * S5_LC, S5_LC)
        ar = a_ref[0, :, re]
        ai = a_ref[1, :, re]

        def group(gi, carry):
            hr, hi = carry
            for s in range(S5_UNROLL):
                rows = pl.ds(pl.multiple_of((gi * S5_UNROLL + s) * bsz, bsz), bsz)
                nr = ar * hr - ai * hi + bu_ref[rows, re]
                ni = ar * hi + ai * hr + bu_ref[rows, im]
                bu_ref[rows, re] = nr
                bu_ref[rows, im] = ni
                hr, hi = nr, ni
            return hr, hi

        hr, hi = lax.fori_loop(0, S5_TC // S5_UNROLL, group, (h_ref[0, :, re], h_ref[1, :, re]))
        h_ref[0, :, re] = hr
        h_ref[1, :, re] = hi

    y = _dot(bu_ref[...].astype(BF16), cblk_ref[...]) + d_ref[...] * u
    yg = _gelu(y).astype(BF16)
    ab = _dot(yg, wglu_ref[...])
    glu = (ab[:, :S5_WIDTH] * jax.nn.sigmoid(ab[:, S5_WIDTH:])).astype(BF16)
    o_ref[...] = _dot(glu, wbr_ref[...]).astype(BF16)


def _s5(u_tm, a8, bblk, cblk, d_row, w_glu, w_br, bsz):
    rows = u_tm.shape[0]
    tr = S5_TC * bsz
    const = lambda shape: pl.BlockSpec(shape, lambda i: (0,) * len(shape))
    return pl.pallas_call(
        functools.partial(_s5_kernel, bsz),
        grid=(rows // tr,),
        in_specs=[
            pl.BlockSpec((tr, S5_WIDTH), lambda i: (i, 0)),
            const((2, bsz, S5_LANES)),
            const((S5_WIDTH, 2 * S5_LANES)),
            const((2 * S5_LANES, S5_WIDTH)),
            const((1, S5_WIDTH)),
            const((S5_WIDTH, 2 * S5_WIDTH)),
            const((S5_WIDTH, D_MODEL)),
        ],
        out_specs=pl.BlockSpec((tr, D_MODEL), lambda i: (i, 0)),
        out_shape=jax.ShapeDtypeStruct((rows, D_MODEL), BF16),
        scratch_shapes=[
            pltpu.VMEM((2, bsz, S5_LANES), F32),
            pltpu.VMEM((tr, 2 * S5_LANES), F32),
        ],
        compiler_params=pltpu.CompilerParams(
            dimension_semantics=("arbitrary",), vmem_limit_bytes=VMEM_LIMIT),
        name="s5",
    )(u_tm, a8, bblk, cblk, d_row, w_glu, w_br)


def _attn_kernel(q_ref, kc_ref, kp_ref, vc_ref, vp_ref, qg_ref, kg_ref, bucket_ref,
                 table_ref, sink_ref, o_ref, bias_ref):
    first = jnp.logical_and(pl.program_id(0) == 0, pl.program_id(1) == 0)

    @pl.when(first)
    def _():
        bucket = bucket_ref[...]
        for h in range(N_Q_HEADS):
            acc = jnp.zeros((ATTN_BLOCK, 2 * ATTN_BLOCK), F32)
            for b in range(N_BUCKETS):
                acc = jnp.where(bucket == b, table_ref[b, h], acc)
            bias_ref[h] = acc

    blk = pl.program_id(1)
    qi = lax.broadcasted_iota(jnp.int32, (ATTN_BLOCK, 2 * ATTN_BLOCK), 0)
    si = lax.broadcasted_iota(jnp.int32, (ATTN_BLOCK, 2 * ATTN_BLOCK), 1)
    dist = ATTN_BLOCK + qi - si
    valid = (dist >= 0) & (dist < ATTN_BLOCK) & ((si >= ATTN_BLOCK) | (blk > 0))
    qg = qg_ref[...]
    kg = kg_ref[...]
    scale = HEAD_DIM ** -0.5

    for kh in range(N_KV_HEADS):
        cols = slice(kh * HEAD_DIM, (kh + 1) * HEAD_DIM)
        kc = _rms(kc_ref[0, :, cols].astype(F32), kg)
        kp = _rms(kp_ref[0, :, cols].astype(F32), kg)
        kband = jnp.concatenate([kp, kc], axis=0).astype(BF16)
        vband = jnp.concatenate([vp_ref[0, :, cols], vc_ref[0, :, cols]], axis=0)
        for g in range(Q_PER_KV):
            h = kh * Q_PER_KV + g
            qh = _rms(q_ref[0, :, h * HEAD_DIM:(h + 1) * HEAD_DIM].astype(F32), qg)
            s = _dot_nt((qh * scale).astype(BF16), kband)
            s = jnp.where(valid, s + bias_ref[h], NEG_INF)
            sink = sink_ref[h]
            m = jnp.maximum(jnp.max(s, axis=-1, keepdims=True), sink)
            p = jnp.exp(s - m)
            den = jnp.sum(p, axis=-1, keepdims=True) + jnp.exp(sink - m)
            o = _dot(p.astype(BF16), vband) / den
            o_ref[0, :, h * HEAD_DIM:(h + 1) * HEAD_DIM] = o.astype(BF16)


def _attention(q, k, v, qg, kg, bucket, table, sinks):
    bsz, seq, _ = q.shape
    nb = seq // ATTN_BLOCK
    cur = lambda b, i: (b, i, 0)
    prev = lambda b, i: (b, jnp.maximum(i - 1, 0), 0)
    const2 = lambda b, i: (0, 0)
    return pl.pallas_call(
        _attn_kernel,
        grid=(bsz, nb),
        in_specs=[
            pl.BlockSpec((1, ATTN_BLOCK, Q_WIDTH), cur),
            pl.BlockSpec((1, ATTN_BLOCK, KV_WIDTH), cur),
            pl.BlockSpec((1, ATTN_BLOCK, KV_WIDTH), prev),
            pl.BlockSpec((1, ATTN_BLOCK, KV_WIDTH), cur),
            pl.BlockSpec((1, ATTN_BLOCK, KV_WIDTH), prev),
            pl.BlockSpec((1, HEAD_DIM), const2),
            pl.BlockSpec((1, HEAD_DIM), const2),
            pl.BlockSpec((ATTN_BLOCK, 2 * ATTN_BLOCK), const2),
            pl.BlockSpec(memory_space=pltpu.SMEM),
            pl.BlockSpec(memory_space=pltpu.SMEM),
        ],
        out_specs=pl.BlockSpec((1, ATTN_BLOCK, Q_WIDTH), cur),
        out_shape=jax.ShapeDtypeStruct((bsz, seq, Q_WIDTH), BF16),
        scratch_shapes=[pltpu.VMEM((N_Q_HEADS, ATTN_BLOCK, 2 * ATTN_BLOCK), F32)],
        compiler_params=pltpu.CompilerParams(
            dimension_semantics=("arbitrary", "arbitrary"), vmem_limit_bytes=VMEM_LIMIT),
        name="attn",
    )(q, k, k, v, v, qg, kg, bucket, table, sinks)


def _t5_bucket_matrix():
    qi = jnp.arange(ATTN_BLOCK)[:, None]
    si = jnp.arange(2 * ATTN_BLOCK)[None, :]
    dist = jnp.maximum(ATTN_BLOCK + qi - si, 0)
    max_exact = N_BUCKETS // 2
    d_f = jnp.maximum(dist, 1).astype(F32)
    large = max_exact + (jnp.log(d_f / max_exact) / math.log(MAX_DISTANCE / max_exact)
                         * (N_BUCKETS - max_exact)).astype(jnp.int32)
    large = jnp.minimum(large, N_BUCKETS - 1)
    return jnp.where(dist < max_exact, dist, large).astype(jnp.int32)


MERGE_TM = 512


def _merge_kernel(x_ref, attn_ref, ys5_ref, g1_ref, wg_ref, bg_ref, wab_ref, wout_ref,
                  g2_ref, wpq_ref, h1_ref, xn2_ref, qp_ref):
    x = x_ref[0]
    xn = _rms(x, g1_ref[...]).astype(BF16)
    gates = jax.nn.sigmoid(_dot(xn, wg_ref[...]) + bg_ref[...])
    y_attn = _dot(attn_ref[0], wab_ref[...])
    y_s5 = ys5_ref[...].astype(F32)
    mix = gates[:, :D_MODEL] * y_s5 + gates[:, D_MODEL:] * y_attn
    h1 = x + _dot(mix.astype(BF16), wout_ref[...])
    h1_ref[...] = h1
    xn2 = _rms(h1, g2_ref[...]).astype(BF16)
    xn2_ref[...] = xn2
    qp_ref[...] = _dot(xn2, wpq_ref[...]).astype(BF16)


def _merge(x, attn, ys5_tm, ln1_g, w_gate, b_gate, w_ab, w_out, ln2_g, w_pq):
    bsz, seq, d = x.shape
    nt = seq // MERGE_TM
    tok = lambda b, i: (b * nt + i, 0)
    const2 = lambda b, i: (0, 0)
    return pl.pallas_call(
        _merge_kernel,
        grid=(bsz, nt),
        in_specs=[
            pl.BlockSpec((1, MERGE_TM, d), lambda b, i: (b, i, 0)),
            pl.BlockSpec((1, MERGE_TM, d), lambda b, i: (b, i, 0)),
            pl.BlockSpec((MERGE_TM, d), lambda b, i: (i, b)),
            pl.BlockSpec((1, d), const2),
            pl.BlockSpec((d, 2 * d), const2),
            pl.BlockSpec((1, 2 * d), const2),
            pl.BlockSpec((d, d), const2),
            pl.BlockSpec((d, d), const2),
            pl.BlockSpec((1, d), const2),
            pl.BlockSpec((d, d), const2),
        ],
        out_specs=[
            pl.BlockSpec((MERGE_TM, d), tok),
            pl.BlockSpec((MERGE_TM, d), tok),
            pl.BlockSpec((MERGE_TM, d), tok),
        ],
        out_shape=[
            jax.ShapeDtypeStruct((bsz * seq, d), F32),
            jax.ShapeDtypeStruct((bsz * seq, d), BF16),
            jax.ShapeDtypeStruct((bsz * seq, d), BF16),
        ],
        compiler_params=pltpu.CompilerParams(
            dimension_semantics=("arbitrary", "arbitrary"), vmem_limit_bytes=VMEM_LIMIT),
        name="merge",
    )(x, attn, ys5_tm, ln1_g, w_gate, b_gate, w_ab, w_out, ln2_g, w_pq)


TOPK_TK = 512
LANES = 128
CAND_ROWS = 80
POS_INVALID = 1.0e9


def _cand_layout():
    pos = np.full((CAND_ROWS,), POS_INVALID, np.float32)
    blocks = [(0, 0), (0, 8), (1, 0), (2, 0), (3, 0), (4, 0), (5, 0), (6, 0), (7, 0)]
    for r, (a, b0) in enumerate(blocks):
        for j in range(8):
            b = b0 + j
            if (a + 1) * (b + 1) <= PEER_TOPK:
                pos[r * 8 + j] = a * PEER_TOPK + b
    for j in range(8):
        pos[72 + j] = (8 + j) * PEER_TOPK
    return blocks, np.broadcast_to(pos[:, None], (CAND_ROWS, LANES)).copy()


_CAND_BLOCKS, _CAND_POS = _cand_layout()


def _extract_top(s, ids, big, n, val_ref, id_ref, lanes):
    for j in range(n):
        m = jnp.max(s, axis=0, keepdims=True)
        sel = jnp.min(jnp.where(s == m, ids, big), axis=0, keepdims=True)
        val_ref[j:j + 1, lanes] = m
        id_ref[j:j + 1, lanes] = sel
        s = jnp.where(ids == sel, -jnp.inf, s)


def _topk_kernel(qp_ref, sk_ref, pos_ref, e_ref, g_ref, v1_ref, i1_ref, v2_ref, i2_ref,
                 best_ref, bpos_ref):
    key_ids = lax.broadcasted_iota(jnp.int32, (N_SUB_KEYS, LANES), 0).astype(F32)
    pos = pos_ref[...]
    for c in range(TOPK_TK // LANES):
        lanes = pl.ds(c * LANES, LANES)
        rows = pl.ds(c * LANES, LANES)
        s1 = _dot_nt(sk_ref[0, 0], qp_ref[rows, :PEER_HALF])
        s2 = _dot_nt(sk_ref[0, 1], qp_ref[rows, PEER_HALF:])
        _extract_top(s1, key_ids, float(N_SUB_KEYS), PEER_TOPK, v1_ref, i1_ref, lanes)
        _extract_top(s2, key_ids, float(N_SUB_KEYS), PEER_TOPK, v2_ref, i2_ref, lanes)

        cand, expert = [], []
        for a, b0 in _CAND_BLOCKS:
            cand.append(v1_ref[a:a + 1, lanes] + v2_ref[b0:b0 + 8, lanes])
            expert.append(i1_ref[a:a + 1, lanes] * float(N_SUB_KEYS) + i2_ref[b0:b0 + 8, lanes])
        cand.append(v1_ref[8:16, lanes] + v2_ref[0:1, lanes])
        expert.append(i1_ref[8:16, lanes] * float(N_SUB_KEYS) + i2_ref[0:1, lanes])
        cand = jnp.concatenate(cand, axis=0)
        expert = jnp.concatenate(expert, axis=0)
        cand = jnp.where(pos < POS_INVALID, cand, -jnp.inf)
        _extract_top(cand, pos, POS_INVALID, PEER_TOPK, best_ref, bpos_ref, lanes)

        picked = []
        for j in range(PEER_TOPK):
            hit = pos == bpos_ref[j:j + 1, lanes]
            picked.append(jnp.sum(jnp.where(hit, expert, 0.0), axis=0, keepdims=True))
        e_ref[:, lanes] = jnp.concatenate(picked, axis=0).astype(jnp.int32)

        best = best_ref[:, lanes]
        p = jnp.exp(best - best_ref[0:1, lanes])
        g_ref[:, lanes] = p / jnp.sum(p, axis=0, keepdims=True)


def _topk(qp, sub_keys, pos):
    t = qp.shape[0]
    scratch = [pltpu.VMEM((PEER_TOPK, TOPK_TK), F32) for _ in range(6)]
    return pl.pallas_call(
        _topk_kernel,
        grid=(t // TOPK_TK, PEER_HEADS),
        in_specs=[
            pl.BlockSpec((TOPK_TK, 2 * PEER_HALF), lambda i, h: (i, h)),
            pl.BlockSpec((1, 2, N_SUB_KEYS, PEER_HALF), lambda i, h: (h, 0, 0, 0)),
            pl.BlockSpec((CAND_ROWS, LANES), lambda i, h: (0, 0)),
        ],
        out_specs=[
            pl.BlockSpec((PEER_TOPK, TOPK_TK), lambda i, h: (h, i)),
            pl.BlockSpec((PEER_TOPK, TOPK_TK), lambda i, h: (h, i)),
        ],
        out_shape=[
            jax.ShapeDtypeStruct((N_PICKS, t), jnp.int32),
            jax.ShapeDtypeStruct((N_PICKS, t), F32),
        ],
        scratch_shapes=scratch,
        compiler_params=pltpu.CompilerParams(
            dimension_semantics=("arbitrary", "arbitrary"), vmem_limit_bytes=VMEM_LIMIT),
        name="topk",
    )(qp, sub_keys, pos)


PEER_TM = 512
PEER_CHUNK = 1024
PEER_SLABS = PEER_CHUNK // N_SUB_KEYS
PEER_NCHUNK = (N_SUB_KEYS * N_SUB_KEYS) // PEER_CHUNK
PEER_TG = 16


def _peer_kernel(xn_ref, u_ref, v_ref, e_ref, g_ref, h1_ref, o_ref, hg_ref, et_ref, gt_ref, w_ref):
    j = pl.program_id(1)

    @pl.when(j == 0)
    def _():
        et_ref[...] = e_ref[...].T
        gt_ref[...] = g_ref[...].T
        sub = lax.broadcasted_iota(jnp.int32, (N_SUB_KEYS, N_PICKS), 0)

        def group(gi, carry):
            t0 = pl.multiple_of(gi * PEER_TG, PEER_TG)
            for tt in range(PEER_TG):
                e_row = et_ref[pl.ds(t0 + tt, 1), :]
                g_row = gt_ref[pl.ds(t0 + tt, 1), :]
                first = jnp.where(sub == (e_row >> 7), g_row, 0.0).astype(BF16)
                second = jnp.where(sub == (e_row & (N_SUB_KEYS - 1)), 1.0, 0.0).astype(BF16)
                w_ref[pl.ds(tt * N_SUB_KEYS, N_SUB_KEYS), :] = _dot_nt(first, second)
            for i1 in range(N_SUB_KEYS):
                rows = w_ref[pl.ds(i1, PEER_TG, stride=N_SUB_KEYS), :]
                c, s = divmod(i1, PEER_SLABS)
                hg_ref[c, pl.ds(t0, PEER_TG), s * N_SUB_KEYS:(s + 1) * N_SUB_KEYS] = rows.astype(BF16)
            return carry

        lax.fori_loop(0, PEER_TM // PEER_TG, group, 0)
        o_ref[...] = h1_ref[...]

    a = _dot_nt(xn_ref[...], u_ref[...])
    h = (_gelu(a) * hg_ref[j].astype(F32)).astype(BF16)
    o_ref[...] += _dot(h, v_ref[...])


def _peer(xn2, u_tab, v_tab, e_t, g_t, h1):
    t, d = xn2.shape
    return pl.pallas_call(
        _peer_kernel,
        grid=(t // PEER_TM, PEER_NCHUNK),
        in_specs=[
            pl.BlockSpec((PEER_TM, d), lambda i, j: (i, 0)),
            pl.BlockSpec((PEER_CHUNK, d), lambda i, j: (j, 0)),
            pl.BlockSpec((PEER_CHUNK, d), lambda i, j: (j, 0)),
            pl.BlockSpec((N_PICKS, PEER_TM), lambda i, j: (0, i)),
            pl.BlockSpec((N_PICKS, PEER_TM), lambda i, j: (0, i)),
            pl.BlockSpec((PEER_TM, d), lambda i, j: (i, 0)),
        ],
        out_specs=pl.BlockSpec((PEER_TM, d), lambda i, j: (i, 0)),
        out_shape=jax.ShapeDtypeStruct((t, d), F32),
        scratch_shapes=[
            pltpu.VMEM((PEER_NCHUNK, PEER_TM, PEER_CHUNK), BF16),
            pltpu.VMEM((PEER_TM, N_PICKS), jnp.int32),
            pltpu.VMEM((PEER_TM, N_PICKS), F32),
            pltpu.VMEM((PEER_TG * N_SUB_KEYS, N_SUB_KEYS), F32),
        ],
        compiler_params=pltpu.CompilerParams(
            dimension_semantics=("arbitrary", "arbitrary"), vmem_limit_bytes=VMEM_LIMIT),
        name="peer",
    )(xn2, u_tab, v_tab, e_t, g_t, h1)


def _s5_params(lam_re, lam_im, log_step, b_re, b_im, c_re, c_im, bsz):
    step = jnp.exp(log_step)[:, None]
    mag = jnp.exp(lam_re * step)
    abar_re = mag * jnp.cos(lam_im * step)
    abar_im = mag * jnp.sin(lam_im * step)
    den = lam_re * lam_re + lam_im * lam_im
    nr, ni = abar_re - 1.0, abar_im
    coef_re = ((nr * lam_re + ni * lam_im) / den)[..., None]
    coef_im = ((ni * lam_re - nr * lam_im) / den)[..., None]
    bbar_re = coef_re * b_re - coef_im * b_im
    bbar_im = coef_re * b_im + coef_im * b_re
    eye = jnp.eye(S5_GROUPS, dtype=F32)
    to_b = lambda m: jnp.einsum('gpc,gh->gchp', m, eye).reshape(S5_WIDTH, S5_LANES)
    to_c = lambda m: jnp.einsum('gcp,gh->gphc', m, eye).reshape(S5_LANES, S5_WIDTH)
    bblk = jnp.concatenate([to_b(bbar_re), to_b(bbar_im)], axis=1).astype(BF16)
    cblk = jnp.concatenate([to_c(c_re), -to_c(c_im)], axis=0).astype(BF16)
    a = jnp.stack([abar_re.reshape(-1), abar_im.reshape(-1)])
    a8 = jnp.broadcast_to(a[:, None, :], (2, bsz, S5_LANES))
    return a8, bblk, cblk


def kernel(x, ln1_g, w_in, b_gate, s5_lambda_re, s5_lambda_im, s5_log_step, s5_b_re, s5_b_im, s5_c_re, s5_c_im, s5_d, s5_w_glu, w_s5_branch, q_norm_g, k_norm_g, rel_bias_table, attn_sinks, w_attn_branch, w_out, ln2_g, peer_w_query, peer_sub_keys, peer_u, peer_v):
    bsz, seq, d = x.shape
    h = x
    for layer in range(ln1_g.shape[0]):
        w_uqkv = w_in[layer, :, :UQKV_WIDTH].astype(BF16)
        w_gate = w_in[layer, :, UQKV_WIDTH:].astype(BF16)
        u_tm, q, k, v = _in_proj(h, ln1_g[layer][None], w_uqkv)

        a8, bblk, cblk = _s5_params(s5_lambda_re[layer], s5_lambda_im[layer], s5_log_step[layer],
                                    s5_b_re[layer], s5_b_im[layer], s5_c_re[layer], s5_c_im[layer], bsz)
        ys5 = _s5(u_tm.reshape(seq * bsz, S5_WIDTH), a8, bblk, cblk,
                  s5_d[layer].reshape(1, S5_WIDTH), s5_w_glu[layer].astype(BF16),
                  w_s5_branch[layer].astype(BF16), bsz)

        attn = _attention(q, k, v, q_norm_g[layer][None], k_norm_g[layer][None],
                          _t5_bucket_matrix(), rel_bias_table, attn_sinks[layer])

        h1, xn2, qp = _merge(h, attn, ys5.reshape(seq, bsz * d), ln1_g[layer][None], w_gate,
                             b_gate[layer][None], w_attn_branch[layer].astype(BF16),
                             w_out[layer].astype(BF16), ln2_g[layer][None],
                             peer_w_query[layer].astype(BF16))

        e_t, g_t = _topk(qp, peer_sub_keys[layer].astype(BF16), jnp.asarray(_CAND_POS))
        out = _peer(xn2, peer_u[layer].astype(BF16), peer_v[layer].astype(BF16), e_t, g_t, h1)
        h = out.reshape(bsz, seq, d)
    return h
```

```python
import functools
import math

import jax
import jax.numpy as jnp
import numpy as np
from jax import lax
from jax.experimental import pallas as pl
from jax.experimental.pallas import tpu as pltpu

F32 = jnp.float32
BF16 = jnp.bfloat16

D_MODEL = 1024
RMS_EPS = 1e-6
NEG_INF = -1e30

S5_WIDTH = 512
S5_GROUP = 16
S5_GROUPS = 32
S5_STATE = 64
S5_LANES = S5_GROUPS * S5_STATE

N_Q_HEADS = 16
N_KV_HEADS = 4
HEAD_DIM = 64
Q_PER_KV = 4
ATTN_BLOCK = 128
N_BUCKETS = 32
MAX_DISTANCE = 128
Q_WIDTH = 1024
KV_WIDTH = 256
UQKV_WIDTH = S5_WIDTH + Q_WIDTH + 2 * KV_WIDTH

PEER_HEADS = 8
PEER_HALF = 64
N_SUB_KEYS = 128
PEER_TOPK = 16
N_PICKS = PEER_HEADS * PEER_TOPK

VMEM_LIMIT = 56 * 1024 * 1024

SQRT_HALF = 0.7071067811865476


def _rms(x, g):
    r = lax.rsqrt(jnp.mean(x * x, axis=-1, keepdims=True) + RMS_EPS)
    return (x * r) * g


def _gelu(x):
    return 0.5 * x * (1.0 + lax.erf(x * SQRT_HALF))


def _dot(a, b):
    return jnp.dot(a, b, preferred_element_type=F32)


def _dot_nt(a, b):
    return lax.dot_general(a, b, (((1,), (1,)), ((), ())), preferred_element_type=F32)


S5_TC = 64


def _in_proj_kernel(bsz, x_ref, g_ref, w_ref, u_ref, q_ref, k_ref, v_ref):
    d = x_ref.shape[-1]
    xn = _rms(x_ref[...].reshape(bsz * S5_TC, d), g_ref[...]).astype(BF16)
    p = _dot(xn, w_ref[...])
    u = p[:, :S5_WIDTH].reshape(bsz, S5_TC, S5_WIDTH)
    u_ref[...] = pltpu.einshape("btc->tbc", u).reshape(S5_TC * bsz, S5_WIDTH)
    q_ref[...] = p[:, S5_WIDTH:S5_WIDTH + Q_WIDTH].astype(BF16).reshape(bsz, S5_TC, Q_WIDTH)
    k_ref[...] = p[:, S5_WIDTH + Q_WIDTH:S5_WIDTH + Q_WIDTH + KV_WIDTH].astype(BF16).reshape(
        bsz, S5_TC, KV_WIDTH)
    v_ref[...] = p[:, S5_WIDTH + Q_WIDTH + KV_WIDTH:].astype(BF16).reshape(bsz, S5_TC, KV_WIDTH)


def _in_proj(x, ln1_g, w_uqkv):
    bsz, seq, d = x.shape
    return pl.pallas_call(
        functools.partial(_in_proj_kernel, bsz),
        grid=(seq // S5_TC,),
        in_specs=[
            pl.BlockSpec((bsz, S5_TC, d), lambda i: (0, i, 0)),
            pl.BlockSpec((1, d), lambda i: (0, 0)),
            pl.BlockSpec((d, UQKV_WIDTH), lambda i: (0, 0)),
        ],
        out_specs=[
            pl.BlockSpec((S5_TC * bsz, S5_WIDTH), lambda i: (i, 0)),
            pl.BlockSpec((bsz, S5_TC, Q_WIDTH), lambda i: (0, i, 0)),
            pl.BlockSpec((bsz, S5_TC, KV_WIDTH), lambda i: (0, i, 0)),
            pl.BlockSpec((bsz, S5_TC, KV_WIDTH), lambda i: (0, i, 0)),
        ],
        out_shape=[
            jax.ShapeDtypeStruct((seq * bsz, S5_WIDTH), F32),
            jax.ShapeDtypeStruct((bsz, seq, Q_WIDTH), BF16),
            jax.ShapeDtypeStruct((bsz, seq, KV_WIDTH), BF16),
            jax.ShapeDtypeStruct((bsz, seq, KV_WIDTH), BF16),
        ],
        compiler_params=pltpu.CompilerParams(
            dimension_semantics=("arbitrary",), vmem_limit_bytes=VMEM_LIMIT),
        name="in_proj",
    )(x, ln1_g, w_uqkv)


S5_LC = 512
S5_UNROLL = 8


def _s5_kernel(bsz, u_ref, a_ref, bblk_ref, cblk_ref, d_ref, wglu_ref, wbr_ref, o_ref,
               h_ref, bu_ref):
    @pl.when(pl.program_id(0) == 0)
    def _():
        h_ref[...] = jnp.zeros_like(h_ref)

    u = u_ref[...]
    bu_ref[...] = _dot(u.astype(BF16), bblk_ref[...])

    for lc in range(S5_LANES // S5_LC):
        re = pl.ds(lc * S5_LC, S5_LC)
        im = pl.ds(S5_LANES + lc * S5_LC, S5_LC)
        ar = a_ref[0, :, re]
        ai = a_ref[1, :, re]

        def group(gi, carry):
            hr, hi = carry
            for s in range(S5_UNROLL):
                rows = pl.ds(pl.multiple_of((gi * S5_UNROLL + s) * bsz, bsz), bsz)
                nr = ar * hr - ai * hi + bu_ref[rows, re]
                ni = ar * hi + ai * hr + bu_ref[rows, im]
                bu_ref[rows, re] = nr
                bu_ref[rows, im] = ni
                hr, hi = nr, ni
            return hr, hi

        hr, hi = lax.fori_loop(0, S5_TC // S5_UNROLL, group, (h_ref[0, :, re], h_ref[1, :, re]))
        h_ref[0, :, re] = hr
        h_ref[1, :, re] = hi

    y = _dot(bu_ref[...].astype(BF16), cblk_ref[...]) + d_ref[...] * u
    yg = _gelu(y).astype(BF16)
    ab = _dot(yg, wglu_ref[...])
    glu = ab[:, :S5_WIDTH] * jax.nn.sigmoid(ab[:, S5_WIDTH:])
    glu = pltpu.einshape("tbc->btc", glu.reshape(S5_TC, bsz, S5_WIDTH)).reshape(bsz * S5_TC, S5_WIDTH)
    o_ref[...] = _dot(glu.astype(BF16), wbr_ref[...]).astype(BF16).reshape(bsz, S5_TC, D_MODEL)


def _s5(u_tm, a8, bblk, cblk, d_row, w_glu, w_br, bsz):
    rows = u_tm.shape[0]
    tr = S5_TC * bsz
    const = lambda shape: pl.BlockSpec(shape, lambda i: (0,) * len(shape))
    return pl.pallas_call(
        functools.partial(_s5_kernel, bsz),
        grid=(rows // tr,),
        in_specs=[
            pl.BlockSpec((tr, S5_WIDTH), lambda i: (i, 0)),
            const((2, bsz, S5_LANES)),
            const((S5_WIDTH, 2 * S5_LANES)),
            const((2 * S5_LANES, S5_WIDTH)),
            const((1, S5_WIDTH)),
            const((S5_WIDTH, 2 * S5_WIDTH)),
            const((S5_WIDTH, D_MODEL)),
        ],
        out_specs=pl.BlockSpec((bsz, S5_TC, D_MODEL), lambda i: (0, i, 0)),
        out_shape=jax.ShapeDtypeStruct((bsz, rows // bsz, D_MODEL), BF16),
        scratch_shapes=[
            pltpu.VMEM((2, bsz, S5_LANES), F32),
            pltpu.VMEM((tr, 2 * S5_LANES), F32),
        ],
        compiler_params=pltpu.CompilerParams(
            dimension_semantics=("arbitrary",), vmem_limit_bytes=VMEM_LIMIT),
        name="s5",
    )(u_tm, a8, bblk, cblk, d_row, w_glu, w_br)


def _attn_kernel(q_ref, kc_ref, kp_ref, vc_ref, vp_ref, qg_ref, kg_ref, bucket_ref,
                 table_ref, sink_ref, o_ref, bias_ref):
    first = jnp.logical_and(pl.program_id(0) == 0, pl.program_id(1) == 0)

    @pl.when(first)
    def _():
        bucket = bucket_ref[...]
        for h in range(N_Q_HEADS):
            acc = jnp.zeros((ATTN_BLOCK, 2 * ATTN_BLOCK), F32)
            for b in range(N_BUCKETS):
                acc = jnp.where(bucket == b, table_ref[b, h], acc)
            bias_ref[h] = acc

    blk = pl.program_id(1)
    qi = lax.broadcasted_iota(jnp.int32, (ATTN_BLOCK, 2 * ATTN_BLOCK), 0)
    si = lax.broadcasted_iota(jnp.int32, (ATTN_BLOCK, 2 * ATTN_BLOCK), 1)
    dist = ATTN_BLOCK + qi - si
    valid = (dist >= 0) & (dist < ATTN_BLOCK) & ((si >= ATTN_BLOCK) | (blk > 0))
    qg = qg_ref[...]
    kg = kg_ref[...]
    scale = HEAD_DIM ** -0.5

    grp_rows = Q_PER_KV * ATTN_BLOCK
    row_head = lax.broadcasted_iota(jnp.int32, (grp_rows, 1), 0) // ATTN_BLOCK

    for kh in range(N_KV_HEADS):
        cols = slice(kh * HEAD_DIM, (kh + 1) * HEAD_DIM)
        kband = jnp.concatenate([kp_ref[0, :, cols], kc_ref[0, :, cols]], axis=0).astype(F32)
        kband = _rms(kband, kg).astype(BF16)
        vband = jnp.concatenate([vp_ref[0, :, cols], vc_ref[0, :, cols]], axis=0)
        h0 = kh * Q_PER_KV
        qs = jnp.concatenate(
            [q_ref[0, :, (h0 + g) * HEAD_DIM:(h0 + g + 1) * HEAD_DIM] for g in range(Q_PER_KV)],
            axis=0).astype(F32)
        qs = (_rms(qs, qg) * scale).astype(BF16)
        s = _dot_nt(qs, kband).reshape(Q_PER_KV, ATTN_BLOCK, 2 * ATTN_BLOCK)
        s = jnp.where(valid[None], s + bias_ref[h0:h0 + Q_PER_KV], NEG_INF).reshape(
            grp_rows, 2 * ATTN_BLOCK)
        sink = jnp.full((grp_rows, 1), sink_ref[h0], F32)
        for g in range(1, Q_PER_KV):
            sink = jnp.where(row_head == g, sink_ref[h0 + g], sink)
        m = jnp.maximum(jnp.max(s, axis=-1, keepdims=True), sink)
        p = jnp.exp(s - m)
        den = jnp.sum(p, axis=-1, keepdims=True) + jnp.exp(sink - m)
        o = (_dot(p.astype(BF16), vband) / den).astype(BF16)
        for g in range(Q_PER_KV):
            o_ref[0, :, (h0 + g) * HEAD_DIM:(h0 + g + 1) * HEAD_DIM] = o[g * ATTN_BLOCK:(g + 1) * ATTN_BLOCK]


def _attention(q, k, v, qg, kg, bucket, table, sinks):
    bsz, seq, _ = q.shape
    nb = seq // ATTN_BLOCK
    cur = lambda b, i: (b, i, 0)
    prev = lambda b, i: (b, jnp.maximum(i - 1, 0), 0)
    const2 = lambda b, i: (0, 0)
    return pl.pallas_call(
        _attn_kernel,
        grid=(bsz, nb),
        in_specs=[
            pl.BlockSpec((1, ATTN_BLOCK, Q_WIDTH), cur),
            pl.BlockSpec((1, ATTN_BLOCK, KV_WIDTH), cur),
            pl.BlockSpec((1, ATTN_BLOCK, KV_WIDTH), prev),
            pl.BlockSpec((1, ATTN_BLOCK, KV_WIDTH), cur),
            pl.BlockSpec((1, ATTN_BLOCK, KV_WIDTH), prev),
            pl.BlockSpec((1, HEAD_DIM), const2),
            pl.BlockSpec((1, HEAD_DIM), const2),
            pl.BlockSpec((ATTN_BLOCK, 2 * ATTN_BLOCK), const2),
            pl.BlockSpec(memory_space=pltpu.SMEM),
            pl.BlockSpec(memory_space=pltpu.SMEM),
        ],
        out_specs=pl.BlockSpec((1, ATTN_BLOCK, Q_WIDTH), cur),
        out_shape=jax.ShapeDtypeStruct((bsz, seq, Q_WIDTH), BF16),
        scratch_shapes=[pltpu.VMEM((N_Q_HEADS, ATTN_BLOCK, 2 * ATTN_BLOCK), F32)],
        compiler_params=pltpu.CompilerParams(
            dimension_semantics=("arbitrary", "arbitrary"), vmem_limit_bytes=VMEM_LIMIT),
        name="attn",
    )(q, k, k, v, v, qg, kg, bucket, table, sinks)


def _t5_bucket_matrix():
    qi = jnp.arange(ATTN_BLOCK)[:, None]
    si = jnp.arange(2 * ATTN_BLOCK)[None, :]
    dist = jnp.maximum(ATTN_BLOCK + qi - si, 0)
    max_exact = N_BUCKETS // 2
    d_f = jnp.maximum(dist, 1).astype(F32)
    large = max_exact + (jnp.log(d_f / max_exact) / math.log(MAX_DISTANCE / max_exact)
                         * (N_BUCKETS - max_exact)).astype(jnp.int32)
    large = jnp.minimum(large, N_BUCKETS - 1)
    return jnp.where(dist < max_exact, dist, large).astype(jnp.int32)


MERGE_TM = 512


def _merge_kernel(x_ref, attn_ref, ys5_ref, g1_ref, wg_ref, bg_ref, wab_ref, wout_ref,
                  g2_ref, wpq_ref, h1_ref, xn2_ref, qp_ref):
    x = x_ref[0]
    xn = _rms(x, g1_ref[...]).astype(BF16)
    gates = jax.nn.sigmoid(_dot(xn, wg_ref[...]) + bg_ref[...])
    y_attn = _dot(attn_ref[0], wab_ref[...])
    y_s5 = ys5_ref[0].astype(F32)
    mix = gates[:, :D_MODEL] * y_s5 + gates[:, D_MODEL:] * y_attn
    h1 = x + _dot(mix.astype(BF16), wout_ref[...])
    h1_ref[...] = h1
    xn2 = _rms(h1, g2_ref[...]).astype(BF16)
    xn2_ref[...] = xn2
    qp_ref[...] = _dot(xn2, wpq_ref[...]).astype(BF16)


def _merge(x, attn, ys5_tm, ln1_g, w_gate, b_gate, w_ab, w_out, ln2_g, w_pq):
    bsz, seq, d = x.shape
    nt = seq // MERGE_TM
    tok = lambda b, i: (b * nt + i, 0)
    const2 = lambda b, i: (0, 0)
    return pl.pallas_call(
        _merge_kernel,
        grid=(bsz, nt),
        in_specs=[
            pl.BlockSpec((1, MERGE_TM, d), lambda b, i: (b, i, 0)),
            pl.BlockSpec((1, MERGE_TM, d), lambda b, i: (b, i, 0)),
            pl.BlockSpec((1, MERGE_TM, d), lambda b, i: (b, i, 0)),
            pl.BlockSpec((1, d), const2),
            pl.BlockSpec((d, 2 * d), const2),
            pl.BlockSpec((1, 2 * d), const2),
            pl.BlockSpec((d, d), const2),
            pl.BlockSpec((d, d), const2),
            pl.BlockSpec((1, d), const2),
            pl.BlockSpec((d, d), const2),
        ],
        out_specs=[
            pl.BlockSpec((MERGE_TM, d), tok),
            pl.BlockSpec((MERGE_TM, d), tok),
            pl.BlockSpec((MERGE_TM, d), tok),
        ],
        out_shape=[
            jax.ShapeDtypeStruct((bsz * seq, d), F32),
            jax.ShapeDtypeStruct((bsz * seq, d), BF16),
            jax.ShapeDtypeStruct((bsz * seq, d), BF16),
        ],
        compiler_params=pltpu.CompilerParams(
            dimension_semantics=("arbitrary", "arbitrary"), vmem_limit_bytes=VMEM_LIMIT),
        name="merge",
    )(x, attn, ys5_tm, ln1_g, w_gate, b_gate, w_ab, w_out, ln2_g, w_pq)


TOPK_TK = 512
LANES = 128
SUBLANES = 8
CAND_ROWS = 80
POS_INVALID = 1.0e9


def _cand_layout():
    pos = np.full((CAND_ROWS,), POS_INVALID, np.float32)
    blocks = [(0, 0), (0, 8), (1, 0), (2, 0), (3, 0), (4, 0), (5, 0), (6, 0), (7, 0)]
    for r, (a, b0) in enumerate(blocks):
        for j in range(8):
            b = b0 + j
            if (a + 1) * (b + 1) <= PEER_TOPK:
                pos[r * 8 + j] = a * PEER_TOPK + b
    for j in range(8):
        pos[72 + j] = (8 + j) * PEER_TOPK
    return blocks, np.broadcast_to(pos[:, None], (CAND_ROWS, LANES)).copy()


_CAND_BLOCKS, _CAND_POS = _cand_layout()


def _extract_top(s, ids, big, n, val_ref, id_ref, lanes):
    for j in range(n):
        m = jnp.max(s, axis=0, keepdims=True)
        sel = jnp.min(jnp.where(s == m, ids, big), axis=0, keepdims=True)
        val_ref[j:j + 1, lanes] = m
        id_ref[j:j + 1, lanes] = sel
        s = jnp.where(ids == sel, -jnp.inf, s)


def _topk_kernel(qp_ref, sk_ref, pos_ref, e_ref, g_ref, v1_ref, i1_ref, v2_ref, i2_ref,
                 best_ref, bpos_ref):
    key_ids = lax.broadcasted_iota(jnp.int32, (N_SUB_KEYS, LANES), 0).astype(F32)
    pos = pos_ref[...]
    for c in range(TOPK_TK // LANES):
        lanes = pl.ds(c * LANES, LANES)
        rows = pl.ds(c * LANES, LANES)
        s1 = _dot_nt(sk_ref[0, 0], qp_ref[rows, :PEER_HALF])
        s2 = _dot_nt(sk_ref[0, 1], qp_ref[rows, PEER_HALF:])
        _extract_top(s1, key_ids, float(N_SUB_KEYS), PEER_TOPK, v1_ref, i1_ref, lanes)
        _extract_top(s2, key_ids, float(N_SUB_KEYS), PEER_TOPK, v2_ref, i2_ref, lanes)

        cand, expert = [], []
        for a, b0 in _CAND_BLOCKS:
            cand.append(v1_ref[a:a + 1, lanes] + v2_ref[b0:b0 + 8, lanes])
            expert.append(i1_ref[a:a + 1, lanes] * float(N_SUB_KEYS) + i2_ref[b0:b0 + 8, lanes])
        cand.append(v1_ref[8:16, lanes] + v2_ref[0:1, lanes])
        expert.append(i1_ref[8:16, lanes] * float(N_SUB_KEYS) + i2_ref[0:1, lanes])
        cand = jnp.concatenate(cand, axis=0)
        expert = jnp.concatenate(expert, axis=0)
        cand = jnp.where(pos < POS_INVALID, cand, -jnp.inf)
        _extract_top(cand, pos, POS_INVALID, PEER_TOPK, best_ref, bpos_ref, lanes)

        picked = []
        for j in range(PEER_TOPK):
            hit = pos == bpos_ref[j:j + 1, lanes]
            picked.append(jnp.sum(jnp.where(hit, expert, 0.0), axis=0, keepdims=True))
        e_ref[:, lanes] = jnp.concatenate(picked, axis=0).astype(jnp.int32)

        best = best_ref[:, lanes]
        p = jnp.exp(best - best_ref[0:1, lanes])
        g_ref[:, lanes] = p / jnp.sum(p, axis=0, keepdims=True)


def _topk(qp, sub_keys, pos):
    t = qp.shape[0]
    scratch = [pltpu.VMEM((PEER_TOPK, TOPK_TK), F32) for _ in range(6)]
    return pl.pallas_call(
        _topk_kernel,
        grid=(t // TOPK_TK, PEER_HEADS),
        in_specs=[
            pl.BlockSpec((TOPK_TK, 2 * PEER_HALF), lambda i, h: (i, h)),
            pl.BlockSpec((1, 2, N_SUB_KEYS, PEER_HALF), lambda i, h: (h, 0, 0, 0)),
            pl.BlockSpec((CAND_ROWS, LANES), lambda i, h: (0, 0)),
        ],
        out_specs=[
            pl.BlockSpec((PEER_TOPK, TOPK_TK), lambda i, h: (h, i)),
            pl.BlockSpec((PEER_TOPK, TOPK_TK), lambda i, h: (h, i)),
        ],
        out_shape=[
            jax.ShapeDtypeStruct((N_PICKS, t), jnp.int32),
            jax.ShapeDtypeStruct((N_PICKS, t), F32),
        ],
        scratch_shapes=scratch,
        compiler_params=pltpu.CompilerParams(
            dimension_semantics=("arbitrary", "arbitrary"), vmem_limit_bytes=VMEM_LIMIT),
        name="topk",
    )(qp, sub_keys, pos)


PEER_TM = 512
PEER_CHUNK = 1024
PEER_SLABS = PEER_CHUNK // N_SUB_KEYS
PEER_NCHUNK = (N_SUB_KEYS * N_SUB_KEYS) // PEER_CHUNK
PEER_TG = 16


def _peer_kernel(xn_ref, u_ref, v_ref, e_ref, g_ref, h1_ref, o_ref, hg_ref, et_ref, gt_ref,
                 wa_ref, wb_ref):
    j = pl.program_id(1)

    @pl.when(j == 0)
    def _():
        et_ref[...] = e_ref[...].T
        gt_ref[...] = g_ref[...].T
        sub = lax.broadcasted_iota(jnp.int32, (N_SUB_KEYS, N_PICKS), 0)

        def scatter(gi, w_ref):
            t0 = gi * PEER_TG if isinstance(gi, int) else pl.multiple_of(gi * PEER_TG, PEER_TG)
            for tt in range(PEER_TG):
                e_row = et_ref[pl.ds(t0 + tt, 1), :]
                g_row = gt_ref[pl.ds(t0 + tt, 1), :]
                first = jnp.where(sub == (e_row >> 7), g_row, 0.0).astype(BF16)
                second = jnp.where(sub == (e_row & (N_SUB_KEYS - 1)), 1.0, 0.0).astype(BF16)
                w = _dot_nt(first, second)
                for v in range(N_SUB_KEYS // SUBLANES):
                    w_ref[v, pl.ds(tt * SUBLANES, SUBLANES), :] = w[v * SUBLANES:(v + 1) * SUBLANES, :]

        def relayout(gi, w_ref):
            t0 = gi * PEER_TG if isinstance(gi, int) else pl.multiple_of(gi * PEER_TG, PEER_TG)
            for i1 in range(N_SUB_KEYS):
                v, r = divmod(i1, SUBLANES)
                rows = w_ref[v, pl.ds(r, PEER_TG, stride=SUBLANES), :]
                c, s = divmod(i1, PEER_SLABS)
                hg_ref[c, pl.ds(t0, PEER_TG), s * N_SUB_KEYS:(s + 1) * N_SUB_KEYS] = rows.astype(BF16)

        def pair(k, carry):
            scatter(2 * k + 1, wb_ref)
            relayout(2 * k, wa_ref)
            scatter(2 * k + 2, wa_ref)
            relayout(2 * k + 1, wb_ref)
            return carry

        n_groups = PEER_TM // PEER_TG
        scatter(0, wa_ref)
        lax.fori_loop(0, n_groups // 2 - 1, pair, 0)
        scatter(n_groups - 1, wb_ref)
        relayout(n_groups - 2, wa_ref)
        relayout(n_groups - 1, wb_ref)
        o_ref[...] = h1_ref[...]

    a = _dot_nt(xn_ref[...], u_ref[...])
    h = (_gelu(a) * hg_ref[j].astype(F32)).astype(BF16)
    o_ref[...] += _dot(h, v_ref[...])


def _peer(xn2, u_tab, v_tab, e_t, g_t, h1):
    t, d = xn2.shape
    return pl.pallas_call(
        _peer_kernel,
        grid=(t // PEER_TM, PEER_NCHUNK),
        in_specs=[
            pl.BlockSpec((PEER_TM, d), lambda i, j: (i, 0)),
            pl.BlockSpec((PEER_CHUNK, d), lambda i, j: (j, 0)),
            pl.BlockSpec((PEER_CHUNK, d), lambda i, j: (j, 0)),
            pl.BlockSpec((N_PICKS, PEER_TM), lambda i, j: (0, i)),
            pl.BlockSpec((N_PICKS, PEER_TM), lambda i, j: (0, i)),
            pl.BlockSpec((PEER_TM, d), lambda i, j: (i, 0)),
        ],
        out_specs=pl.BlockSpec((PEER_TM, d), lambda i, j: (i, 0)),
        out_shape=jax.ShapeDtypeStruct((t, d), F32),
        scratch_shapes=[
            pltpu.VMEM((PEER_NCHUNK, PEER_TM, PEER_CHUNK), BF16),
            pltpu.VMEM((PEER_TM, N_PICKS), jnp.int32),
            pltpu.VMEM((PEER_TM, N_PICKS), F32),
            pltpu.VMEM((N_SUB_KEYS // SUBLANES, PEER_TG * SUBLANES, N_SUB_KEYS), F32),
            pltpu.VMEM((N_SUB_KEYS // SUBLANES, PEER_TG * SUBLANES, N_SUB_KEYS), F32),
        ],
        compiler_params=pltpu.CompilerParams(
            dimension_semantics=("arbitrary", "arbitrary"), vmem_limit_bytes=VMEM_LIMIT),
        name="peer",
    )(xn2, u_tab, v_tab, e_t, g_t, h1)


def _s5_params(lam_re, lam_im, log_step, b_re, b_im, c_re, c_im, bsz):
    step = jnp.exp(log_step)[:, None]
    mag = jnp.exp(lam_re * step)
    abar_re = mag * jnp.cos(lam_im * step)
    abar_im = mag * jnp.sin(lam_im * step)
    den = lam_re * lam_re + lam_im * lam_im
    nr, ni = abar_re - 1.0, abar_im
    coef_re = ((nr * lam_re + ni * lam_im) / den)[..., None]
    coef_im = ((ni * lam_re - nr * lam_im) / den)[..., None]
    bbar_re = coef_re * b_re - coef_im * b_im
    bbar_im = coef_re * b_im + coef_im * b_re
    eye = jnp.eye(S5_GROUPS, dtype=F32)
    to_b = lambda m: jnp.einsum('gpc,gh->gchp', m, eye).reshape(S5_WIDTH, S5_LANES)
    to_c = lambda m: jnp.einsum('gcp,gh->gphc', m, eye).reshape(S5_LANES, S5_WIDTH)
    bblk = jnp.concatenate([to_b(bbar_re), to_b(bbar_im)], axis=1).astype(BF16)
    cblk = jnp.concatenate([to_c(c_re), -to_c(c_im)], axis=0).astype(BF16)
    a = jnp.stack([abar_re.reshape(-1), abar_im.reshape(-1)])
    a8 = jnp.broadcast_to(a[:, None, :], (2, bsz, S5_LANES))
    return a8, bblk, cblk


def kernel(x, ln1_g, w_in, b_gate, s5_lambda_re, s5_lambda_im, s5_log_step, s5_b_re, s5_b_im, s5_c_re, s5_c_im, s5_d, s5_w_glu, w_s5_branch, q_norm_g, k_norm_g, rel_bias_table, attn_sinks, w_attn_branch, w_out, ln2_g, peer_w_query, peer_sub_keys, peer_u, peer_v):
    bsz, seq, d = x.shape
    h = x
    for layer in range(ln1_g.shape[0]):
        w_uqkv = w_in[layer, :, :UQKV_WIDTH].astype(BF16)
        w_gate = w_in[layer, :, UQKV_WIDTH:].astype(BF16)
        u_tm, q, k, v = _in_proj(h, ln1_g[layer][None], w_uqkv)

        a8, bblk, cblk = _s5_params(s5_lambda_re[layer], s5_lambda_im[layer], s5_log_step[layer],
                                    s5_b_re[layer], s5_b_im[layer], s5_c_re[layer], s5_c_im[layer], bsz)
        ys5 = _s5(u_tm, a8, bblk, cblk, s5_d[layer].reshape(1, S5_WIDTH),
                  s5_w_glu[layer].astype(BF16), w_s5_branch[layer].astype(BF16), bsz)

        attn = _attention(q, k, v, q_norm_g[layer][None], k_norm_g[layer][None],
                          _t5_bucket_matrix(), rel_bias_table, attn_sinks[layer])

        h1, xn2, qp = _merge(h, attn, ys5, ln1_g[layer][None], w_gate,
                             b_gate[layer][None], w_attn_branch[layer].astype(BF16),
                             w_out[layer].astype(BF16), ln2_g[layer][None],
                             peer_w_query[layer].astype(BF16))

        e_t, g_t = _topk(qp, peer_sub_keys[layer].astype(BF16), jnp.asarray(_CAND_POS))
        out = _peer(xn2, peer_u[layer].astype(BF16), peer_v[layer].astype(BF16), e_t, g_t, h1)
        h = out.reshape(bsz, seq, d)
    return h
```

```python
import functools
import math

import jax
import jax.numpy as jnp
import numpy as np
from jax import lax
from jax.experimental import pallas as pl
from jax.experimental.pallas import tpu as pltpu

F32 = jnp.float32
BF16 = jnp.bfloat16

D_MODEL = 1024
RMS_EPS = 1e-6
NEG_INF = -1e30

S5_WIDTH = 512
S5_GROUP = 16
S5_GROUPS = 32
S5_STATE = 64
S5_LANES = S5_GROUPS * S5_STATE

N_Q_HEADS = 16
N_KV_HEADS = 4
HEAD_DIM = 64
Q_PER_KV = 4
ATTN_BLOCK = 128
N_BUCKETS = 32
MAX_DISTANCE = 128
Q_WIDTH = 1024
KV_WIDTH = 256
UQKV_WIDTH = S5_WIDTH + Q_WIDTH + 2 * KV_WIDTH

PEER_HEADS = 8
PEER_HALF = 64
N_SUB_KEYS = 128
PEER_TOPK = 16
N_PICKS = PEER_HEADS * PEER_TOPK

VMEM_LIMIT = 56 * 1024 * 1024

SQRT_HALF = 0.7071067811865476


def _rms(x, g):
    r = lax.rsqrt(jnp.mean(x * x, axis=-1, keepdims=True) + RMS_EPS)
    return (x * r) * g


def _gelu(x):
    return 0.5 * x * (1.0 + lax.erf(x * SQRT_HALF))


def _dot(a, b):
    return jnp.dot(a, b, preferred_element_type=F32)


def _dot_nt(a, b):
    return lax.dot_general(a, b, (((1,), (1,)), ((), ())), preferred_element_type=F32)


S5_TC = 64


def _in_proj_kernel(bsz, x_ref, g_ref, w_ref, u_ref, q_ref, k_ref, v_ref):
    d = x_ref.shape[-1]
    xn = _rms(x_ref[...].reshape(bsz * S5_TC, d), g_ref[...]).astype(BF16)
    p = _dot(xn, w_ref[...])
    u = p[:, :S5_WIDTH].reshape(bsz, S5_TC, S5_WIDTH)
    u_ref[...] = pltpu.einshape("btc->tbc", u).reshape(S5_TC * bsz, S5_WIDTH)
    q_ref[...] = p[:, S5_WIDTH:S5_WIDTH + Q_WIDTH].astype(BF16).reshape(bsz, S5_TC, Q_WIDTH)
    k_ref[...] = p[:, S5_WIDTH + Q_WIDTH:S5_WIDTH + Q_WIDTH + KV_WIDTH].astype(BF16).reshape(
        bsz, S5_TC, KV_WIDTH)
    v_ref[...] = p[:, S5_WIDTH + Q_WIDTH + KV_WIDTH:].astype(BF16).reshape(bsz, S5_TC, KV_WIDTH)


def _in_proj(x, ln1_g, w_uqkv):
    bsz, seq, d = x.shape
    return pl.pallas_call(
        functools.partial(_in_proj_kernel, bsz),
        grid=(seq // S5_TC,),
        in_specs=[
            pl.BlockSpec((bsz, S5_TC, d), lambda i: (0, i, 0)),
            pl.BlockSpec((1, d), lambda i: (0, 0)),
            pl.BlockSpec((d, UQKV_WIDTH), lambda i: (0, 0)),
        ],
        out_specs=[
            pl.BlockSpec((S5_TC * bsz, S5_WIDTH), lambda i: (i, 0)),
            pl.BlockSpec((bsz, S5_TC, Q_WIDTH), lambda i: (0, i, 0)),
            pl.BlockSpec((bsz, S5_TC, KV_WIDTH), lambda i: (0, i, 0)),
            pl.BlockSpec((bsz, S5_TC, KV_WIDTH), lambda i: (0, i, 0)),
        ],
        out_shape=[
            jax.ShapeDtypeStruct((seq * bsz, S5_WIDTH), F32),
            jax.ShapeDtypeStruct((bsz, seq, Q_WIDTH), BF16),
            jax.ShapeDtypeStruct((bsz, seq, KV_WIDTH), BF16),
            jax.ShapeDtypeStruct((bsz, seq, KV_WIDTH), BF16),
        ],
        compiler_params=pltpu.CompilerParams(
            dimension_semantics=("arbitrary",), vmem_limit_bytes=VMEM_LIMIT),
        name="in_proj",
    )(x, ln1_g, w_uqkv)


S5_LC = 512
S5_UNROLL = 8


def _s5_kernel(bsz, u_ref, a_ref, bblk_ref, cblk_ref, d_ref, wglu_ref, wbr_ref, o_ref,
               h_ref, bu_ref):
    @pl.when(pl.program_id(0) == 0)
    def _():
        h_ref[...] = jnp.zeros_like(h_ref)

    u = u_ref[...]
    bu_ref[...] = _dot(u.astype(BF16), bblk_ref[...])

    for lc in range(S5_LANES // S5_LC):
        re = pl.ds(lc * S5_LC, S5_LC)
        im = pl.ds(S5_LANES + lc * S5_LC, S5_LC)
        ar = a_ref[0, :, re]
        ai = a_ref[1, :, re]

        def group(gi, carry):
            hr, hi = carry
            for s in range(S5_UNROLL):
                rows = pl.ds(pl.multiple_of((gi * S5_UNROLL + s) * bsz, bsz), bsz)
                nr = ar * hr - ai * hi + bu_ref[rows, re]
                ni = ar * hi + ai * hr + bu_ref[rows, im]
                bu_ref[rows, re] = nr
                bu_ref[rows, im] = ni
                hr, hi = nr, ni
            return hr, hi

        hr, hi = lax.fori_loop(0, S5_TC // S5_UNROLL, group, (h_ref[0, :, re], h_ref[1, :, re]))
        h_ref[0, :, re] = hr
        h_ref[1, :, re] = hi

    y = _dot(bu_ref[...].astype(BF16), cblk_ref[...]) + d_ref[...] * u
    yg = _gelu(y).astype(BF16)
    ab = _dot(yg, wglu_ref[...])
    glu = ab[:, :S5_WIDTH] * jax.nn.sigmoid(ab[:, S5_WIDTH:])
    glu = pltpu.einshape("tbc->btc", glu.reshape(S5_TC, bsz, S5_WIDTH)).reshape(bsz * S5_TC, S5_WIDTH)
    o_ref[...] = _dot(glu.astype(BF16), wbr_ref[...]).astype(BF16).reshape(bsz, S5_TC, D_MODEL)


def _s5(u_tm, a8, bblk, cblk, d_row, w_glu, w_br, bsz):
    rows = u_tm.shape[0]
    tr = S5_TC * bsz
    const = lambda shape: pl.BlockSpec(shape, lambda i: (0,) * len(shape))
    return pl.pallas_call(
        functools.partial(_s5_kernel, bsz),
        grid=(rows // tr,),
        in_specs=[
            pl.BlockSpec((tr, S5_WIDTH), lambda i: (i, 0)),
            const((2, bsz, S5_LANES)),
            const((S5_WIDTH, 2 * S5_LANES)),
            const((2 * S5_LANES, S5_WIDTH)),
            const((1, S5_WIDTH)),
            const((S5_WIDTH, 2 * S5_WIDTH)),
            const((S5_WIDTH, D_MODEL)),
        ],
        out_specs=pl.BlockSpec((bsz, S5_TC, D_MODEL), lambda i: (0, i, 0)),
        out_shape=jax.ShapeDtypeStruct((bsz, rows // bsz, D_MODEL), BF16),
        scratch_shapes=[
            pltpu.VMEM((2, bsz, S5_LANES), F32),
            pltpu.VMEM((tr, 2 * S5_LANES), F32),
        ],
        compiler_params=pltpu.CompilerParams(
            dimension_semantics=("arbitrary",), vmem_limit_bytes=VMEM_LIMIT),
        name="s5",
    )(u_tm, a8, bblk, cblk, d_row, w_glu, w_br)


def _attn_kernel(q_ref, kc_ref, kp_ref, vc_ref, vp_ref, qg_ref, kg_ref, bucket_ref,
                 table_ref, sink_ref, o_ref, bias_ref):
    first = jnp.logical_and(pl.program_id(0) == 0, pl.program_id(1) == 0)

    @pl.when(first)
    def _():
        bucket = bucket_ref[...]
        for h in range(N_Q_HEADS):
            acc = jnp.zeros((ATTN_BLOCK, 2 * ATTN_BLOCK), F32)
            for b in range(N_BUCKETS):
                acc = jnp.where(bucket == b, table_ref[b, h], acc)
            bias_ref[h] = acc

    blk = pl.program_id(1)
    qi = lax.broadcasted_iota(jnp.int32, (ATTN_BLOCK, 2 * ATTN_BLOCK), 0)
    si = lax.broadcasted_iota(jnp.int32, (ATTN_BLOCK, 2 * ATTN_BLOCK), 1)
    dist = ATTN_BLOCK + qi - si
    valid = (dist >= 0) & (dist < ATTN_BLOCK) & ((si >= ATTN_BLOCK) | (blk > 0))
    qg = qg_ref[...]
    kg = kg_ref[...]
    scale = HEAD_DIM ** -0.5

    grp_rows = Q_PER_KV * ATTN_BLOCK
    row_head = lax.broadcasted_iota(jnp.int32, (grp_rows, 1), 0) // ATTN_BLOCK

    for kh in range(N_KV_HEADS):
        cols = slice(kh * HEAD_DIM, (kh + 1) * HEAD_DIM)
        kband = jnp.concatenate([kp_ref[0, :, cols], kc_ref[0, :, cols]], axis=0).astype(F32)
        kband = _rms(kband, kg).astype(BF16)
        vband = jnp.concatenate([vp_ref[0, :, cols], vc_ref[0, :, cols]], axis=0)
        h0 = kh * Q_PER_KV
        qs = jnp.concatenate(
            [q_ref[0, :, (h0 + g) * HEAD_DIM:(h0 + g + 1) * HEAD_DIM] for g in range(Q_PER_KV)],
            axis=0).astype(F32)
        qs = (_rms(qs, qg) * scale).astype(BF16)
        s = _dot_nt(qs, kband).reshape(Q_PER_KV, ATTN_BLOCK, 2 * ATTN_BLOCK)
        s = jnp.where(valid[None], s + bias_ref[h0:h0 + Q_PER_KV], NEG_INF).reshape(
            grp_rows, 2 * ATTN_BLOCK)
        sink = jnp.full((grp_rows, 1), sink_ref[h0], F32)
        for g in range(1, Q_PER_KV):
            sink = jnp.where(row_head == g, sink_ref[h0 + g], sink)
        m = jnp.maximum(jnp.max(s, axis=-1, keepdims=True), sink)
        p = jnp.exp(s - m)
        den = jnp.sum(p, axis=-1, keepdims=True) + jnp.exp(sink - m)
        o = (_dot(p.astype(BF16), vband) / den).astype(BF16)
        for g in range(Q_PER_KV):
            o_ref[0, :, (h0 + g) * HEAD_DIM:(h0 + g + 1) * HEAD_DIM] = o[g * ATTN_BLOCK:(g + 1) * ATTN_BLOCK]


def _attention(q, k, v, qg, kg, bucket, table, sinks):
    bsz, seq, _ = q.shape
    nb = seq // ATTN_BLOCK
    cur = lambda b, i: (b, i, 0)
    prev = lambda b, i: (b, jnp.maximum(i - 1, 0), 0)
    const2 = lambda b, i: (0, 0)
    return pl.pallas_call(
        _attn_kernel,
        grid=(bsz, nb),
        in_specs=[
            pl.BlockSpec((1, ATTN_BLOCK, Q_WIDTH), cur),
            pl.BlockSpec((1, ATTN_BLOCK, KV_WIDTH), cur),
            pl.BlockSpec((1, ATTN_BLOCK, KV_WIDTH), prev),
            pl.BlockSpec((1, ATTN_BLOCK, KV_WIDTH), cur),
            pl.BlockSpec((1, ATTN_BLOCK, KV_WIDTH), prev),
            pl.BlockSpec((1, HEAD_DIM), const2),
            pl.BlockSpec((1, HEAD_DIM), const2),
            pl.BlockSpec((ATTN_BLOCK, 2 * ATTN_BLOCK), const2),
            pl.BlockSpec(memory_space=pltpu.SMEM),
            pl.BlockSpec(memory_space=pltpu.SMEM),
        ],
        out_specs=pl.BlockSpec((1, ATTN_BLOCK, Q_WIDTH), cur),
        out_shape=jax.ShapeDtypeStruct((bsz, seq, Q_WIDTH), BF16),
        scratch_shapes=[pltpu.VMEM((N_Q_HEADS, ATTN_BLOCK, 2 * ATTN_BLOCK), F32)],
        compiler_params=pltpu.CompilerParams(
            dimension_semantics=("arbitrary", "arbitrary"), vmem_limit_bytes=VMEM_LIMIT),
        name="attn",
    )(q, k, k, v, v, qg, kg, bucket, table, sinks)


def _t5_bucket_matrix():
    qi = jnp.arange(ATTN_BLOCK)[:, None]
    si = jnp.arange(2 * ATTN_BLOCK)[None, :]
    dist = jnp.maximum(ATTN_BLOCK + qi - si, 0)
    max_exact = N_BUCKETS // 2
    d_f = jnp.maximum(dist, 1).astype(F32)
    large = max_exact + (jnp.log(d_f / max_exact) / math.log(MAX_DISTANCE / max_exact)
                         * (N_BUCKETS - max_exact)).astype(jnp.int32)
    large = jnp.minimum(large, N_BUCKETS - 1)
    return jnp.where(dist < max_exact, dist, large).astype(jnp.int32)


MERGE_TM = 512


def _merge_kernel(x_ref, attn_ref, ys5_ref, g1_ref, wg_ref, bg_ref, wab_ref, wout_ref,
                  g2_ref, wpq_ref, h1_ref, xn2_ref, qp_ref):
    x = x_ref[0]
    xn = _rms(x, g1_ref[...]).astype(BF16)
    gates = jax.nn.sigmoid(_dot(xn, wg_ref[...]) + bg_ref[...])
    y_attn = _dot(attn_ref[0], wab_ref[...])
    y_s5 = ys5_ref[0].astype(F32)
    mix = gates[:, :D_MODEL] * y_s5 + gates[:, D_MODEL:] * y_attn
    h1 = x + _dot(mix.astype(BF16), wout_ref[...])
    h1_ref[...] = h1
    xn2 = _rms(h1, g2_ref[...]).astype(BF16)
    xn2_ref[...] = xn2
    qp_ref[...] = _dot(xn2, wpq_ref[...]).astype(BF16)


def _merge(x, attn, ys5_tm, ln1_g, w_gate, b_gate, w_ab, w_out, ln2_g, w_pq):
    bsz, seq, d = x.shape
    nt = seq // MERGE_TM
    tok = lambda b, i: (b * nt + i, 0)
    const2 = lambda b, i: (0, 0)
    return pl.pallas_call(
        _merge_kernel,
        grid=(bsz, nt),
        in_specs=[
            pl.BlockSpec((1, MERGE_TM, d), lambda b, i: (b, i, 0)),
            pl.BlockSpec((1, MERGE_TM, d), lambda b, i: (b, i, 0)),
            pl.BlockSpec((1, MERGE_TM, d), lambda b, i: (b, i, 0)),
            pl.BlockSpec((1, d), const2),
            pl.BlockSpec((d, 2 * d), const2),
            pl.BlockSpec((1, 2 * d), const2),
            pl.BlockSpec((d, d), const2),
            pl.BlockSpec((d, d), const2),
            pl.BlockSpec((1, d), const2),
            pl.BlockSpec((d, d), const2),
        ],
        out_specs=[
            pl.BlockSpec((MERGE_TM, d), tok),
            pl.BlockSpec((MERGE_TM, d), tok),
            pl.BlockSpec((MERGE_TM, d), tok),
        ],
        out_shape=[
            jax.ShapeDtypeStruct((bsz * seq, d), F32),
            jax.ShapeDtypeStruct((bsz * seq, d), BF16),
            jax.ShapeDtypeStruct((bsz * seq, d), BF16),
        ],
        compiler_params=pltpu.CompilerParams(
            dimension_semantics=("arbitrary", "arbitrary"), vmem_limit_bytes=VMEM_LIMIT),
        name="merge",
    )(x, attn, ys5_tm, ln1_g, w_gate, b_gate, w_ab, w_out, ln2_g, w_pq)


TOPK_TK = 512
LANES = 128
SUBLANES = 8
CAND_ROWS = 80
POS_INVALID = 1.0e9


def _cand_layout():
    pos = np.full((CAND_ROWS,), POS_INVALID, np.float32)
    blocks = [(0, 0), (0, 8), (1, 0), (2, 0), (3, 0), (4, 0), (5, 0), (6, 0), (7, 0)]
    for r, (a, b0) in enumerate(blocks):
        for j in range(8):
            b = b0 + j
            if (a + 1) * (b + 1) <= PEER_TOPK:
                pos[r * 8 + j] = a * PEER_TOPK + b
    for j in range(8):
        pos[72 + j] = (8 + j) * PEER_TOPK
    return blocks, np.broadcast_to(pos[:, None], (CAND_ROWS, LANES)).copy()


_CAND_BLOCKS, _CAND_POS = _cand_layout()
_KEY_IDS = np.broadcast_to(np.arange(N_SUB_KEYS, dtype=np.float32)[:, None], (N_SUB_KEYS, LANES)).copy()


def _extract_round(s, ids, big):
    m = jnp.max(s, axis=0, keepdims=True)
    sel = jnp.min(jnp.where(s == m, ids, big), axis=0, keepdims=True)
    return m, sel, ids == sel


def _topk_stages(sk_ref, chunks, ids_ref, pos_ref, scratch):
    v1_ref, i1_ref, v2_ref, i2_ref, best_ref, s_ref, cand_ref, expert_ref, picked_ref = scratch
    nkeys = float(N_SUB_KEYS)
    for qp, lanes, _ in chunks:
        s_ref[0, :, lanes] = _dot_nt(sk_ref[0, 0], qp[:, :PEER_HALF])
        s_ref[1, :, lanes] = _dot_nt(sk_ref[0, 1], qp[:, PEER_HALF:])
    yield
    for j in range(PEER_TOPK):
        for _, lanes, _ in chunks:
            for half, (val_ref, id_ref) in enumerate(((v1_ref, i1_ref), (v2_ref, i2_ref))):
                s = s_ref[half, :, lanes]
                m, sel, hit = _extract_round(s, ids_ref[...], nkeys)
                val_ref[j:j + 1, lanes] = m
                id_ref[j:j + 1, lanes] = sel
                s_ref[half, :, lanes] = jnp.where(hit, -jnp.inf, s)
        yield

    for _, lanes, _ in chunks:
        for r, (a, b0) in enumerate(_CAND_BLOCKS):
            rows = slice(r * SUBLANES, (r + 1) * SUBLANES)
            cand_ref[rows, lanes] = v1_ref[a:a + 1, lanes] + v2_ref[b0:b0 + 8, lanes]
            expert_ref[rows, lanes] = i1_ref[a:a + 1, lanes] * nkeys + i2_ref[b0:b0 + 8, lanes]
        rows = slice(CAND_ROWS - SUBLANES, CAND_ROWS)
        cand_ref[rows, lanes] = v1_ref[8:16, lanes] + v2_ref[0:1, lanes]
        expert_ref[rows, lanes] = i1_ref[8:16, lanes] * nkeys + i2_ref[0:1, lanes]
        cand_ref[:, lanes] = jnp.where(pos_ref[...] < POS_INVALID, cand_ref[:, lanes], -jnp.inf)
    yield
    for j in range(PEER_TOPK):
        for _, lanes, _ in chunks:
            cand = cand_ref[:, lanes]
            m, _, hit = _extract_round(cand, pos_ref[...], POS_INVALID)
            best_ref[j:j + 1, lanes] = m
            picked_ref[j:j + 1, lanes] = jnp.sum(jnp.where(hit, expert_ref[:, lanes], 0.0), axis=0,
                                                 keepdims=True)
            cand_ref[:, lanes] = jnp.where(hit, -jnp.inf, cand)
        if j % 2 == 1:
            yield
    for _, lanes, store in chunks:
        best = best_ref[:, lanes]
        p = jnp.exp(best - best_ref[0:1, lanes])
        store(picked_ref[:, lanes].astype(jnp.int32), p / jnp.sum(p, axis=0, keepdims=True))
    yield


def _topk_scratch(n_chunks):
    w = n_chunks * LANES
    return ([pltpu.VMEM((PEER_TOPK, w), F32) for _ in range(5)]
            + [pltpu.VMEM((2, N_SUB_KEYS, w), F32), pltpu.VMEM((CAND_ROWS, w), F32),
               pltpu.VMEM((CAND_ROWS, w), F32), pltpu.VMEM((PEER_TOPK, w), F32)])


def _topk_kernel(qp_ref, sk_ref, ids_ref, pos_ref, e_ref, g_ref, *scratch):
    def store(c):
        def fn(e, g):
            e_ref[c] = e
            g_ref[c] = g
        return fn

    chunks = [(qp_ref[pl.ds(c * LANES, LANES), :], pl.ds(c * LANES, LANES), store(c))
              for c in range(TOPK_TK // LANES)]
    for _ in _topk_stages(sk_ref, chunks, ids_ref, pos_ref, scratch):
        pass


def _topk(qp, sub_keys, ids, pos):
    n_lc = TOPK_TK // LANES
    return pl.pallas_call(
        _topk_kernel,
        grid=(1, PEER_HEADS),
        in_specs=[
            pl.BlockSpec((TOPK_TK, 2 * PEER_HALF), lambda i, h: (i, h)),
            pl.BlockSpec((1, 2, N_SUB_KEYS, PEER_HALF), lambda i, h: (h, 0, 0, 0)),
            pl.BlockSpec((N_SUB_KEYS, LANES), lambda i, h: (0, 0)),
            pl.BlockSpec((CAND_ROWS, LANES), lambda i, h: (0, 0)),
        ],
        out_specs=[
            pl.BlockSpec((n_lc, PEER_TOPK, LANES), lambda i, h: (0, h, 0)),
            pl.BlockSpec((n_lc, PEER_TOPK, LANES), lambda i, h: (0, h, 0)),
        ],
        out_shape=[
            jax.ShapeDtypeStruct((n_lc, N_PICKS, LANES), jnp.int32),
            jax.ShapeDtypeStruct((n_lc, N_PICKS, LANES), F32),
        ],
        scratch_shapes=_topk_scratch(n_lc),
        compiler_params=pltpu.CompilerParams(
            dimension_semantics=("arbitrary", "arbitrary"), vmem_limit_bytes=VMEM_LIMIT),
        name="topk",
    )(qp, sub_keys, ids, pos)


PEER_TM = 512
PEER_CHUNK = 1024
PEER_SLABS = PEER_CHUNK // N_SUB_KEYS
PEER_NCHUNK = (N_SUB_KEYS * N_SUB_KEYS) // PEER_CHUNK
PEER_TG = 16


PEER_SUB = 4
PEER_SELECT_LEAD = 3
PEER_SELECT_PER_SLICE = 3
PEER_LC = PEER_TM // LANES
PEER_LC_STEP = PEER_LC * PEER_HEADS // PEER_NCHUNK
assert PEER_NCHUNK * PEER_LC_STEP == PEER_LC * PEER_HEADS and PEER_TM == TOPK_TK


def _peer_kernel(xn_ref, u_ref, v_ref, e0_ref, g0_ref, h1_ref, qpn_ref, sk_ref, ids_ref, pos_ref,
                 o_ref, hg_ref, et_ref, gt_ref, wa_ref, wb_ref, ebuf_ref, gbuf_ref, *topk_scratch):
    i = pl.program_id(0)
    j = pl.program_id(1)
    cur = i % 2

    @pl.when(jnp.logical_and(i == 0, j == 0))
    def _():
        ebuf_ref[0] = e0_ref[...]
        gbuf_ref[0] = g0_ref[...]

    @pl.when(j == 0)
    def _():
        for c in range(PEER_LC):
            et_ref[c * LANES:(c + 1) * LANES, :] = ebuf_ref[cur, c].T
            gt_ref[c * LANES:(c + 1) * LANES, :] = gbuf_ref[cur, c].T
        sub = lax.broadcasted_iota(jnp.int32, (N_SUB_KEYS, N_PICKS), 0)

        def scatter(gi, w_ref):
            t0 = gi * PEER_TG if isinstance(gi, int) else pl.multiple_of(gi * PEER_TG, PEER_TG)
            for tt in range(PEER_TG):
                e_row = et_ref[pl.ds(t0 + tt, 1), :]
                g_row = gt_ref[pl.ds(t0 + tt, 1), :]
                first = jnp.where(sub == (e_row >> 7), g_row, 0.0).astype(BF16)
                second = jnp.where(sub == (e_row & (N_SUB_KEYS - 1)), 1.0, 0.0).astype(BF16)
                w = _dot_nt(first, second)
                for v in range(N_SUB_KEYS // SUBLANES):
                    w_ref[v, pl.ds(tt * SUBLANES, SUBLANES), :] = w[v * SUBLANES:(v + 1) * SUBLANES, :]

        def relayout(gi, w_ref):
            t0 = gi * PEER_TG if isinstance(gi, int) else pl.multiple_of(gi * PEER_TG, PEER_TG)
            for i1 in range(N_SUB_KEYS):
                v, r = divmod(i1, SUBLANES)
                rows = w_ref[v, pl.ds(r, PEER_TG, stride=SUBLANES), :]
                c, s = divmod(i1, PEER_SLABS)
                hg_ref[c, pl.ds(t0, PEER_TG), s * N_SUB_KEYS:(s + 1) * N_SUB_KEYS] = rows.astype(BF16)

        def pair(k, carry):
            scatter(2 * k + 1, wb_ref)
            relayout(2 * k, wa_ref)
            scatter(2 * k + 2, wa_ref)
            relayout(2 * k + 1, wb_ref)
            return carry

        n_groups = PEER_TM // PEER_TG
        scatter(0, wa_ref)
        lax.fori_loop(0, n_groups // 2 - 1, pair, 0)
        scatter(n_groups - 1, wb_ref)
        relayout(n_groups - 2, wa_ref)
        relayout(n_groups - 1, wb_ref)
        o_ref[...] = h1_ref[...]

    def expert_stages():
        xn = xn_ref[...]
        sub = PEER_CHUNK // PEER_SUB
        hs = []
        for q in range(PEER_SUB):
            es = slice(q * sub, (q + 1) * sub)
            a = _dot_nt(xn, u_ref[es, :])
            hs.append((_gelu(a) * hg_ref[j, :, es].astype(F32)).astype(BF16))
            yield
        h = jnp.concatenate(hs, axis=1)
        nsub = D_MODEL // PEER_SUB
        for q in range(PEER_SUB):
            ns = slice(q * nsub, (q + 1) * nsub)
            o_ref[:, ns] += _dot(h, v_ref[:, ns])
            yield

    head_rows = pl.ds(pl.multiple_of((j * PEER_LC_STEP // PEER_LC) * PEER_TOPK, PEER_TOPK), PEER_TOPK)

    def store(lc):
        def fn(e, g):
            ebuf_ref[1 - cur, lc, head_rows, :] = e
            gbuf_ref[1 - cur, lc, head_rows, :] = g
        return fn

    chunks = []
    for c in range(PEER_LC_STEP):
        lc = (j * PEER_LC_STEP) % PEER_LC + c
        qp = qpn_ref[pl.ds(pl.multiple_of(lc * LANES, LANES), LANES), :]
        chunks.append((qp, pl.ds(c * LANES, LANES), store(lc)))
    select = _topk_stages(sk_ref, chunks, ids_ref, pos_ref, topk_scratch)
    experts = expert_stages()
    for _ in range(PEER_SELECT_LEAD):
        next(select, None)
    for _ in experts:
        for _ in range(PEER_SELECT_PER_SLICE):
            next(select, None)
    for _ in select:
        pass


def _peer(xn2, u_tab, v_tab, e0, g0, h1, qp, sub_keys, ids, pos):
    t, d = xn2.shape
    nblk = t // PEER_TM
    steps_per_head = PEER_LC // PEER_LC_STEP
    const3 = lambda i, j: (0, 0, 0)
    return pl.pallas_call(
        _peer_kernel,
        grid=(nblk, PEER_NCHUNK),
        in_specs=[
            pl.BlockSpec((PEER_TM, d), lambda i, j: (i, 0)),
            pl.BlockSpec((PEER_CHUNK, d), lambda i, j: (j, 0)),
            pl.BlockSpec((PEER_CHUNK, d), lambda i, j: (j, 0)),
            pl.BlockSpec((PEER_LC, N_PICKS, LANES), const3),
            pl.BlockSpec((PEER_LC, N_PICKS, LANES), const3),
            pl.BlockSpec((PEER_TM, d), lambda i, j: (i, 0)),
            pl.BlockSpec((PEER_TM, 2 * PEER_HALF),
                         lambda i, j: (jnp.minimum(i + 1, nblk - 1), j // steps_per_head)),
            pl.BlockSpec((1, 2, N_SUB_KEYS, PEER_HALF), lambda i, j: (j // steps_per_head, 0, 0, 0)),
            pl.BlockSpec((N_SUB_KEYS, LANES), lambda i, j: (0, 0)),
            pl.BlockSpec((CAND_ROWS, LANES), lambda i, j: (0, 0)),
        ],
        out_specs=pl.BlockSpec((PEER_TM, d), lambda i, j: (i, 0)),
        out_shape=jax.ShapeDtypeStruct((t, d), F32),
        scratch_shapes=[
            pltpu.VMEM((PEER_NCHUNK, PEER_TM, PEER_CHUNK), BF16),
            pltpu.VMEM((PEER_TM, N_PICKS), jnp.int32),
            pltpu.VMEM((PEER_TM, N_PICKS), F32),
            pltpu.VMEM((N_SUB_KEYS // SUBLANES, PEER_TG * SUBLANES, N_SUB_KEYS), F32),
            pltpu.VMEM((N_SUB_KEYS // SUBLANES, PEER_TG * SUBLANES, N_SUB_KEYS), F32),
            pltpu.VMEM((2, PEER_LC, N_PICKS, LANES), jnp.int32),
            pltpu.VMEM((2, PEER_LC, N_PICKS, LANES), F32),
        ] + _topk_scratch(PEER_LC_STEP),
        compiler_params=pltpu.CompilerParams(
            dimension_semantics=("arbitrary", "arbitrary"), vmem_limit_bytes=VMEM_LIMIT),
        name="peer",
    )(xn2, u_tab, v_tab, e0, g0, h1, qp, sub_keys, ids, pos)


def _s5_params(lam_re, lam_im, log_step, b_re, b_im, c_re, c_im, bsz):
    step = jnp.exp(log_step)[:, None]
    mag = jnp.exp(lam_re * step)
    abar_re = mag * jnp.cos(lam_im * step)
    abar_im = mag * jnp.sin(lam_im * step)
    den = lam_re * lam_re + lam_im * lam_im
    nr, ni = abar_re - 1.0, abar_im
    coef_re = ((nr * lam_re + ni * lam_im) / den)[..., None]
    coef_im = ((ni * lam_re - nr * lam_im) / den)[..., None]
    bbar_re = coef_re * b_re - coef_im * b_im
    bbar_im = coef_re * b_im + coef_im * b_re
    eye = jnp.eye(S5_GROUPS, dtype=F32)
    to_b = lambda m: jnp.einsum('gpc,gh->gchp', m, eye).reshape(S5_WIDTH, S5_LANES)
    to_c = lambda m: jnp.einsum('gcp,gh->gphc', m, eye).reshape(S5_LANES, S5_WIDTH)
    bblk = jnp.concatenate([to_b(bbar_re), to_b(bbar_im)], axis=1).astype(BF16)
    cblk = jnp.concatenate([to_c(c_re), -to_c(c_im)], axis=0).astype(BF16)
    a = jnp.stack([abar_re.reshape(-1), abar_im.reshape(-1)])
    a8 = jnp.broadcast_to(a[:, None, :], (2, bsz, S5_LANES))
    return a8, bblk, cblk


def kernel(x, ln1_g, w_in, b_gate, s5_lambda_re, s5_lambda_im, s5_log_step, s5_b_re, s5_b_im, s5_c_re, s5_c_im, s5_d, s5_w_glu, w_s5_branch, q_norm_g, k_norm_g, rel_bias_table, attn_sinks, w_attn_branch, w_out, ln2_g, peer_w_query, peer_sub_keys, peer_u, peer_v):
    bsz, seq, d = x.shape
    h = x
    for layer in range(ln1_g.shape[0]):
        w_uqkv = w_in[layer, :, :UQKV_WIDTH].astype(BF16)
        w_gate = w_in[layer, :, UQKV_WIDTH:].astype(BF16)
        u_tm, q, k, v = _in_proj(h, ln1_g[layer][None], w_uqkv)

        a8, bblk, cblk = _s5_params(s5_lambda_re[layer], s5_lambda_im[layer], s5_log_step[layer],
                                    s5_b_re[layer], s5_b_im[layer], s5_c_re[layer], s5_c_im[layer], bsz)
        ys5 = _s5(u_tm, a8, bblk, cblk, s5_d[layer].reshape(1, S5_WIDTH),
                  s5_w_glu[layer].astype(BF16), w_s5_branch[layer].astype(BF16), bsz)

        attn = _attention(q, k, v, q_norm_g[layer][None], k_norm_g[layer][None],
                          _t5_bucket_matrix(), rel_bias_table, attn_sinks[layer])

        h1, xn2, qp = _merge(h, attn, ys5, ln1_g[layer][None], w_gate,
                             b_gate[layer][None], w_attn_branch[layer].astype(BF16),
                             w_out[layer].astype(BF16), ln2_g[layer][None],
                             peer_w_query[layer].astype(BF16))

        sub_keys = peer_sub_keys[layer].astype(BF16)
        pos = jnp.asarray(_CAND_POS)
        ids = jnp.asarray(_KEY_IDS)
        e0, g0 = _topk(qp, sub_keys, ids, pos)
        out = _peer(xn2, peer_u[layer].astype(BF16), peer_v[layer].astype(BF16), e0, g0, h1,
                    qp, sub_keys, ids, pos)
        h = out.reshape(bsz, seq, d)
    return h
```

```python
import functools
import math

import jax
import jax.numpy as jnp
import numpy as np
from jax import lax
from jax.experimental import pallas as pl
from jax.experimental.pallas import tpu as pltpu

F32 = jnp.float32
BF16 = jnp.bfloat16

D_MODEL = 1024
RMS_EPS = 1e-6
NEG_INF = -1e30

S5_WIDTH = 512
S5_GROUP = 16
S5_GROUPS = 32
S5_STATE = 64
S5_LANES = S5_GROUPS * S5_STATE

N_Q_HEADS = 16
N_KV_HEADS = 4
HEAD_DIM = 64
Q_PER_KV = 4
ATTN_BLOCK = 128
N_BUCKETS = 32
MAX_DISTANCE = 128
Q_WIDTH = 1024
KV_WIDTH = 256
UQKV_WIDTH = S5_WIDTH + Q_WIDTH + 2 * KV_WIDTH

PEER_HEADS = 8
PEER_HALF = 64
N_SUB_KEYS = 128
PEER_TOPK = 16
N_PICKS = PEER_HEADS * PEER_TOPK

VMEM_LIMIT = 56 * 1024 * 1024

SQRT_HALF = 0.7071067811865476


def _rms(x, g):
    r = lax.rsqrt(jnp.mean(x * x, axis=-1, keepdims=True) + RMS_EPS)
    return (x * r) * g


def _gelu(x):
    return 0.5 * x * (1.0 + lax.erf(x * SQRT_HALF))


def _dot(a, b):
    return jnp.dot(a, b, preferred_element_type=F32)


def _dot_nt(a, b):
    return lax.dot_general(a, b, (((1,), (1,)), ((), ())), preferred_element_type=F32)


S5_TC = 64


def _in_proj_kernel(bsz, x_ref, g_ref, w_ref, u_ref, q_ref, k_ref, v_ref):
    d = x_ref.shape[-1]
    xn = _rms(x_ref[...].reshape(bsz * S5_TC, d), g_ref[...]).astype(BF16)
    p = _dot(xn, w_ref[...])
    u = p[:, :S5_WIDTH].reshape(bsz, S5_TC, S5_WIDTH)
    u_ref[...] = pltpu.einshape("btc->tbc", u).reshape(S5_TC * bsz, S5_WIDTH)
    q_ref[...] = p[:, S5_WIDTH:S5_WIDTH + Q_WIDTH].astype(BF16).reshape(bsz, S5_TC, Q_WIDTH)
    k_ref[...] = p[:, S5_WIDTH + Q_WIDTH:S5_WIDTH + Q_WIDTH + KV_WIDTH].astype(BF16).reshape(
        bsz, S5_TC, KV_WIDTH)
    v_ref[...] = p[:, S5_WIDTH + Q_WIDTH + KV_WIDTH:].astype(BF16).reshape(bsz, S5_TC, KV_WIDTH)


def _in_proj(x, ln1_g, w_uqkv):
    bsz, seq, d = x.shape
    return pl.pallas_call(
        functools.partial(_in_proj_kernel, bsz),
        grid=(seq // S5_TC,),
        in_specs=[
            pl.BlockSpec((bsz, S5_TC, d), lambda i: (0, i, 0)),
            pl.BlockSpec((1, d), lambda i: (0, 0)),
            pl.BlockSpec((d, UQKV_WIDTH), lambda i: (0, 0)),
        ],
        out_specs=[
            pl.BlockSpec((S5_TC * bsz, S5_WIDTH), lambda i: (i, 0)),
            pl.BlockSpec((bsz, S5_TC, Q_WIDTH), lambda i: (0, i, 0)),
            pl.BlockSpec((bsz, S5_TC, KV_WIDTH), lambda i: (0, i, 0)),
            pl.BlockSpec((bsz, S5_TC, KV_WIDTH), lambda i: (0, i, 0)),
        ],
        out_shape=[
            jax.ShapeDtypeStruct((seq * bsz, S5_WIDTH), F32),
            jax.ShapeDtypeStruct((bsz, seq, Q_WIDTH), BF16),
            jax.ShapeDtypeStruct((bsz, seq, KV_WIDTH), BF16),
            jax.ShapeDtypeStruct((bsz, seq, KV_WIDTH), BF16),
        ],
        compiler_params=pltpu.CompilerParams(
            dimension_semantics=("arbitrary",), vmem_limit_bytes=VMEM_LIMIT),
        name="in_proj",
    )(x, ln1_g, w_uqkv)


S5_LC = 512
S5_UNROLL = 8


def _s5_kernel(bsz, u_ref, a_ref, bblk_ref, cblk_ref, d_ref, wglu_ref, wbr_ref, o_ref,
               h_ref, bu_ref):
    @pl.when(pl.program_id(0) == 0)
    def _():
        h_ref[...] = jnp.zeros_like(h_ref)

    u = u_ref[...]
    bu_ref[...] = _dot(u.astype(BF16), bblk_ref[...])

    for lc in range(S5_LANES // S5_LC):
        re = pl.ds(lc * S5_LC, S5_LC)
        im = pl.ds(S5_LANES + lc * S5_LC, S5_LC)
        ar = a_ref[0, :, re]
        ai = a_ref[1, :, re]

        def group(gi, carry):
            hr, hi = carry
            for s in range(S5_UNROLL):
                rows = pl.ds(pl.multiple_of((gi * S5_UNROLL + s) * bsz, bsz), bsz)
                nr = ar * hr - ai * hi + bu_ref[rows, re]
                ni = ar * hi + ai * hr + bu_ref[rows, im]
                bu_ref[rows, re] = nr
                bu_ref[rows, im] = ni
                hr, hi = nr, ni
            return hr, hi

        hr, hi = lax.fori_loop(0, S5_TC // S5_UNROLL, group, (h_ref[0, :, re], h_ref[1, :, re]))
        h_ref[0, :, re] = hr
        h_ref[1, :, re] = hi

    y = _dot(bu_ref[...].astype(BF16), cblk_ref[...]) + d_ref[...] * u
    yg = _gelu(y).astype(BF16)
    ab = _dot(yg, wglu_ref[...])
    glu = ab[:, :S5_WIDTH] * jax.nn.sigmoid(ab[:, S5_WIDTH:])
    glu = pltpu.einshape("tbc->btc", glu.reshape(S5_TC, bsz, S5_WIDTH)).reshape(bsz * S5_TC, S5_WIDTH)
    o_ref[...] = _dot(glu.astype(BF16), wbr_ref[...]).astype(BF16).reshape(bsz, S5_TC, D_MODEL)


def _s5(u_tm, a8, bblk, cblk, d_row, w_glu, w_br, bsz):
    rows = u_tm.shape[0]
    tr = S5_TC * bsz
    const = lambda shape: pl.BlockSpec(shape, lambda i: (0,) * len(shape))
    return pl.pallas_call(
        functools.partial(_s5_kernel, bsz),
        grid=(rows // tr,),
        in_specs=[
            pl.BlockSpec((tr, S5_WIDTH), lambda i: (i, 0)),
            const((2, bsz, S5_LANES)),
            const((S5_WIDTH, 2 * S5_LANES)),
            const((2 * S5_LANES, S5_WIDTH)),
            const((1, S5_WIDTH)),
            const((S5_WIDTH, 2 * S5_WIDTH)),
            const((S5_WIDTH, D_MODEL)),
        ],
        out_specs=pl.BlockSpec((bsz, S5_TC, D_MODEL), lambda i: (0, i, 0)),
        out_shape=jax.ShapeDtypeStruct((bsz, rows // bsz, D_MODEL), BF16),
        scratch_shapes=[
            pltpu.VMEM((2, bsz, S5_LANES), F32),
            pltpu.VMEM((tr, 2 * S5_LANES), F32),
        ],
        compiler_params=pltpu.CompilerParams(
            dimension_semantics=("arbitrary",), vmem_limit_bytes=VMEM_LIMIT),
        name="s5",
    )(u_tm, a8, bblk, cblk, d_row, w_glu, w_br)


def _attn_kernel(q_ref, kc_ref, kp_ref, vc_ref, vp_ref, qg_ref, kg_ref, bucket_ref,
                 table_ref, sink_ref, o_ref, bias_ref):
    first = jnp.logical_and(pl.program_id(0) == 0, pl.program_id(1) == 0)

    @pl.when(first)
    def _():
        bucket = bucket_ref[...]
        for h in range(N_Q_HEADS):
            acc = jnp.zeros((ATTN_BLOCK, 2 * ATTN_BLOCK), F32)
            for b in range(N_BUCKETS):
                acc = jnp.where(bucket == b, table_ref[b, h], acc)
            bias_ref[h] = acc

    blk = pl.program_id(1)
    qi = lax.broadcasted_iota(jnp.int32, (ATTN_BLOCK, 2 * ATTN_BLOCK), 0)
    si = lax.broadcasted_iota(jnp.int32, (ATTN_BLOCK, 2 * ATTN_BLOCK), 1)
    dist = ATTN_BLOCK + qi - si
    valid = (dist >= 0) & (dist < ATTN_BLOCK) & ((si >= ATTN_BLOCK) | (blk > 0))
    qg = qg_ref[...]
    kg = kg_ref[...]
    scale = HEAD_DIM ** -0.5

    grp_rows = Q_PER_KV * ATTN_BLOCK
    row_head = lax.broadcasted_iota(jnp.int32, (grp_rows, 1), 0) // ATTN_BLOCK

    for kh in range(N_KV_HEADS):
        cols = slice(kh * HEAD_DIM, (kh + 1) * HEAD_DIM)
        kband = jnp.concatenate([kp_ref[0, :, cols], kc_ref[0, :, cols]], axis=0).astype(F32)
        kband = _rms(kband, kg).astype(BF16)
        vband = jnp.concatenate([vp_ref[0, :, cols], vc_ref[0, :, cols]], axis=0)
        h0 = kh * Q_PER_KV
        qs = jnp.concatenate(
            [q_ref[0, :, (h0 + g) * HEAD_DIM:(h0 + g + 1) * HEAD_DIM] for g in range(Q_PER_KV)],
            axis=0).astype(F32)
        qs = (_rms(qs, qg) * scale).astype(BF16)
        s = _dot_nt(qs, kband).reshape(Q_PER_KV, ATTN_BLOCK, 2 * ATTN_BLOCK)
        s = jnp.where(valid[None], s + bias_ref[h0:h0 + Q_PER_KV], NEG_INF).reshape(
            grp_rows, 2 * ATTN_BLOCK)
        sink = jnp.full((grp_rows, 1), sink_ref[h0], F32)
        for g in range(1, Q_PER_KV):
            sink = jnp.where(row_head == g, sink_ref[h0 + g], sink)
        m = jnp.maximum(jnp.max(s, axis=-1, keepdims=True), sink)
        p = jnp.exp(s - m)
        den = jnp.sum(p, axis=-1, keepdims=True) + jnp.exp(sink - m)
        o = (_dot(p.astype(BF16), vband) / den).astype(BF16)
        for g in range(Q_PER_KV):
            o_ref[0, :, (h0 + g) * HEAD_DIM:(h0 + g + 1) * HEAD_DIM] = o[g * ATTN_BLOCK:(g + 1) * ATTN_BLOCK]


def _attention(q, k, v, qg, kg, bucket, table, sinks):
    bsz, seq, _ = q.shape
    nb = seq // ATTN_BLOCK
    cur = lambda b, i: (b, i, 0)
    prev = lambda b, i: (b, jnp.maximum(i - 1, 0), 0)
    const2 = lambda b, i: (0, 0)
    return pl.pallas_call(
        _attn_kernel,
        grid=(bsz, nb),
        in_specs=[
            pl.BlockSpec((1, ATTN_BLOCK, Q_WIDTH), cur),
            pl.BlockSpec((1, ATTN_BLOCK, KV_WIDTH), cur),
            pl.BlockSpec((1, ATTN_BLOCK, KV_WIDTH), prev),
            pl.BlockSpec((1, ATTN_BLOCK, KV_WIDTH), cur),
            pl.BlockSpec((1, ATTN_BLOCK, KV_WIDTH), prev),
            pl.BlockSpec((1, HEAD_DIM), const2),
            pl.BlockSpec((1, HEAD_DIM), const2),
            pl.BlockSpec((ATTN_BLOCK, 2 * ATTN_BLOCK), const2),
            pl.BlockSpec(memory_space=pltpu.SMEM),
            pl.BlockSpec(memory_space=pltpu.SMEM),
        ],
        out_specs=pl.BlockSpec((1, ATTN_BLOCK, Q_WIDTH), cur),
        out_shape=jax.ShapeDtypeStruct((bsz, seq, Q_WIDTH), BF16),
        scratch_shapes=[pltpu.VMEM((N_Q_HEADS, ATTN_BLOCK, 2 * ATTN_BLOCK), F32)],
        compiler_params=pltpu.CompilerParams(
            dimension_semantics=("arbitrary", "arbitrary"), vmem_limit_bytes=VMEM_LIMIT),
        name="attn",
    )(q, k, k, v, v, qg, kg, bucket, table, sinks)


def _t5_bucket_matrix():
    qi = jnp.arange(ATTN_BLOCK)[:, None]
    si = jnp.arange(2 * ATTN_BLOCK)[None, :]
    dist = jnp.maximum(ATTN_BLOCK + qi - si, 0)
    max_exact = N_BUCKETS // 2
    d_f = jnp.maximum(dist, 1).astype(F32)
    large = max_exact + (jnp.log(d_f / max_exact) / math.log(MAX_DISTANCE / max_exact)
                         * (N_BUCKETS - max_exact)).astype(jnp.int32)
    large = jnp.minimum(large, N_BUCKETS - 1)
    return jnp.where(dist < max_exact, dist, large).astype(jnp.int32)


MERGE_TM = 512


def _merge_kernel(x_ref, attn_ref, ys5_ref, g1_ref, wg_ref, bg_ref, wab_ref, wout_ref,
                  g2_ref, wpq_ref, h1_ref, xn2_ref, qp_ref):
    x = x_ref[0]
    xn = _rms(x, g1_ref[...]).astype(BF16)
    gates = jax.nn.sigmoid(_dot(xn, wg_ref[...]) + bg_ref[...])
    y_attn = _dot(attn_ref[0], wab_ref[...])
    y_s5 = ys5_ref[0].astype(F32)
    mix = gates[:, :D_MODEL] * y_s5 + gates[:, D_MODEL:] * y_attn
    h1 = x + _dot(mix.astype(BF16), wout_ref[...])
    h1_ref[...] = h1
    xn2 = _rms(h1, g2_ref[...]).astype(BF16)
    xn2_ref[...] = xn2
    qp_ref[...] = _dot(xn2, wpq_ref[...]).astype(BF16)


def _merge(x, attn, ys5_tm, ln1_g, w_gate, b_gate, w_ab, w_out, ln2_g, w_pq):
    bsz, seq, d = x.shape
    nt = seq // MERGE_TM
    tok = lambda b, i: (b * nt + i, 0)
    const2 = lambda b, i: (0, 0)
    return pl.pallas_call(
        _merge_kernel,
        grid=(bsz, nt),
        in_specs=[
            pl.BlockSpec((1, MERGE_TM, d), lambda b, i: (b, i, 0)),
            pl.BlockSpec((1, MERGE_TM, d), lambda b, i: (b, i, 0)),
            pl.BlockSpec((1, MERGE_TM, d), lambda b, i: (b, i, 0)),
            pl.BlockSpec((1, d), const2),
            pl.BlockSpec((d, 2 * d), const2),
            pl.BlockSpec((1, 2 * d), const2),
            pl.BlockSpec((d, d), const2),
            pl.BlockSpec((d, d), const2),
            pl.BlockSpec((1, d), const2),
            pl.BlockSpec((d, d), const2),
        ],
        out_specs=[
            pl.BlockSpec((MERGE_TM, d), tok),
            pl.BlockSpec((MERGE_TM, d), tok),
            pl.BlockSpec((MERGE_TM, d), tok),
        ],
        out_shape=[
            jax.ShapeDtypeStruct((bsz * seq, d), F32),
            jax.ShapeDtypeStruct((bsz * seq, d), BF16),
            jax.ShapeDtypeStruct((bsz * seq, d), BF16),
        ],
        compiler_params=pltpu.CompilerParams(
            dimension_semantics=("arbitrary", "arbitrary"), vmem_limit_bytes=VMEM_LIMIT),
        name="merge",
    )(x, attn, ys5_tm, ln1_g, w_gate, b_gate, w_ab, w_out, ln2_g, w_pq)


TOPK_TK = 512
LANES = 128
SUBLANES = 8
CAND_ROWS = 80
POS_INVALID = 1.0e9


def _cand_layout():
    pos = np.full((CAND_ROWS,), POS_INVALID, np.float32)
    blocks = [(0, 0), (0, 8), (1, 0), (2, 0), (3, 0), (4, 0), (5, 0), (6, 0), (7, 0)]
    for r, (a, b0) in enumerate(blocks):
        for j in range(8):
            b = b0 + j
            if (a + 1) * (b + 1) <= PEER_TOPK:
                pos[r * 8 + j] = a * PEER_TOPK + b
    for j in range(8):
        pos[72 + j] = (8 + j) * PEER_TOPK
    return blocks, np.broadcast_to(pos[:, None], (CAND_ROWS, LANES)).copy()


_CAND_BLOCKS, _CAND_POS = _cand_layout()
_KEY_IDS = np.broadcast_to(np.arange(N_SUB_KEYS, dtype=np.float32)[:, None], (N_SUB_KEYS, LANES)).copy()


def _extract_round(s, ids, big):
    m = jnp.max(s, axis=0, keepdims=True)
    sel = jnp.min(jnp.where(s == m, ids, big), axis=0, keepdims=True)
    return m, sel, ids == sel


def _topk_stages(sk_ref, chunks, ids_ref, pos_ref, scratch):
    v1_ref, i1_ref, v2_ref, i2_ref, best_ref, s_ref, cand_ref, expert_ref, picked_ref = scratch
    nkeys = float(N_SUB_KEYS)
    tile = slice(0, SUBLANES)
    for qp, lanes, _ in chunks:
        s_ref[0, :, lanes] = _dot_nt(sk_ref[0, 0], qp[:, :PEER_HALF])
        s_ref[1, :, lanes] = _dot_nt(sk_ref[0, 1], qp[:, PEER_HALF:])
    live = [(s_ref, (half, tile, lanes)) for _, lanes, _ in chunks for half in range(2)]
    yield live
    for j in range(PEER_TOPK):
        for _, lanes, _ in chunks:
            for half, (val_ref, id_ref) in enumerate(((v1_ref, i1_ref), (v2_ref, i2_ref))):
                s = s_ref[half, :, lanes]
                m, sel, hit = _extract_round(s, ids_ref[...], nkeys)
                val_ref[j:j + 1, lanes] = m
                id_ref[j:j + 1, lanes] = sel
                s_ref[half, :, lanes] = jnp.where(hit, -jnp.inf, s)
        yield live

    for _, lanes, _ in chunks:
        for r, (a, b0) in enumerate(_CAND_BLOCKS):
            rows = slice(r * SUBLANES, (r + 1) * SUBLANES)
            cand_ref[rows, lanes] = v1_ref[a:a + 1, lanes] + v2_ref[b0:b0 + 8, lanes]
            expert_ref[rows, lanes] = i1_ref[a:a + 1, lanes] * nkeys + i2_ref[b0:b0 + 8, lanes]
        rows = slice(CAND_ROWS - SUBLANES, CAND_ROWS)
        cand_ref[rows, lanes] = v1_ref[8:16, lanes] + v2_ref[0:1, lanes]
        expert_ref[rows, lanes] = i1_ref[8:16, lanes] * nkeys + i2_ref[0:1, lanes]
        cand_ref[:, lanes] = jnp.where(pos_ref[...] < POS_INVALID, cand_ref[:, lanes], -jnp.inf)
    live = [(cand_ref, (tile, lanes)) for _, lanes, _ in chunks]
    yield live
    for j in range(PEER_TOPK):
        for _, lanes, _ in chunks:
            cand = cand_ref[:, lanes]
            m, _, hit = _extract_round(cand, pos_ref[...], POS_INVALID)
            best_ref[j:j + 1, lanes] = m
            picked_ref[j:j + 1, lanes] = jnp.sum(jnp.where(hit, expert_ref[:, lanes], 0.0), axis=0,
                                                 keepdims=True)
            cand_ref[:, lanes] = jnp.where(hit, -jnp.inf, cand)
        if j % 2 == 1:
            yield live
    for _, lanes, store in chunks:
        best = best_ref[:, lanes]
        p = jnp.exp(best - best_ref[0:1, lanes])
        store(picked_ref[:, lanes].astype(jnp.int32), p / jnp.sum(p, axis=0, keepdims=True))
    yield []


def _order_after(live, value_tile):
    zero = jnp.minimum(jnp.abs(value_tile), 0.0)
    for ref, idx in live:
        ref[idx] = ref[idx] + zero


def _topk_scratch(n_chunks):
    w = n_chunks * LANES
    return ([pltpu.VMEM((PEER_TOPK, w), F32) for _ in range(5)]
            + [pltpu.VMEM((2, N_SUB_KEYS, w), F32), pltpu.VMEM((CAND_ROWS, w), F32),
               pltpu.VMEM((CAND_ROWS, w), F32), pltpu.VMEM((PEER_TOPK, w), F32)])


def _topk_kernel(qp_ref, sk_ref, ids_ref, pos_ref, e_ref, g_ref, *scratch):
    def store(c):
        def fn(e, g):
            e_ref[c] = e
            g_ref[c] = g
        return fn

    chunks = [(qp_ref[pl.ds(c * LANES, LANES), :], pl.ds(c * LANES, LANES), store(c))
              for c in range(TOPK_TK // LANES)]
    for _ in _topk_stages(sk_ref, chunks, ids_ref, pos_ref, scratch):
        pass


def _topk(qp, sub_keys, ids, pos):
    n_lc = TOPK_TK // LANES
    return pl.pallas_call(
        _topk_kernel,
        grid=(1, PEER_HEADS),
        in_specs=[
            pl.BlockSpec((TOPK_TK, 2 * PEER_HALF), lambda i, h: (i, h)),
            pl.BlockSpec((1, 2, N_SUB_KEYS, PEER_HALF), lambda i, h: (h, 0, 0, 0)),
            pl.BlockSpec((N_SUB_KEYS, LANES), lambda i, h: (0, 0)),
            pl.BlockSpec((CAND_ROWS, LANES), lambda i, h: (0, 0)),
        ],
        out_specs=[
            pl.BlockSpec((n_lc, PEER_TOPK, LANES), lambda i, h: (0, h, 0)),
            pl.BlockSpec((n_lc, PEER_TOPK, LANES), lambda i, h: (0, h, 0)),
        ],
        out_shape=[
            jax.ShapeDtypeStruct((n_lc, N_PICKS, LANES), jnp.int32),
            jax.ShapeDtypeStruct((n_lc, N_PICKS, LANES), F32),
        ],
        scratch_shapes=_topk_scratch(n_lc),
        compiler_params=pltpu.CompilerParams(
            dimension_semantics=("arbitrary", "arbitrary"), vmem_limit_bytes=VMEM_LIMIT),
        name="topk",
    )(qp, sub_keys, ids, pos)


PEER_TM = 512
PEER_CHUNK = 2048
PEER_SLABS = PEER_CHUNK // N_SUB_KEYS
PEER_NCHUNK = (N_SUB_KEYS * N_SUB_KEYS) // PEER_CHUNK
PEER_TG = 16


PEER_SUB = 2
PEER_SELECT_LEAD = 3
PEER_SELECT_PER_STAGE = 6
PEER_LC = PEER_TM // LANES
PEER_LC_STEP = PEER_LC * PEER_HEADS // PEER_NCHUNK
assert PEER_NCHUNK * PEER_LC_STEP == PEER_LC * PEER_HEADS and PEER_TM == TOPK_TK


def _peer_kernel(xn_ref, u_ref, v_ref, e0_ref, g0_ref, h1_ref, qpn_ref, sk_ref, ids_ref, pos_ref,
                 o_ref, hg_ref, et_ref, gt_ref, wa_ref, wb_ref, ebuf_ref, gbuf_ref, *topk_scratch):
    i = pl.program_id(0)
    j = pl.program_id(1)
    cur = i % 2

    @pl.when(jnp.logical_and(i == 0, j == 0))
    def _():
        ebuf_ref[0] = e0_ref[...]
        gbuf_ref[0] = g0_ref[...]

    @pl.when(j == 0)
    def _():
        for c in range(PEER_LC):
            et_ref[c * LANES:(c + 1) * LANES, :] = ebuf_ref[cur, c].T
            gt_ref[c * LANES:(c + 1) * LANES, :] = gbuf_ref[cur, c].T
        sub = lax.broadcasted_iota(jnp.int32, (N_SUB_KEYS, N_PICKS), 0)

        def scatter(gi, w_ref):
            t0 = gi * PEER_TG if isinstance(gi, int) else pl.multiple_of(gi * PEER_TG, PEER_TG)
            for tt in range(PEER_TG):
                e_row = et_ref[pl.ds(t0 + tt, 1), :]
                g_row = gt_ref[pl.ds(t0 + tt, 1), :]
                first = jnp.where(sub == (e_row >> 7), g_row, 0.0).astype(BF16)
                second = jnp.where(sub == (e_row & (N_SUB_KEYS - 1)), 1.0, 0.0).astype(BF16)
                w = _dot_nt(first, second)
                for v in range(N_SUB_KEYS // SUBLANES):
                    w_ref[v, pl.ds(tt * SUBLANES, SUBLANES), :] = w[v * SUBLANES:(v + 1) * SUBLANES, :]

        def relayout(gi, w_ref):
            t0 = gi * PEER_TG if isinstance(gi, int) else pl.multiple_of(gi * PEER_TG, PEER_TG)
            for i1 in range(N_SUB_KEYS):
                v, r = divmod(i1, SUBLANES)
                rows = w_ref[v, pl.ds(r, PEER_TG, stride=SUBLANES), :]
                c, s = divmod(i1, PEER_SLABS)
                hg_ref[c, pl.ds(t0, PEER_TG), s * N_SUB_KEYS:(s + 1) * N_SUB_KEYS] = rows.astype(BF16)

        def pair(k, carry):
            scatter(2 * k + 1, wb_ref)
            relayout(2 * k, wa_ref)
            scatter(2 * k + 2, wa_ref)
            relayout(2 * k + 1, wb_ref)
            return carry

        n_groups = PEER_TM // PEER_TG
        scatter(0, wa_ref)
        lax.fori_loop(0, n_groups // 2 - 1, pair, 0)
        scatter(n_groups - 1, wb_ref)
        relayout(n_groups - 2, wa_ref)
        relayout(n_groups - 1, wb_ref)
        o_ref[...] = h1_ref[...]

    def expert_stages():
        xn = xn_ref[...]
        sub = PEER_CHUNK // PEER_SUB
        hs = []
        for q in range(PEER_SUB):
            es = slice(q * sub, (q + 1) * sub)
            a = _dot_nt(xn, u_ref[es, :])
            hs.append((_gelu(a) * hg_ref[j, :, es].astype(F32)).astype(BF16))
            yield a[:SUBLANES, :LANES]
        for q in range(PEER_SUB):
            es = slice(q * sub, (q + 1) * sub)
            r = _dot(hs[q], v_ref[es, :])
            o_ref[...] += r
            yield r[:SUBLANES, :LANES]

    head_rows = pl.ds(pl.multiple_of((j * PEER_LC_STEP // PEER_LC) * PEER_TOPK, PEER_TOPK), PEER_TOPK)

    def store(lc):
        def fn(e, g):
            ebuf_ref[1 - cur, lc, head_rows, :] = e
            gbuf_ref[1 - cur, lc, head_rows, :] = g
        return fn

    chunks = []
    for c in range(PEER_LC_STEP):
        lc = (j * PEER_LC_STEP) % PEER_LC + c
        qp = qpn_ref[pl.ds(pl.multiple_of(lc * LANES, LANES), LANES), :]
        chunks.append((qp, pl.ds(c * LANES, LANES), store(lc)))
    select = _topk_stages(sk_ref, chunks, ids_ref, pos_ref, topk_scratch)
    experts = expert_stages()
    live = []
    for _ in range(PEER_SELECT_LEAD):
        live = next(select, [])
    for result_tile in experts:
        _order_after(live, result_tile)
        for _ in range(PEER_SELECT_PER_STAGE):
            live = next(select, [])
    for _ in select:
        pass


def _peer(xn2, u_tab, v_tab, e0, g0, h1, qp, sub_keys, ids, pos):
    t, d = xn2.shape
    nblk = t // PEER_TM
    steps_per_head = PEER_LC // PEER_LC_STEP
    const3 = lambda i, j: (0, 0, 0)
    return pl.pallas_call(
        _peer_kernel,
        grid=(nblk, PEER_NCHUNK),
        in_specs=[
            pl.BlockSpec((PEER_TM, d), lambda i, j: (i, 0)),
            pl.BlockSpec((PEER_CHUNK, d), lambda i, j: (j, 0)),
            pl.BlockSpec((PEER_CHUNK, d), lambda i, j: (j, 0)),
            pl.BlockSpec((PEER_LC, N_PICKS, LANES), const3),
            pl.BlockSpec((PEER_LC, N_PICKS, LANES), const3),
            pl.BlockSpec((PEER_TM, d), lambda i, j: (i, 0)),
            pl.BlockSpec((PEER_TM, 2 * PEER_HALF),
                         lambda i, j: (jnp.minimum(i + 1, nblk - 1), j // steps_per_head)),
            pl.BlockSpec((1, 2, N_SUB_KEYS, PEER_HALF), lambda i, j: (j // steps_per_head, 0, 0, 0)),
            pl.BlockSpec((N_SUB_KEYS, LANES), lambda i, j: (0, 0)),
            pl.BlockSpec((CAND_ROWS, LANES), lambda i, j: (0, 0)),
        ],
        out_specs=pl.BlockSpec((PEER_TM, d), lambda i, j: (i, 0)),
        out_shape=jax.ShapeDtypeStruct((t, d), F32),
        scratch_shapes=[
            pltpu.VMEM((PEER_NCHUNK, PEER_TM, PEER_CHUNK), BF16),
            pltpu.VMEM((PEER_TM, N_PICKS), jnp.int32),
            pltpu.VMEM((PEER_TM, N_PICKS), F32),
            pltpu.VMEM((N_SUB_KEYS // SUBLANES, PEER_TG * SUBLANES, N_SUB_KEYS), F32),
            pltpu.VMEM((N_SUB_KEYS // SUBLANES, PEER_TG * SUBLANES, N_SUB_KEYS), F32),
            pltpu.VMEM((2, PEER_LC, N_PICKS, LANES), jnp.int32),
            pltpu.VMEM((2, PEER_LC, N_PICKS, LANES), F32),
        ] + _topk_scratch(PEER_LC_STEP),
        compiler_params=pltpu.CompilerParams(
            dimension_semantics=("arbitrary", "arbitrary"), vmem_limit_bytes=VMEM_LIMIT),
        name="peer",
    )(xn2, u_tab, v_tab, e0, g0, h1, qp, sub_keys, ids, pos)


def _s5_params(lam_re, lam_im, log_step, b_re, b_im, c_re, c_im, bsz):
    step = jnp.exp(log_step)[:, None]
    mag = jnp.exp(lam_re * step)
    abar_re = mag * jnp.cos(lam_im * step)
    abar_im = mag * jnp.sin(lam_im * step)
    den = lam_re * lam_re + lam_im * lam_im
    nr, ni = abar_re - 1.0, abar_im
    coef_re = ((nr * lam_re + ni * lam_im) / den)[..., None]
    coef_im = ((ni * lam_re - nr * lam_im) / den)[..., None]
    bbar_re = coef_re * b_re - coef_im * b_im
    bbar_im = coef_re * b_im + coef_im * b_re
    eye = jnp.eye(S5_GROUPS, dtype=F32)
    to_b = lambda m: jnp.einsum('gpc,gh->gchp', m, eye).reshape(S5_WIDTH, S5_LANES)
    to_c = lambda m: jnp.einsum('gcp,gh->gphc', m, eye).reshape(S5_LANES, S5_WIDTH)
    bblk = jnp.concatenate([to_b(bbar_re), to_b(bbar_im)], axis=1).astype(BF16)
    cblk = jnp.concatenate([to_c(c_re), -to_c(c_im)], axis=0).astype(BF16)
    a = jnp.stack([abar_re.reshape(-1), abar_im.reshape(-1)])
    a8 = jnp.broadcast_to(a[:, None, :], (2, bsz, S5_LANES))
    return a8, bblk, cblk


def kernel(x, ln1_g, w_in, b_gate, s5_lambda_re, s5_lambda_im, s5_log_step, s5_b_re, s5_b_im, s5_c_re, s5_c_im, s5_d, s5_w_glu, w_s5_branch, q_norm_g, k_norm_g, rel_bias_table, attn_sinks, w_attn_branch, w_out, ln2_g, peer_w_query, peer_sub_keys, peer_u, peer_v):
    bsz, seq, d = x.shape
    h = x
    for layer in range(ln1_g.shape[0]):
        w_uqkv = w_in[layer, :, :UQKV_WIDTH].astype(BF16)
        w_gate = w_in[layer, :, UQKV_WIDTH:].astype(BF16)
        u_tm, q, k, v = _in_proj(h, ln1_g[layer][None], w_uqkv)

        a8, bblk, cblk = _s5_params(s5_lambda_re[layer], s5_lambda_im[layer], s5_log_step[layer],
                                    s5_b_re[layer], s5_b_im[layer], s5_c_re[layer], s5_c_im[layer], bsz)
        ys5 = _s5(u_tm, a8, bblk, cblk, s5_d[layer].reshape(1, S5_WIDTH),
                  s5_w_glu[layer].astype(BF16), w_s5_branch[layer].astype(BF16), bsz)

        attn = _attention(q, k, v, q_norm_g[layer][None], k_norm_g[layer][None],
                          _t5_bucket_matrix(), rel_bias_table, attn_sinks[layer])

        h1, xn2, qp = _merge(h, attn, ys5, ln1_g[layer][None], w_gate,
                             b_gate[layer][None], w_attn_branch[layer].astype(BF16),
                             w_out[layer].astype(BF16), ln2_g[layer][None],
                             peer_w_query[layer].astype(BF16))

        sub_keys = peer_sub_keys[layer].astype(BF16)
        pos = jnp.asarray(_CAND_POS)
        ids = jnp.asarray(_KEY_IDS)
        e0, g0 = _topk(qp, sub_keys, ids, pos)
        out = _peer(xn2, peer_u[layer].astype(BF16), peer_v[layer].astype(BF16), e0, g0, h1,
                    qp, sub_keys, ids, pos)
        h = out.reshape(bsz, seq, d)
    return h
```

```python
import functools
import math

import jax
import jax.numpy as jnp
import numpy as np
from jax import lax
from jax.experimental import pallas as pl
from jax.experimental.pallas import tpu as pltpu

F32 = jnp.float32
BF16 = jnp.bfloat16

D_MODEL = 1024
RMS_EPS = 1e-6
NEG_INF = -1e30

S5_WIDTH = 512
S5_GROUP = 16
S5_GROUPS = 32
S5_STATE = 64
S5_LANES = S5_GROUPS * S5_STATE

N_Q_HEADS = 16
N_KV_HEADS = 4
HEAD_DIM = 64
Q_PER_KV = 4
ATTN_BLOCK = 128
N_BUCKETS = 32
MAX_DISTANCE = 128
Q_WIDTH = 1024
KV_WIDTH = 256
UQKV_WIDTH = S5_WIDTH + Q_WIDTH + 2 * KV_WIDTH

PEER_HEADS = 8
PEER_HALF = 64
N_SUB_KEYS = 128
PEER_TOPK = 16
N_PICKS = PEER_HEADS * PEER_TOPK

VMEM_LIMIT = 56 * 1024 * 1024

SQRT_HALF = 0.7071067811865476


def _rms(x, g):
    r = lax.rsqrt(jnp.mean(x * x, axis=-1, keepdims=True) + RMS_EPS)
    return (x * r) * g


def _gelu(x):
    return 0.5 * x * (1.0 + lax.erf(x * SQRT_HALF))


def _dot(a, b):
    return jnp.dot(a, b, preferred_element_type=F32)


def _dot_nt(a, b):
    return lax.dot_general(a, b, (((1,), (1,)), ((), ())), preferred_element_type=F32)


S5_TC = 64


def _in_proj_kernel(bsz, x_ref, g_ref, w_ref, u_ref, q_ref, k_ref, v_ref):
    d = x_ref.shape[-1]
    xn = _rms(x_ref[...].reshape(bsz * S5_TC, d), g_ref[...]).astype(BF16)
    p = _dot(xn, w_ref[...])
    u = p[:, :S5_WIDTH].reshape(bsz, S5_TC, S5_WIDTH)
    u_ref[...] = jnp.transpose(u, (1, 0, 2)).reshape(S5_TC * bsz, S5_WIDTH)
    q_ref[...] = p[:, S5_WIDTH:S5_WIDTH + Q_WIDTH].astype(BF16).reshape(bsz, S5_TC, Q_WIDTH)
    k_ref[...] = p[:, S5_WIDTH + Q_WIDTH:S5_WIDTH + Q_WIDTH + KV_WIDTH].astype(BF16).reshape(
        bsz, S5_TC, KV_WIDTH)
    v_ref[...] = p[:, S5_WIDTH + Q_WIDTH + KV_WIDTH:].astype(BF16).reshape(bsz, S5_TC, KV_WIDTH)


def _in_proj(x, ln1_g, w_uqkv):
    bsz, seq, d = x.shape
    return pl.pallas_call(
        functools.partial(_in_proj_kernel, bsz),
        grid=(seq // S5_TC,),
        in_specs=[
            pl.BlockSpec((bsz, S5_TC, d), lambda i: (0, i, 0)),
            pl.BlockSpec((1, d), lambda i: (0, 0)),
            pl.BlockSpec((d, UQKV_WIDTH), lambda i: (0, 0)),
        ],
        out_specs=[
            pl.BlockSpec((S5_TC * bsz, S5_WIDTH), lambda i: (i, 0)),
            pl.BlockSpec((bsz, S5_TC, Q_WIDTH), lambda i: (0, i, 0)),
            pl.BlockSpec((bsz, S5_TC, KV_WIDTH), lambda i: (0, i, 0)),
            pl.BlockSpec((bsz, S5_TC, KV_WIDTH), lambda i: (0, i, 0)),
        ],
        out_shape=[
            jax.ShapeDtypeStruct((seq * bsz, S5_WIDTH), F32),
            jax.ShapeDtypeStruct((bsz, seq, Q_WIDTH), BF16),
            jax.ShapeDtypeStruct((bsz, seq, KV_WIDTH), BF16),
            jax.ShapeDtypeStruct((bsz, seq, KV_WIDTH), BF16),
        ],
        compiler_params=pltpu.CompilerParams(
            dimension_semantics=("arbitrary",), vmem_limit_bytes=VMEM_LIMIT),
        name="in_proj",
    )(x, ln1_g, w_uqkv)


S5_LC = 512
S5_UNROLL = 8


def _s5_kernel(bsz, u_ref, a_ref, bblk_ref, cblk_ref, d_ref, wglu_ref, wbr_ref, o_ref,
               h_ref, bu_ref):
    @pl.when(pl.program_id(0) == 0)
    def _():
        h_ref[...] = jnp.zeros_like(h_ref)

    u = u_ref[...]
    bu_ref[...] = _dot(u.astype(BF16), bblk_ref[...])

    for lc in range(S5_LANES // S5_LC):
        re = pl.ds(lc * S5_LC, S5_LC)
        im = pl.ds(S5_LANES + lc * S5_LC, S5_LC)
        ar = a_ref[0, :, re]
        ai = a_ref[1, :, re]

        def group(gi, carry):
            hr, hi = carry
            for s in range(S5_UNROLL):
                rows = pl.ds(pl.multiple_of((gi * S5_UNROLL + s) * bsz, bsz), bsz)
                nr = ar * hr - ai * hi + bu_ref[rows, re]
                ni = ar * hi + ai * hr + bu_ref[rows, im]
                bu_ref[rows, re] = nr
                bu_ref[rows, im] = ni
                hr, hi = nr, ni
            return hr, hi

        hr, hi = lax.fori_loop(0, S5_TC // S5_UNROLL, group, (h_ref[0, :, re], h_ref[1, :, re]))
        h_ref[0, :, re] = hr
        h_ref[1, :, re] = hi

    y = _dot(bu_ref[...].astype(BF16), cblk_ref[...]) + d_ref[...] * u
    yg = _gelu(y).astype(BF16)
    ab = _dot(yg, wglu_ref[...])
    glu = ab[:, :S5_WIDTH] * jax.nn.sigmoid(ab[:, S5_WIDTH:])
    glu = jnp.transpose(glu.reshape(S5_TC, bsz, S5_WIDTH), (1, 0, 2)).reshape(bsz * S5_TC, S5_WIDTH)
    o_ref[...] = _dot(glu.astype(BF16), wbr_ref[...]).astype(BF16).reshape(bsz, S5_TC, D_MODEL)


def _s5(u_tm, a8, bblk, cblk, d_row, w_glu, w_br, bsz):
    rows = u_tm.shape[0]
    tr = S5_TC * bsz
    const = lambda shape: pl.BlockSpec(shape, lambda i: (0,) * len(shape))
    return pl.pallas_call(
        functools.partial(_s5_kernel, bsz),
        grid=(rows // tr,),
        in_specs=[
            pl.BlockSpec((tr, S5_WIDTH), lambda i: (i, 0)),
            const((2, bsz, S5_LANES)),
            const((S5_WIDTH, 2 * S5_LANES)),
            const((2 * S5_LANES, S5_WIDTH)),
            const((1, S5_WIDTH)),
            const((S5_WIDTH, 2 * S5_WIDTH)),
            const((S5_WIDTH, D_MODEL)),
        ],
        out_specs=pl.BlockSpec((bsz, S5_TC, D_MODEL), lambda i: (0, i, 0)),
        out_shape=jax.ShapeDtypeStruct((bsz, rows // bsz, D_MODEL), BF16),
        scratch_shapes=[
            pltpu.VMEM((2, bsz, S5_LANES), F32),
            pltpu.VMEM((tr, 2 * S5_LANES), F32),
        ],
        compiler_params=pltpu.CompilerParams(
            dimension_semantics=("arbitrary",), vmem_limit_bytes=VMEM_LIMIT),
        name="s5",
    )(u_tm, a8, bblk, cblk, d_row, w_glu, w_br)


def _attn_kernel(q_ref, kc_ref, kp_ref, vc_ref, vp_ref, qg_ref, kg_ref, bucket_ref,
                 table_ref, sink_ref, o_ref, bias_ref):
    first = jnp.logical_and(pl.program_id(0) == 0, pl.program_id(1) == 0)

    @pl.when(first)
    def _():
        bucket = bucket_ref[...]
        for h in range(N_Q_HEADS):
            acc = jnp.zeros((ATTN_BLOCK, 2 * ATTN_BLOCK), F32)
            for b in range(N_BUCKETS):
                acc = jnp.where(bucket == b, table_ref[b, h], acc)
            bias_ref[h] = acc

    blk = pl.program_id(1)
    qi = lax.broadcasted_iota(jnp.int32, (ATTN_BLOCK, 2 * ATTN_BLOCK), 0)
    si = lax.broadcasted_iota(jnp.int32, (ATTN_BLOCK, 2 * ATTN_BLOCK), 1)
    dist = ATTN_BLOCK + qi - si
    valid = (dist >= 0) & (dist < ATTN_BLOCK) & ((si >= ATTN_BLOCK) | (blk > 0))
    qg = qg_ref[...]
    kg = kg_ref[...]
    scale = HEAD_DIM ** -0.5

    grp_rows = Q_PER_KV * ATTN_BLOCK
    row_head = lax.broadcasted_iota(jnp.int32, (grp_rows, 1), 0) // ATTN_BLOCK

    ones_d = jnp.ones((HEAD_DIM, LANES), BF16)
    ones_s = jnp.ones((2 * ATTN_BLOCK, LANES), BF16)

    def rms_rows(x, g):
        ssq = _dot((x * x).astype(BF16), ones_d)[:, :HEAD_DIM]
        return (x * lax.rsqrt(ssq * (1.0 / HEAD_DIM) + RMS_EPS)) * g

    for kh in range(N_KV_HEADS):
        cols = slice(kh * HEAD_DIM, (kh + 1) * HEAD_DIM)
        kband = jnp.concatenate([kp_ref[0, :, cols], kc_ref[0, :, cols]], axis=0).astype(F32)
        kband = rms_rows(kband, kg).astype(BF16)
        vband = jnp.concatenate([vp_ref[0, :, cols], vc_ref[0, :, cols]], axis=0)
        h0 = kh * Q_PER_KV
        qs = jnp.concatenate(
            [q_ref[0, :, (h0 + g) * HEAD_DIM:(h0 + g + 1) * HEAD_DIM] for g in range(Q_PER_KV)],
            axis=0).astype(F32)
        qs = (rms_rows(qs, qg) * scale).astype(BF16)
        s = _dot_nt(qs, kband).reshape(Q_PER_KV, ATTN_BLOCK, 2 * ATTN_BLOCK)
        s = jnp.where(valid[None], s + bias_ref[h0:h0 + Q_PER_KV], NEG_INF).reshape(
            grp_rows, 2 * ATTN_BLOCK)
        sink = jnp.full((grp_rows, 1), sink_ref[h0], F32)
        for g in range(1, Q_PER_KV):
            sink = jnp.where(row_head == g, sink_ref[h0 + g], sink)
        m = jnp.maximum(jnp.max(s, axis=-1, keepdims=True), sink)
        p = jnp.exp(s - m).astype(BF16)
        den = _dot(p, ones_s)[:, :HEAD_DIM] + jnp.exp(sink - m)
        o = (_dot(p, vband) / den).astype(BF16)
        for g in range(Q_PER_KV):
            o_ref[0, :, (h0 + g) * HEAD_DIM:(h0 + g + 1) * HEAD_DIM] = o[g * ATTN_BLOCK:(g + 1) * ATTN_BLOCK]


def _attention(q, k, v, qg, kg, bucket, table, sinks):
    bsz, seq, _ = q.shape
    nb = seq // ATTN_BLOCK
    cur = lambda b, i: (b, i, 0)
    prev = lambda b, i: (b, jnp.maximum(i - 1, 0), 0)
    const2 = lambda b, i: (0, 0)
    return pl.pallas_call(
        _attn_kernel,
        grid=(bsz, nb),
        in_specs=[
            pl.BlockSpec((1, ATTN_BLOCK, Q_WIDTH), cur),
            pl.BlockSpec((1, ATTN_BLOCK, KV_WIDTH), cur),
            pl.BlockSpec((1, ATTN_BLOCK, KV_WIDTH), prev),
            pl.BlockSpec((1, ATTN_BLOCK, KV_WIDTH), cur),
            pl.BlockSpec((1, ATTN_BLOCK, KV_WIDTH), prev),
            pl.BlockSpec((1, HEAD_DIM), const2),
            pl.BlockSpec((1, HEAD_DIM), const2),
            pl.BlockSpec((ATTN_BLOCK, 2 * ATTN_BLOCK), const2),
            pl.BlockSpec(memory_space=pltpu.SMEM),
            pl.BlockSpec(memory_space=pltpu.SMEM),
        ],
        out_specs=pl.BlockSpec((1, ATTN_BLOCK, Q_WIDTH), cur),
        out_shape=jax.ShapeDtypeStruct((bsz, seq, Q_WIDTH), BF16),
        scratch_shapes=[pltpu.VMEM((N_Q_HEADS, ATTN_BLOCK, 2 * ATTN_BLOCK), F32)],
        compiler_params=pltpu.CompilerParams(
            dimension_semantics=("arbitrary", "arbitrary"), vmem_limit_bytes=VMEM_LIMIT),
        name="attn",
    )(q, k, k, v, v, qg, kg, bucket, table, sinks)


def _t5_bucket_matrix():
    qi = np.arange(ATTN_BLOCK)[:, None]
    si = np.arange(2 * ATTN_BLOCK)[None, :]
    dist = np.maximum(ATTN_BLOCK + qi - si, 0)
    max_exact = N_BUCKETS // 2
    d_f = np.maximum(dist, 1).astype(np.float64)
    large = max_exact + np.floor(np.log(d_f / max_exact) / math.log(MAX_DISTANCE / max_exact)
                                 * (N_BUCKETS - max_exact)).astype(np.int32)
    large = np.minimum(large, N_BUCKETS - 1)
    return jnp.asarray(np.where(dist < max_exact, dist, large).astype(np.int32))


MERGE_TM = 512


def _merge_kernel(x_ref, attn_ref, ys5_ref, g1_ref, wg_ref, bg_ref, wab_ref, wout_ref,
                  g2_ref, wpq_ref, h1_ref, xn2_ref, qp_ref):
    x = x_ref[0]
    xn = _rms(x, g1_ref[...]).astype(BF16)
    gates = jax.nn.sigmoid(_dot(xn, wg_ref[...]) + bg_ref[...])
    y_attn = _dot(attn_ref[0], wab_ref[...])
    y_s5 = ys5_ref[0].astype(F32)
    mix = gates[:, :D_MODEL] * y_s5 + gates[:, D_MODEL:] * y_attn
    h1 = x + _dot(mix.astype(BF16), wout_ref[...])
    h1_ref[...] = h1
    xn2 = _rms(h1, g2_ref[...]).astype(BF16)
    xn2_ref[...] = xn2
    qp_ref[...] = _dot(xn2, wpq_ref[...]).astype(BF16)


def _merge(x, attn, ys5_tm, ln1_g, w_gate, b_gate, w_ab, w_out, ln2_g, w_pq):
    bsz, seq, d = x.shape
    nt = seq // MERGE_TM
    tok = lambda b, i: (b * nt + i, 0)
    const2 = lambda b, i: (0, 0)
    return pl.pallas_call(
        _merge_kernel,
        grid=(bsz, nt),
        in_specs=[
            pl.BlockSpec((1, MERGE_TM, d), lambda b, i: (b, i, 0)),
            pl.BlockSpec((1, MERGE_TM, d), lambda b, i: (b, i, 0)),
            pl.BlockSpec((1, MERGE_TM, d), lambda b, i: (b, i, 0)),
            pl.BlockSpec((1, d), const2),
            pl.BlockSpec((d, 2 * d), const2),
            pl.BlockSpec((1, 2 * d), const2),
            pl.BlockSpec((d, d), const2),
            pl.BlockSpec((d, d), const2),
            pl.BlockSpec((1, d), const2),
            pl.BlockSpec((d, d), const2),
        ],
        out_specs=[
            pl.BlockSpec((MERGE_TM, d), tok),
            pl.BlockSpec((MERGE_TM, d), tok),
            pl.BlockSpec((MERGE_TM, d), tok),
        ],
        out_shape=[
            jax.ShapeDtypeStruct((bsz * seq, d), F32),
            jax.ShapeDtypeStruct((bsz * seq, d), BF16),
            jax.ShapeDtypeStruct((bsz * seq, d), BF16),
        ],
        compiler_params=pltpu.CompilerParams(
            dimension_semantics=("arbitrary", "arbitrary"), vmem_limit_bytes=VMEM_LIMIT),
        name="merge",
    )(x, attn, ys5_tm, ln1_g, w_gate, b_gate, w_ab, w_out, ln2_g, w_pq)


TOPK_TK = 512
LANES = 128
SUBLANES = 8
CAND_ROWS = 80
POS_INVALID = 1.0e9


def _cand_layout():
    pos = np.full((CAND_ROWS,), POS_INVALID, np.float32)
    blocks = [(0, 0), (0, 8), (1, 0), (2, 0), (3, 0), (4, 0), (5, 0), (6, 0), (7, 0)]
    for r, (a, b0) in enumerate(blocks):
        for j in range(8):
            b = b0 + j
            if (a + 1) * (b + 1) <= PEER_TOPK:
                pos[r * 8 + j] = a * PEER_TOPK + b
    for j in range(8):
        pos[72 + j] = (8 + j) * PEER_TOPK
    return blocks, np.broadcast_to(pos[:, None], (CAND_ROWS, LANES)).copy()


_CAND_BLOCKS, _CAND_POS = _cand_layout()
_KEY_IDS = np.broadcast_to(np.arange(N_SUB_KEYS, dtype=np.float32)[:, None], (N_SUB_KEYS, LANES)).copy()


def _extract_round(s, ids, big):
    m = jnp.max(s, axis=0, keepdims=True)
    sel = jnp.min(jnp.where(s == m, ids, big), axis=0, keepdims=True)
    return m, sel, ids == sel


def _topk_stages(sk_ref, chunks, ids_ref, pos_ref, scratch):
    v1_ref, i1_ref, v2_ref, i2_ref, best_ref, s_ref, cand_ref, expert_ref, picked_ref = scratch
    nkeys = float(N_SUB_KEYS)
    tile = slice(0, SUBLANES)
    for qp, lanes, _ in chunks:
        s_ref[0, :, lanes] = _dot_nt(sk_ref[0, 0], qp[:, :PEER_HALF])
        s_ref[1, :, lanes] = _dot_nt(sk_ref[0, 1], qp[:, PEER_HALF:])
    live = [(s_ref, (half, tile, lanes)) for _, lanes, _ in chunks for half in range(2)]
    yield live
    for j in range(PEER_TOPK):
        for _, lanes, _ in chunks:
            for half, (val_ref, id_ref) in enumerate(((v1_ref, i1_ref), (v2_ref, i2_ref))):
                s = s_ref[half, :, lanes]
                m, sel, hit = _extract_round(s, ids_ref[...], nkeys)
                val_ref[j:j + 1, lanes] = m
                id_ref[j:j + 1, lanes] = sel
                s_ref[half, :, lanes] = jnp.where(hit, -jnp.inf, s)
        yield live

    for _, lanes, _ in chunks:
        for r, (a, b0) in enumerate(_CAND_BLOCKS):
            rows = slice(r * SUBLANES, (r + 1) * SUBLANES)
            cand_ref[rows, lanes] = v1_ref[a:a + 1, lanes] + v2_ref[b0:b0 + 8, lanes]
            expert_ref[rows, lanes] = i1_ref[a:a + 1, lanes] * nkeys + i2_ref[b0:b0 + 8, lanes]
        rows = slice(CAND_ROWS - SUBLANES, CAND_ROWS)
        cand_ref[rows, lanes] = v1_ref[8:16, lanes] + v2_ref[0:1, lanes]
        expert_ref[rows, lanes] = i1_ref[8:16, lanes] * nkeys + i2_ref[0:1, lanes]
        cand_ref[:, lanes] = jnp.where(pos_ref[...] < POS_INVALID, cand_ref[:, lanes], -jnp.inf)
    live = [(cand_ref, (tile, lanes)) for _, lanes, _ in chunks]
    yield live
    for j in range(PEER_TOPK):
        for _, lanes, _ in chunks:
            cand = cand_ref[:, lanes]
            m, _, hit = _extract_round(cand, pos_ref[...], POS_INVALID)
            best_ref[j:j + 1, lanes] = m
            picked_ref[j:j + 1, lanes] = jnp.sum(jnp.where(hit, expert_ref[:, lanes], 0.0), axis=0,
                                                 keepdims=True)
            cand_ref[:, lanes] = jnp.where(hit, -jnp.inf, cand)
        if j % 2 == 1:
            yield live
    for _, lanes, store in chunks:
        best = best_ref[:, lanes]
        p = jnp.exp(best - best_ref[0:1, lanes])
        store(picked_ref[:, lanes].astype(jnp.int32), p / jnp.sum(p, axis=0, keepdims=True))
    yield []


def _order_after(live, value_tile):
    zero = jnp.minimum(jnp.abs(value_tile), 0.0)
    for ref, idx in live:
        ref[idx] = ref[idx] + zero


def _topk_scratch(n_chunks):
    w = n_chunks * LANES
    return ([pltpu.VMEM((PEER_TOPK, w), F32) for _ in range(5)]
            + [pltpu.VMEM((2, N_SUB_KEYS, w), F32), pltpu.VMEM((CAND_ROWS, w), F32),
               pltpu.VMEM((CAND_ROWS, w), F32), pltpu.VMEM((PEER_TOPK, w), F32)])


def _topk_kernel(qp_ref, sk_ref, ids_ref, pos_ref, e_ref, g_ref, *scratch):
    def store(c):
        def fn(e, g):
            e_ref[c] = e
            g_ref[c] = g
        return fn

    chunks = [(qp_ref[pl.ds(c * LANES, LANES), :], pl.ds(c * LANES, LANES), store(c))
              for c in range(TOPK_TK // LANES)]
    for _ in _topk_stages(sk_ref, chunks, ids_ref, pos_ref, scratch):
        pass


def _topk(qp, sub_keys, ids, pos):
    n_lc = TOPK_TK // LANES
    return pl.pallas_call(
        _topk_kernel,
        grid=(1, PEER_HEADS),
        in_specs=[
            pl.BlockSpec((TOPK_TK, 2 * PEER_HALF), lambda i, h: (i, h)),
            pl.BlockSpec((1, 2, N_SUB_KEYS, PEER_HALF), lambda i, h: (h, 0, 0, 0)),
            pl.BlockSpec((N_SUB_KEYS, LANES), lambda i, h: (0, 0)),
            pl.BlockSpec((CAND_ROWS, LANES), lambda i, h: (0, 0)),
        ],
        out_specs=[
            pl.BlockSpec((n_lc, PEER_TOPK, LANES), lambda i, h: (0, h, 0)),
            pl.BlockSpec((n_lc, PEER_TOPK, LANES), lambda i, h: (0, h, 0)),
        ],
        out_shape=[
            jax.ShapeDtypeStruct((n_lc, N_PICKS, LANES), jnp.int32),
            jax.ShapeDtypeStruct((n_lc, N_PICKS, LANES), F32),
        ],
        scratch_shapes=_topk_scratch(n_lc),
        compiler_params=pltpu.CompilerParams(
            dimension_semantics=("arbitrary", "arbitrary"), vmem_limit_bytes=VMEM_LIMIT),
        name="topk",
    )(qp, sub_keys, ids, pos)


PEER_TM = 512
PEER_CHUNK = 2048
PEER_SLABS = PEER_CHUNK // N_SUB_KEYS
PEER_NCHUNK = (N_SUB_KEYS * N_SUB_KEYS) // PEER_CHUNK
PEER_TG = 16


PEER_SUB = 2
PEER_SELECT_LEAD = 3
PEER_SELECT_PER_STAGE = 6
PEER_LC = PEER_TM // LANES
PEER_LC_STEP = PEER_LC * PEER_HEADS // PEER_NCHUNK
assert PEER_NCHUNK * PEER_LC_STEP == PEER_LC * PEER_HEADS and PEER_TM == TOPK_TK


def _peer_kernel(xn_ref, u_ref, v_ref, e0_ref, g0_ref, h1_ref, qpn_ref, sk_ref, ids_ref, pos_ref,
                 o_ref, hg_ref, et_ref, gt_ref, wa_ref, wb_ref, ebuf_ref, gbuf_ref, *topk_scratch):
    i = pl.program_id(0)
    j = pl.program_id(1)
    cur = i % 2

    @pl.when(jnp.logical_and(i == 0, j == 0))
    def _():
        ebuf_ref[0] = e0_ref[...]
        gbuf_ref[0] = g0_ref[...]

    @pl.when(j == 0)
    def _():
        for c in range(PEER_LC):
            et_ref[c * LANES:(c + 1) * LANES, :] = ebuf_ref[cur, c].T
            gt_ref[c * LANES:(c + 1) * LANES, :] = gbuf_ref[cur, c].T
        sub = lax.broadcasted_iota(jnp.int32, (N_SUB_KEYS, N_PICKS), 0)

        def scatter(gi, w_ref):
            t0 = gi * PEER_TG if isinstance(gi, int) else pl.multiple_of(gi * PEER_TG, PEER_TG)
            for tt in range(PEER_TG):
                e_row = et_ref[pl.ds(t0 + tt, 1), :]
                g_row = gt_ref[pl.ds(t0 + tt, 1), :]
                first = jnp.where(sub == (e_row >> 7), g_row, 0.0).astype(BF16)
                second = jnp.where(sub == (e_row & (N_SUB_KEYS - 1)), 1.0, 0.0).astype(BF16)
                w = _dot_nt(first, second)
                for v in range(N_SUB_KEYS // SUBLANES):
                    w_ref[v, pl.ds(tt * SUBLANES, SUBLANES), :] = w[v * SUBLANES:(v + 1) * SUBLANES, :]

        def relayout(gi, w_ref):
            t0 = gi * PEER_TG if isinstance(gi, int) else pl.multiple_of(gi * PEER_TG, PEER_TG)
            for i1 in range(N_SUB_KEYS):
                v, r = divmod(i1, SUBLANES)
                rows = w_ref[v, pl.ds(r, PEER_TG, stride=SUBLANES), :]
                c, s = divmod(i1, PEER_SLABS)
                hg_ref[c, pl.ds(t0, PEER_TG), s * N_SUB_KEYS:(s + 1) * N_SUB_KEYS] = rows.astype(BF16)

        def pair(k, carry):
            scatter(2 * k + 1, wb_ref)
            relayout(2 * k, wa_ref)
            scatter(2 * k + 2, wa_ref)
            relayout(2 * k + 1, wb_ref)
            return carry

        n_groups = PEER_TM // PEER_TG
        scatter(0, wa_ref)
        lax.fori_loop(0, n_groups // 2 - 1, pair, 0)
        scatter(n_groups - 1, wb_ref)
        relayout(n_groups - 2, wa_ref)
        relayout(n_groups - 1, wb_ref)
        o_ref[...] = h1_ref[...]

    def expert_stages():
        xn = xn_ref[...]
        sub = PEER_CHUNK // PEER_SUB
        hs = []
        for q in range(PEER_SUB):
            es = slice(q * sub, (q + 1) * sub)
            a = _dot_nt(xn, u_ref[es, :])
            hq = _gelu(a) * hg_ref[j, :, es].astype(F32)
            hs.append(hq.astype(BF16))
            yield hq[:SUBLANES, :LANES] + hq[-SUBLANES:, -LANES:]
        for q in range(PEER_SUB):
            es = slice(q * sub, (q + 1) * sub)
            r = _dot(hs[q], v_ref[es, :])
            o_ref[...] += r
            yield r[:SUBLANES, :LANES]

    head_rows = pl.ds(pl.multiple_of((j * PEER_LC_STEP // PEER_LC) * PEER_TOPK, PEER_TOPK), PEER_TOPK)

    def store(lc):
        def fn(e, g):
            ebuf_ref[1 - cur, lc, head_rows, :] = e
            gbuf_ref[1 - cur, lc, head_rows, :] = g
        return fn

    chunks = []
    for c in range(PEER_LC_STEP):
        lc = (j * PEER_LC_STEP) % PEER_LC + c
        qp = qpn_ref[pl.ds(pl.multiple_of(lc * LANES, LANES), LANES), :]
        chunks.append((qp, pl.ds(c * LANES, LANES), store(lc)))
    select = _topk_stages(sk_ref, chunks, ids_ref, pos_ref, topk_scratch)
    experts = expert_stages()
    live = []
    for _ in range(PEER_SELECT_LEAD):
        live = next(select, [])
    for result_tile in experts:
        _order_after(live, result_tile)
        for _ in range(PEER_SELECT_PER_STAGE):
            live = next(select, [])
    for _ in select:
        pass


def _peer(xn2, u_tab, v_tab, e0, g0, h1, qp, sub_keys, ids, pos):
    t, d = xn2.shape
    nblk = t // PEER_TM
    steps_per_head = PEER_LC // PEER_LC_STEP
    const3 = lambda i, j: (0, 0, 0)
    return pl.pallas_call(
        _peer_kernel,
        grid=(nblk, PEER_NCHUNK),
        in_specs=[
            pl.BlockSpec((PEER_TM, d), lambda i, j: (i, 0)),
            pl.BlockSpec((PEER_CHUNK, d), lambda i, j: (j, 0)),
            pl.BlockSpec((PEER_CHUNK, d), lambda i, j: (j, 0)),
            pl.BlockSpec((PEER_LC, N_PICKS, LANES), const3),
            pl.BlockSpec((PEER_LC, N_PICKS, LANES), const3),
            pl.BlockSpec((PEER_TM, d), lambda i, j: (i, 0)),
            pl.BlockSpec((PEER_TM, 2 * PEER_HALF),
                         lambda i, j: (jnp.minimum(i + 1, nblk - 1), j // steps_per_head)),
            pl.BlockSpec((1, 2, N_SUB_KEYS, PEER_HALF), lambda i, j: (j // steps_per_head, 0, 0, 0)),
            pl.BlockSpec((N_SUB_KEYS, LANES), lambda i, j: (0, 0)),
            pl.BlockSpec((CAND_ROWS, LANES), lambda i, j: (0, 0)),
        ],
        out_specs=pl.BlockSpec((PEER_TM, d), lambda i, j: (i, 0)),
        out_shape=jax.ShapeDtypeStruct((t, d), F32),
        scratch_shapes=[
            pltpu.VMEM((PEER_NCHUNK, PEER_TM, PEER_CHUNK), BF16),
            pltpu.VMEM((PEER_TM, N_PICKS), jnp.int32),
            pltpu.VMEM((PEER_TM, N_PICKS), F32),
            pltpu.VMEM((N_SUB_KEYS // SUBLANES, PEER_TG * SUBLANES, N_SUB_KEYS), F32),
            pltpu.VMEM((N_SUB_KEYS // SUBLANES, PEER_TG * SUBLANES, N_SUB_KEYS), F32),
            pltpu.VMEM((2, PEER_LC, N_PICKS, LANES), jnp.int32),
            pltpu.VMEM((2, PEER_LC, N_PICKS, LANES), F32),
        ] + _topk_scratch(PEER_LC_STEP),
        compiler_params=pltpu.CompilerParams(
            dimension_semantics=("arbitrary", "arbitrary"), vmem_limit_bytes=VMEM_LIMIT),
        name="peer",
    )(xn2, u_tab, v_tab, e0, g0, h1, qp, sub_keys, ids, pos)


def _s5_params(lam_re, lam_im, log_step, b_re, b_im, c_re, c_im, bsz):
    step = jnp.exp(log_step)[:, None]
    mag = jnp.exp(lam_re * step)
    abar_re = mag * jnp.cos(lam_im * step)
    abar_im = mag * jnp.sin(lam_im * step)
    den = lam_re * lam_re + lam_im * lam_im
    nr, ni = abar_re - 1.0, abar_im
    coef_re = ((nr * lam_re + ni * lam_im) / den)[..., None]
    coef_im = ((ni * lam_re - nr * lam_im) / den)[..., None]
    bbar_re = coef_re * b_re - coef_im * b_im
    bbar_im = coef_re * b_im + coef_im * b_re
    eye = jnp.eye(S5_GROUPS, dtype=F32)
    to_b = lambda m: jnp.einsum('gpc,gh->gchp', m, eye).reshape(S5_WIDTH, S5_LANES)
    to_c = lambda m: jnp.einsum('gcp,gh->gphc', m, eye).reshape(S5_LANES, S5_WIDTH)
    bblk = jnp.concatenate([to_b(bbar_re), to_b(bbar_im)], axis=1).astype(BF16)
    cblk = jnp.concatenate([to_c(c_re), -to_c(c_im)], axis=0).astype(BF16)
    a = jnp.stack([abar_re.reshape(-1), abar_im.reshape(-1)])
    a8 = jnp.broadcast_to(a[:, None, :], (2, bsz, S5_LANES))
    return a8, bblk, cblk


def kernel(x, ln1_g, w_in, b_gate, s5_lambda_re, s5_lambda_im, s5_log_step, s5_b_re, s5_b_im, s5_c_re, s5_c_im, s5_d, s5_w_glu, w_s5_branch, q_norm_g, k_norm_g, rel_bias_table, attn_sinks, w_attn_branch, w_out, ln2_g, peer_w_query, peer_sub_keys, peer_u, peer_v):
    bsz, seq, d = x.shape
    h = x
    for layer in range(ln1_g.shape[0]):
        w_uqkv = w_in[layer, :, :UQKV_WIDTH].astype(BF16)
        w_gate = w_in[layer, :, UQKV_WIDTH:].astype(BF16)
        u_tm, q, k, v = _in_proj(h, ln1_g[layer][None], w_uqkv)

        a8, bblk, cblk = _s5_params(s5_lambda_re[layer], s5_lambda_im[layer], s5_log_step[layer],
                                    s5_b_re[layer], s5_b_im[layer], s5_c_re[layer], s5_c_im[layer], bsz)
        ys5 = _s5(u_tm, a8, bblk, cblk, s5_d[layer].reshape(1, S5_WIDTH),
                  s5_w_glu[layer].astype(BF16), w_s5_branch[layer].astype(BF16), bsz)

        attn = _attention(q, k, v, q_norm_g[layer][None], k_norm_g[layer][None],
                          _t5_bucket_matrix(), rel_bias_table, attn_sinks[layer])

        h1, xn2, qp = _merge(h, attn, ys5, ln1_g[layer][None], w_gate,
                             b_gate[layer][None], w_attn_branch[layer].astype(BF16),
                             w_out[layer].astype(BF16), ln2_g[layer][None],
                             peer_w_query[layer].astype(BF16))

        sub_keys = peer_sub_keys[layer].astype(BF16)
        pos = jnp.asarray(_CAND_POS)
        ids = jnp.asarray(_KEY_IDS)
        e0, g0 = _topk(qp, sub_keys, ids, pos)
        out = _peer(xn2, peer_u[layer].astype(BF16), peer_v[layer].astype(BF16), e0, g0, h1,
                    qp, sub_keys, ids, pos)
        h = out.reshape(bsz, seq, d)
    return h
```

```python
import functools
import math

import jax
import jax.numpy as jnp
import numpy as np
from jax import lax
from jax.experimental import pallas as pl
from jax.experimental.pallas import tpu as pltpu

F32 = jnp.float32
BF16 = jnp.bfloat16

D_MODEL = 1024
RMS_EPS = 1e-6
NEG_INF = -1e30

S5_WIDTH = 512
S5_GROUP = 16
S5_GROUPS = 32
S5_STATE = 64
S5_LANES = S5_GROUPS * S5_STATE

N_Q_HEADS = 16
N_KV_HEADS = 4
HEAD_DIM = 64
Q_PER_KV = 4
ATTN_BLOCK = 128
N_BUCKETS = 32
MAX_DISTANCE = 128
Q_WIDTH = 1024
KV_WIDTH = 256
UQKV_WIDTH = S5_WIDTH + Q_WIDTH + 2 * KV_WIDTH

PEER_HEADS = 8
PEER_HALF = 64
N_SUB_KEYS = 128
PEER_TOPK = 16
N_PICKS = PEER_HEADS * PEER_TOPK

VMEM_LIMIT = 56 * 1024 * 1024

SQRT_HALF = 0.7071067811865476


def _rms(x, g):
    r = lax.rsqrt(jnp.mean(x * x, axis=-1, keepdims=True) + RMS_EPS)
    return (x * r) * g


def _gelu(x):
    return 0.5 * x * (1.0 + lax.erf(x * SQRT_HALF))


def _dot(a, b):
    return jnp.dot(a, b, preferred_element_type=F32)


def _dot_nt(a, b):
    return lax.dot_general(a, b, (((1,), (1,)), ((), ())), preferred_element_type=F32)


S5_TC = 64


def _in_proj_kernel(bsz, x_ref, g_ref, w_ref, u_ref, q_ref, k_ref, v_ref):
    d = x_ref.shape[-1]
    xn = _rms(x_ref[...].reshape(bsz * S5_TC, d), g_ref[...]).astype(BF16)
    p = _dot(xn, w_ref[...])
    u = p[:, :S5_WIDTH].reshape(bsz, S5_TC, S5_WIDTH)
    u_ref[...] = jnp.transpose(u, (1, 0, 2)).reshape(S5_TC * bsz, S5_WIDTH)
    q_ref[...] = p[:, S5_WIDTH:S5_WIDTH + Q_WIDTH].astype(BF16).reshape(bsz, S5_TC, Q_WIDTH)
    k_ref[...] = p[:, S5_WIDTH + Q_WIDTH:S5_WIDTH + Q_WIDTH + KV_WIDTH].astype(BF16).reshape(
        bsz, S5_TC, KV_WIDTH)
    v_ref[...] = p[:, S5_WIDTH + Q_WIDTH + KV_WIDTH:].astype(BF16).reshape(bsz, S5_TC, KV_WIDTH)


def _in_proj(x, ln1_g, w_uqkv):
    bsz, seq, d = x.shape
    return pl.pallas_call(
        functools.partial(_in_proj_kernel, bsz),
        grid=(seq // S5_TC,),
        in_specs=[
            pl.BlockSpec((bsz, S5_TC, d), lambda i: (0, i, 0)),
            pl.BlockSpec((1, d), lambda i: (0, 0)),
            pl.BlockSpec((d, UQKV_WIDTH), lambda i: (0, 0)),
        ],
        out_specs=[
            pl.BlockSpec((S5_TC * bsz, S5_WIDTH), lambda i: (i, 0)),
            pl.BlockSpec((bsz, S5_TC, Q_WIDTH), lambda i: (0, i, 0)),
            pl.BlockSpec((bsz, S5_TC, KV_WIDTH), lambda i: (0, i, 0)),
            pl.BlockSpec((bsz, S5_TC, KV_WIDTH), lambda i: (0, i, 0)),
        ],
        out_shape=[
            jax.ShapeDtypeStruct((seq * bsz, S5_WIDTH), F32),
            jax.ShapeDtypeStruct((bsz, seq, Q_WIDTH), BF16),
            jax.ShapeDtypeStruct((bsz, seq, KV_WIDTH), BF16),
            jax.ShapeDtypeStruct((bsz, seq, KV_WIDTH), BF16),
        ],
        compiler_params=pltpu.CompilerParams(
            dimension_semantics=("arbitrary",), vmem_limit_bytes=VMEM_LIMIT),
        name="in_proj",
    )(x, ln1_g, w_uqkv)


S5_LC = 512
S5_CH = S5_LC // S5_STATE * S5_GROUP


def _s5_kernel(bsz, u_ref, a_ref, bblk_ref, cblk_ref, d_ref, wglu_ref, wbr_ref, o_ref,
               h_ref, bu_ref):
    @pl.when(pl.program_id(0) == 0)
    def _():
        h_ref[...] = jnp.zeros_like(h_ref)

    u = u_ref[...]
    ub = u.astype(BF16)
    for lc in range(S5_LANES // S5_LC):
        ch = slice(lc * S5_CH, (lc + 1) * S5_CH)
        for part in range(2):
            cols = pl.ds(part * S5_LANES + lc * S5_LC, S5_LC)
            bu_ref[:, cols] = _dot(ub[:, ch], bblk_ref[ch, cols])

    for lc in range(S5_LANES // S5_LC):
        re = pl.ds(lc * S5_LC, S5_LC)
        im = pl.ds(S5_LANES + lc * S5_LC, S5_LC)
        ar = a_ref[0, :, re]
        ai = a_ref[1, :, re]

        hr, hi = h_ref[0, :, re], h_ref[1, :, re]
        for t in range(S5_TC):
            rows = pl.ds(t * bsz, bsz)
            nr = ar * hr - ai * hi + bu_ref[rows, re]
            ni = ar * hi + ai * hr + bu_ref[rows, im]
            bu_ref[rows, re] = nr
            bu_ref[rows, im] = ni
            hr, hi = nr, ni
        h_ref[0, :, re] = hr
        h_ref[1, :, re] = hi

    ys = []
    for lc in range(S5_LANES // S5_LC):
        ch = slice(lc * S5_CH, (lc + 1) * S5_CH)
        re = pl.ds(lc * S5_LC, S5_LC)
        im = pl.ds(S5_LANES + lc * S5_LC, S5_LC)
        ys.append(_dot(bu_ref[:, re].astype(BF16), cblk_ref[re, ch])
                  + _dot(bu_ref[:, im].astype(BF16), cblk_ref[im, ch]))
    y = jnp.concatenate(ys, axis=1) + d_ref[...] * u
    yg = _gelu(y).astype(BF16)
    ab = _dot(yg, wglu_ref[...])
    glu = ab[:, :S5_WIDTH] * jax.nn.sigmoid(ab[:, S5_WIDTH:])
    glu = jnp.transpose(glu.reshape(S5_TC, bsz, S5_WIDTH), (1, 0, 2)).reshape(bsz * S5_TC, S5_WIDTH)
    o_ref[...] = _dot(glu.astype(BF16), wbr_ref[...]).astype(BF16).reshape(bsz, S5_TC, D_MODEL)


def _s5(u_tm, a8, bblk, cblk, d_row, w_glu, w_br, bsz):
    rows = u_tm.shape[0]
    tr = S5_TC * bsz
    const = lambda shape: pl.BlockSpec(shape, lambda i: (0,) * len(shape))
    return pl.pallas_call(
        functools.partial(_s5_kernel, bsz),
        grid=(rows // tr,),
        in_specs=[
            pl.BlockSpec((tr, S5_WIDTH), lambda i: (i, 0)),
            const((2, bsz, S5_LANES)),
            const((S5_WIDTH, 2 * S5_LANES)),
            const((2 * S5_LANES, S5_WIDTH)),
            const((1, S5_WIDTH)),
            const((S5_WIDTH, 2 * S5_WIDTH)),
            const((S5_WIDTH, D_MODEL)),
        ],
        out_specs=pl.BlockSpec((bsz, S5_TC, D_MODEL), lambda i: (0, i, 0)),
        out_shape=jax.ShapeDtypeStruct((bsz, rows // bsz, D_MODEL), BF16),
        scratch_shapes=[
            pltpu.VMEM((2, bsz, S5_LANES), F32),
            pltpu.VMEM((tr, 2 * S5_LANES), F32),
        ],
        compiler_params=pltpu.CompilerParams(
            dimension_semantics=("arbitrary",), vmem_limit_bytes=VMEM_LIMIT),
        name="s5",
    )(u_tm, a8, bblk, cblk, d_row, w_glu, w_br)


def _attn_kernel(q_ref, kc_ref, kp_ref, vc_ref, vp_ref, qg_ref, kg_ref, bucket_ref,
                 table_ref, sink_ref, o_ref, bias_ref):
    first = jnp.logical_and(pl.program_id(0) == 0, pl.program_id(1) == 0)

    @pl.when(first)
    def _():
        bucket = bucket_ref[...]
        for h in range(N_Q_HEADS):
            acc = jnp.zeros((ATTN_BLOCK, 2 * ATTN_BLOCK), F32)
            for b in range(N_BUCKETS):
                acc = jnp.where(bucket == b, table_ref[b, h], acc)
            bias_ref[h] = acc

    blk = pl.program_id(1)
    qi = lax.broadcasted_iota(jnp.int32, (ATTN_BLOCK, 2 * ATTN_BLOCK), 0)
    si = lax.broadcasted_iota(jnp.int32, (ATTN_BLOCK, 2 * ATTN_BLOCK), 1)
    dist = ATTN_BLOCK + qi - si
    valid = (dist >= 0) & (dist < ATTN_BLOCK) & ((si >= ATTN_BLOCK) | (blk > 0))
    qg = qg_ref[...]
    kg = kg_ref[...]
    scale = HEAD_DIM ** -0.5

    grp_rows = Q_PER_KV * ATTN_BLOCK
    row_head = lax.broadcasted_iota(jnp.int32, (grp_rows, 1), 0) // ATTN_BLOCK

    ones_d = jnp.ones((HEAD_DIM, LANES), BF16)
    ones_s = jnp.ones((2 * ATTN_BLOCK, LANES), BF16)

    def rms_rows(x, g):
        ssq = _dot((x * x).astype(BF16), ones_d)[:, :HEAD_DIM]
        return (x * lax.rsqrt(ssq * (1.0 / HEAD_DIM) + RMS_EPS)) * g

    groups = range(N_KV_HEADS)
    kbands, vbands, qss = [], [], []
    for kh in groups:
        cols = slice(kh * HEAD_DIM, (kh + 1) * HEAD_DIM)
        kband = jnp.concatenate([kp_ref[0, :, cols], kc_ref[0, :, cols]], axis=0).astype(F32)
        kbands.append(rms_rows(kband, kg).astype(BF16))
        vbands.append(jnp.concatenate([vp_ref[0, :, cols], vc_ref[0, :, cols]], axis=0))
        h0 = kh * Q_PER_KV
        qs = jnp.concatenate(
            [q_ref[0, :, (h0 + g) * HEAD_DIM:(h0 + g + 1) * HEAD_DIM] for g in range(Q_PER_KV)],
            axis=0).astype(F32)
        qss.append((rms_rows(qs, qg) * scale).astype(BF16))
    ss, sinks = [], []
    for kh in groups:
        h0 = kh * Q_PER_KV
        s = _dot_nt(qss[kh], kbands[kh]).reshape(Q_PER_KV, ATTN_BLOCK, 2 * ATTN_BLOCK)
        ss.append(jnp.where(valid[None], s + bias_ref[h0:h0 + Q_PER_KV], NEG_INF).reshape(
            grp_rows, 2 * ATTN_BLOCK))
        sink = jnp.full((grp_rows, 1), sink_ref[h0], F32)
        for g in range(1, Q_PER_KV):
            sink = jnp.where(row_head == g, sink_ref[h0 + g], sink)
        sinks.append(sink)
    ps, dens = [], []
    for kh in groups:
        m = jnp.maximum(jnp.max(ss[kh], axis=-1, keepdims=True), sinks[kh])
        p = jnp.exp(ss[kh] - m).astype(BF16)
        ps.append(p)
        dens.append(_dot(p, ones_s)[:, :HEAD_DIM] + jnp.exp(sinks[kh] - m))
    for kh in groups:
        h0 = kh * Q_PER_KV
        o = (_dot(ps[kh], vbands[kh]) / dens[kh]).astype(BF16)
        for g in range(Q_PER_KV):
            o_ref[0, :, (h0 + g) * HEAD_DIM:(h0 + g + 1) * HEAD_DIM] = o[g * ATTN_BLOCK:(g + 1) * ATTN_BLOCK]


def _attention(q, k, v, qg, kg, bucket, table, sinks):
    bsz, seq, _ = q.shape
    nb = seq // ATTN_BLOCK
    cur = lambda b, i: (b, i, 0)
    prev = lambda b, i: (b, jnp.maximum(i - 1, 0), 0)
    const2 = lambda b, i: (0, 0)
    return pl.pallas_call(
        _attn_kernel,
        grid=(bsz, nb),
        in_specs=[
            pl.BlockSpec((1, ATTN_BLOCK, Q_WIDTH), cur),
            pl.BlockSpec((1, ATTN_BLOCK, KV_WIDTH), cur),
            pl.BlockSpec((1, ATTN_BLOCK, KV_WIDTH), prev),
            pl.BlockSpec((1, ATTN_BLOCK, KV_WIDTH), cur),
            pl.BlockSpec((1, ATTN_BLOCK, KV_WIDTH), prev),
            pl.BlockSpec((1, HEAD_DIM), const2),
            pl.BlockSpec((1, HEAD_DIM), const2),
            pl.BlockSpec((ATTN_BLOCK, 2 * ATTN_BLOCK), const2),
            pl.BlockSpec(memory_space=pltpu.SMEM),
            pl.BlockSpec(memory_space=pltpu.SMEM),
        ],
        out_specs=pl.BlockSpec((1, ATTN_BLOCK, Q_WIDTH), cur),
        out_shape=jax.ShapeDtypeStruct((bsz, seq, Q_WIDTH), BF16),
        scratch_shapes=[pltpu.VMEM((N_Q_HEADS, ATTN_BLOCK, 2 * ATTN_BLOCK), F32)],
        compiler_params=pltpu.CompilerParams(
            dimension_semantics=("arbitrary", "arbitrary"), vmem_limit_bytes=VMEM_LIMIT),
        name="attn",
    )(q, k, k, v, v, qg, kg, bucket, table, sinks)


def _t5_bucket_matrix():
    qi = np.arange(ATTN_BLOCK)[:, None]
    si = np.arange(2 * ATTN_BLOCK)[None, :]
    dist = np.maximum(ATTN_BLOCK + qi - si, 0)
    max_exact = N_BUCKETS // 2
    d_f = np.maximum(dist, 1).astype(np.float64)
    large = max_exact + np.floor(np.log(d_f / max_exact) / math.log(MAX_DISTANCE / max_exact)
                                 * (N_BUCKETS - max_exact)).astype(np.int32)
    large = np.minimum(large, N_BUCKETS - 1)
    return jnp.asarray(np.where(dist < max_exact, dist, large).astype(np.int32))


MERGE_TM = 512


def _merge_kernel(x_ref, attn_ref, ys5_ref, g1_ref, wg_ref, bg_ref, wab_ref, wout_ref,
                  g2_ref, wpq_ref, h1_ref, xn2_ref, qp_ref):
    x = x_ref[0]
    xn = _rms(x, g1_ref[...]).astype(BF16)
    gates = jax.nn.sigmoid(_dot(xn, wg_ref[...]) + bg_ref[...])
    y_attn = _dot(attn_ref[0], wab_ref[...])
    y_s5 = ys5_ref[0].astype(F32)
    mix = gates[:, :D_MODEL] * y_s5 + gates[:, D_MODEL:] * y_attn
    h1 = x + _dot(mix.astype(BF16), wout_ref[...])
    h1_ref[...] = h1
    xn2 = _rms(h1, g2_ref[...]).astype(BF16)
    xn2_ref[...] = xn2
    qp_ref[...] = _dot(xn2, wpq_ref[...]).astype(BF16)


def _merge(x, attn, ys5_tm, ln1_g, w_gate, b_gate, w_ab, w_out, ln2_g, w_pq):
    bsz, seq, d = x.shape
    nt = seq // MERGE_TM
    tok = lambda b, i: (b * nt + i, 0)
    const2 = lambda b, i: (0, 0)
    return pl.pallas_call(
        _merge_kernel,
        grid=(bsz, nt),
        in_specs=[
            pl.BlockSpec((1, MERGE_TM, d), lambda b, i: (b, i, 0)),
            pl.BlockSpec((1, MERGE_TM, d), lambda b, i: (b, i, 0)),
            pl.BlockSpec((1, MERGE_TM, d), lambda b, i: (b, i, 0)),
            pl.BlockSpec((1, d), const2),
            pl.BlockSpec((d, 2 * d), const2),
            pl.BlockSpec((1, 2 * d), const2),
            pl.BlockSpec((d, d), const2),
            pl.BlockSpec((d, d), const2),
            pl.BlockSpec((1, d), const2),
            pl.BlockSpec((d, d), const2),
        ],
        out_specs=[
            pl.BlockSpec((MERGE_TM, d), tok),
            pl.BlockSpec((MERGE_TM, d), tok),
            pl.BlockSpec((MERGE_TM, d), tok),
        ],
        out_shape=[
            jax.ShapeDtypeStruct((bsz * seq, d), F32),
            jax.ShapeDtypeStruct((bsz * seq, d), BF16),
            jax.ShapeDtypeStruct((bsz * seq, d), BF16),
        ],
        compiler_params=pltpu.CompilerParams(
            dimension_semantics=("arbitrary", "arbitrary"), vmem_limit_bytes=VMEM_LIMIT),
        name="merge",
    )(x, attn, ys5_tm, ln1_g, w_gate, b_gate, w_ab, w_out, ln2_g, w_pq)


TOPK_TK = 512
LANES = 128
SUBLANES = 8
CAND_ROWS = 80
POS_INVALID = 1.0e9


def _cand_layout():
    pos = np.full((CAND_ROWS,), POS_INVALID, np.float32)
    blocks = [(0, 0), (0, 8), (1, 0), (2, 0), (3, 0), (4, 0), (5, 0), (6, 0), (7, 0)]
    for r, (a, b0) in enumerate(blocks):
        for j in range(8):
            b = b0 + j
            if (a + 1) * (b + 1) <= PEER_TOPK:
                pos[r * 8 + j] = a * PEER_TOPK + b
    for j in range(8):
        pos[72 + j] = (8 + j) * PEER_TOPK
    return blocks, np.broadcast_to(pos[:, None], (CAND_ROWS, LANES)).copy()


_CAND_BLOCKS, _CAND_POS = _cand_layout()
_KEY_IDS = np.broadcast_to(np.arange(N_SUB_KEYS, dtype=np.float32)[:, None], (N_SUB_KEYS, LANES)).copy()


def _extract_round(s, ids, big):
    m = jnp.max(s, axis=0, keepdims=True)
    sel = jnp.min(jnp.where(s == m, ids, big), axis=0, keepdims=True)
    return m, sel, ids == sel


def _topk_stages(sk_ref, chunks, ids_ref, pos_ref, scratch):
    v1_ref, i1_ref, v2_ref, i2_ref, best_ref, s_ref, cand_ref, expert_ref, picked_ref = scratch
    nkeys = float(N_SUB_KEYS)
    tile = slice(0, SUBLANES)
    for qp, lanes, _ in chunks:
        s_ref[0, :, lanes] = _dot_nt(sk_ref[0, 0], qp[:, :PEER_HALF])
        s_ref[1, :, lanes] = _dot_nt(sk_ref[0, 1], qp[:, PEER_HALF:])
    live = [(s_ref, (half, tile, lanes)) for _, lanes, _ in chunks for half in range(2)]
    yield live
    for j in range(PEER_TOPK):
        for _, lanes, _ in chunks:
            for half, (val_ref, id_ref) in enumerate(((v1_ref, i1_ref), (v2_ref, i2_ref))):
                s = s_ref[half, :, lanes]
                m, sel, hit = _extract_round(s, ids_ref[...], nkeys)
                val_ref[j:j + 1, lanes] = m
                id_ref[j:j + 1, lanes] = sel
                s_ref[half, :, lanes] = jnp.where(hit, -jnp.inf, s)
        yield live

    for _, lanes, _ in chunks:
        for r, (a, b0) in enumerate(_CAND_BLOCKS):
            rows = slice(r * SUBLANES, (r + 1) * SUBLANES)
            cand_ref[rows, lanes] = v1_ref[a:a + 1, lanes] + v2_ref[b0:b0 + 8, lanes]
            expert_ref[rows, lanes] = i1_ref[a:a + 1, lanes] * nkeys + i2_ref[b0:b0 + 8, lanes]
        rows = slice(CAND_ROWS - SUBLANES, CAND_ROWS)
        cand_ref[rows, lanes] = v1_ref[8:16, lanes] + v2_ref[0:1, lanes]
        expert_ref[rows, lanes] = i1_ref[8:16, lanes] * nkeys + i2_ref[0:1, lanes]
        cand_ref[:, lanes] = jnp.where(pos_ref[...] < POS_INVALID, cand_ref[:, lanes], -jnp.inf)
    live = [(cand_ref, (tile, lanes)) for _, lanes, _ in chunks]
    yield live
    for j in range(PEER_TOPK):
        for _, lanes, _ in chunks:
            cand = cand_ref[:, lanes]
            m, _, hit = _extract_round(cand, pos_ref[...], POS_INVALID)
            best_ref[j:j + 1, lanes] = m
            picked_ref[j:j + 1, lanes] = jnp.sum(jnp.where(hit, expert_ref[:, lanes], 0.0), axis=0,
                                                 keepdims=True)
            cand_ref[:, lanes] = jnp.where(hit, -jnp.inf, cand)
        if j % 2 == 1:
            yield live
    for _, lanes, store in chunks:
        best = best_ref[:, lanes]
        p = jnp.exp(best - best_ref[0:1, lanes])
        store(picked_ref[:, lanes].astype(jnp.int32), p / jnp.sum(p, axis=0, keepdims=True))
    yield []


def _order_after(live, value_tile):
    zero = jnp.minimum(jnp.abs(value_tile), 0.0)
    for ref, idx in live:
        ref[idx] = ref[idx] + zero


def _topk_scratch(n_chunks):
    w = n_chunks * LANES
    return ([pltpu.VMEM((PEER_TOPK, w), F32) for _ in range(5)]
            + [pltpu.VMEM((2, N_SUB_KEYS, w), F32), pltpu.VMEM((CAND_ROWS, w), F32),
               pltpu.VMEM((CAND_ROWS, w), F32), pltpu.VMEM((PEER_TOPK, w), F32)])


def _topk_kernel(qp_ref, sk_ref, ids_ref, pos_ref, e_ref, g_ref, *scratch):
    def store(c):
        def fn(e, g):
            e_ref[c] = e
            g_ref[c] = g
        return fn

    chunks = [(qp_ref[pl.ds(c * LANES, LANES), :], pl.ds(c * LANES, LANES), store(c))
              for c in range(TOPK_TK // LANES)]
    for _ in _topk_stages(sk_ref, chunks, ids_ref, pos_ref, scratch):
        pass


def _topk(qp, sub_keys, ids, pos):
    n_lc = TOPK_TK // LANES
    return pl.pallas_call(
        _topk_kernel,
        grid=(1, PEER_HEADS),
        in_specs=[
            pl.BlockSpec((TOPK_TK, 2 * PEER_HALF), lambda i, h: (i, h)),
            pl.BlockSpec((1, 2, N_SUB_KEYS, PEER_HALF), lambda i, h: (h, 0, 0, 0)),
            pl.BlockSpec((N_SUB_KEYS, LANES), lambda i, h: (0, 0)),
            pl.BlockSpec((CAND_ROWS, LANES), lambda i, h: (0, 0)),
        ],
        out_specs=[
            pl.BlockSpec((n_lc, PEER_TOPK, LANES), lambda i, h: (0, h, 0)),
            pl.BlockSpec((n_lc, PEER_TOPK, LANES), lambda i, h: (0, h, 0)),
        ],
        out_shape=[
            jax.ShapeDtypeStruct((n_lc, N_PICKS, LANES), jnp.int32),
            jax.ShapeDtypeStruct((n_lc, N_PICKS, LANES), F32),
        ],
        scratch_shapes=_topk_scratch(n_lc),
        compiler_params=pltpu.CompilerParams(
            dimension_semantics=("arbitrary", "arbitrary"), vmem_limit_bytes=VMEM_LIMIT),
        name="topk",
    )(qp, sub_keys, ids, pos)


PEER_TM = 512
PEER_CHUNK = 2048
PEER_SLABS = PEER_CHUNK // N_SUB_KEYS
PEER_NCHUNK = (N_SUB_KEYS * N_SUB_KEYS) // PEER_CHUNK
PEER_TG = 16


PEER_SUB = 2
PEER_SELECT_LEAD = 3
PEER_SELECT_PER_STAGE = 6
PEER_LC = PEER_TM // LANES
PEER_LC_STEP = PEER_LC * PEER_HEADS // PEER_NCHUNK
assert PEER_NCHUNK * PEER_LC_STEP == PEER_LC * PEER_HEADS and PEER_TM == TOPK_TK


def _peer_kernel(xn_ref, u_ref, v_ref, e0_ref, g0_ref, h1_ref, qpn_ref, sk_ref, ids_ref, pos_ref,
                 o_ref, hg_ref, et_ref, gt_ref, wa_ref, wb_ref, ebuf_ref, gbuf_ref, *topk_scratch):
    i = pl.program_id(0)
    j = pl.program_id(1)
    cur = i % 2

    @pl.when(jnp.logical_and(i == 0, j == 0))
    def _():
        ebuf_ref[0] = e0_ref[...]
        gbuf_ref[0] = g0_ref[...]

    @pl.when(j == 0)
    def _():
        for c in range(PEER_LC):
            et_ref[c * LANES:(c + 1) * LANES, :] = ebuf_ref[cur, c].T
            gt_ref[c * LANES:(c + 1) * LANES, :] = gbuf_ref[cur, c].T
        sub = lax.broadcasted_iota(jnp.int32, (N_SUB_KEYS, N_PICKS), 0)

        def scatter(gi, w_ref):
            t0 = gi * PEER_TG if isinstance(gi, int) else pl.multiple_of(gi * PEER_TG, PEER_TG)
            for tt in range(PEER_TG):
                e_row = et_ref[pl.ds(t0 + tt, 1), :]
                g_row = gt_ref[pl.ds(t0 + tt, 1), :]
                first = jnp.where(sub == (e_row >> 7), g_row, 0.0).astype(BF16)
                second = jnp.where(sub == (e_row & (N_SUB_KEYS - 1)), 1.0, 0.0).astype(BF16)
                w = _dot_nt(first, second)
                for v in range(N_SUB_KEYS // SUBLANES):
                    w_ref[v, pl.ds(tt * SUBLANES, SUBLANES), :] = w[v * SUBLANES:(v + 1) * SUBLANES, :]

        def relayout(gi, w_ref):
            t0 = gi * PEER_TG if isinstance(gi, int) else pl.multiple_of(gi * PEER_TG, PEER_TG)
            for i1 in range(N_SUB_KEYS):
                v, r = divmod(i1, SUBLANES)
                rows = w_ref[v, pl.ds(r, PEER_TG, stride=SUBLANES), :]
                c, s = divmod(i1, PEER_SLABS)
                hg_ref[c, pl.ds(t0, PEER_TG), s * N_SUB_KEYS:(s + 1) * N_SUB_KEYS] = rows.astype(BF16)

        def pair(k, carry):
            scatter(2 * k + 1, wb_ref)
            relayout(2 * k, wa_ref)
            scatter(2 * k + 2, wa_ref)
            relayout(2 * k + 1, wb_ref)
            return carry

        n_groups = PEER_TM // PEER_TG
        scatter(0, wa_ref)
        lax.fori_loop(0, n_groups // 2 - 1, pair, 0)
        scatter(n_groups - 1, wb_ref)
        relayout(n_groups - 2, wa_ref)
        relayout(n_groups - 1, wb_ref)
        o_ref[...] = h1_ref[...]

    def expert_stages():
        xn = xn_ref[...]
        sub = PEER_CHUNK // PEER_SUB
        hs = []
        for q in range(PEER_SUB):
            es = slice(q * sub, (q + 1) * sub)
            a = _dot_nt(xn, u_ref[es, :])
            hq = _gelu(a) * hg_ref[j, :, es].astype(F32)
            hs.append(hq.astype(BF16))
            yield hq[:SUBLANES, :LANES] + hq[-SUBLANES:, -LANES:]
        for q in range(PEER_SUB):
            es = slice(q * sub, (q + 1) * sub)
            r = _dot(hs[q], v_ref[es, :])
            o_ref[...] += r
            yield r[:SUBLANES, :LANES]

    head_rows = pl.ds(pl.multiple_of((j * PEER_LC_STEP // PEER_LC) * PEER_TOPK, PEER_TOPK), PEER_TOPK)

    def store(lc):
        def fn(e, g):
            ebuf_ref[1 - cur, lc, head_rows, :] = e
            gbuf_ref[1 - cur, lc, head_rows, :] = g
        return fn

    chunks = []
    for c in range(PEER_LC_STEP):
        lc = (j * PEER_LC_STEP) % PEER_LC + c
        qp = qpn_ref[pl.ds(pl.multiple_of(lc * LANES, LANES), LANES), :]
        chunks.append((qp, pl.ds(c * LANES, LANES), store(lc)))
    select = _topk_stages(sk_ref, chunks, ids_ref, pos_ref, topk_scratch)
    experts = expert_stages()
    live = []
    for _ in range(PEER_SELECT_LEAD):
        live = next(select, [])
    for result_tile in experts:
        _order_after(live, result_tile)
        for _ in range(PEER_SELECT_PER_STAGE):
            live = next(select, [])
    for _ in select:
        pass


def _peer(xn2, u_tab, v_tab, e0, g0, h1, qp, sub_keys, ids, pos):
    t, d = xn2.shape
    nblk = t // PEER_TM
    steps_per_head = PEER_LC // PEER_LC_STEP
    const3 = lambda i, j: (0, 0, 0)
    return pl.pallas_call(
        _peer_kernel,
        grid=(nblk, PEER_NCHUNK),
        in_specs=[
            pl.BlockSpec((PEER_TM, d), lambda i, j: (i, 0)),
            pl.BlockSpec((PEER_CHUNK, d), lambda i, j: (j, 0)),
            pl.BlockSpec((PEER_CHUNK, d), lambda i, j: (j, 0)),
            pl.BlockSpec((PEER_LC, N_PICKS, LANES), const3),
            pl.BlockSpec((PEER_LC, N_PICKS, LANES), const3),
            pl.BlockSpec((PEER_TM, d), lambda i, j: (i, 0)),
            pl.BlockSpec((PEER_TM, 2 * PEER_HALF),
                         lambda i, j: (jnp.minimum(i + 1, nblk - 1), j // steps_per_head)),
            pl.BlockSpec((1, 2, N_SUB_KEYS, PEER_HALF), lambda i, j: (j // steps_per_head, 0, 0, 0)),
            pl.BlockSpec((N_SUB_KEYS, LANES), lambda i, j: (0, 0)),
            pl.BlockSpec((CAND_ROWS, LANES), lambda i, j: (0, 0)),
        ],
        out_specs=pl.BlockSpec((PEER_TM, d), lambda i, j: (i, 0)),
        out_shape=jax.ShapeDtypeStruct((t, d), F32),
        scratch_shapes=[
            pltpu.VMEM((PEER_NCHUNK, PEER_TM, PEER_CHUNK), BF16),
            pltpu.VMEM((PEER_TM, N_PICKS), jnp.int32),
            pltpu.VMEM((PEER_TM, N_PICKS), F32),
            pltpu.VMEM((N_SUB_KEYS // SUBLANES, PEER_TG * SUBLANES, N_SUB_KEYS), F32),
            pltpu.VMEM((N_SUB_KEYS // SUBLANES, PEER_TG * SUBLANES, N_SUB_KEYS), F32),
            pltpu.VMEM((2, PEER_LC, N_PICKS, LANES), jnp.int32),
            pltpu.VMEM((2, PEER_LC, N_PICKS, LANES), F32),
        ] + _topk_scratch(PEER_LC_STEP),
        compiler_params=pltpu.CompilerParams(
            dimension_semantics=("arbitrary", "arbitrary"), vmem_limit_bytes=VMEM_LIMIT),
        name="peer",
    )(xn2, u_tab, v_tab, e0, g0, h1, qp, sub_keys, ids, pos)


def _s5_params(lam_re, lam_im, log_step, b_re, b_im, c_re, c_im, bsz):
    step = jnp.exp(log_step)[:, None]
    mag = jnp.exp(lam_re * step)
    abar_re = mag * jnp.cos(lam_im * step)
    abar_im = mag * jnp.sin(lam_im * step)
    den = lam_re * lam_re + lam_im * lam_im
    nr, ni = abar_re - 1.0, abar_im
    coef_re = ((nr * lam_re + ni * lam_im) / den)[..., None]
    coef_im = ((ni * lam_re - nr * lam_im) / den)[..., None]
    bbar_re = coef_re * b_re - coef_im * b_im
    bbar_im = coef_re * b_im + coef_im * b_re
    eye = jnp.eye(S5_GROUPS, dtype=F32)
    to_b = lambda m: jnp.einsum('gpc,gh->gchp', m, eye).reshape(S5_WIDTH, S5_LANES)
    to_c = lambda m: jnp.einsum('gcp,gh->gphc', m, eye).reshape(S5_LANES, S5_WIDTH)
    bblk = jnp.concatenate([to_b(bbar_re), to_b(bbar_im)], axis=1).astype(BF16)
    cblk = jnp.concatenate([to_c(c_re), -to_c(c_im)], axis=0).astype(BF16)
    a = jnp.stack([abar_re.reshape(-1), abar_im.reshape(-1)])
    a8 = jnp.broadcast_to(a[:, None, :], (2, bsz, S5_LANES))
    return a8, bblk, cblk


def kernel(x, ln1_g, w_in, b_gate, s5_lambda_re, s5_lambda_im, s5_log_step, s5_b_re, s5_b_im, s5_c_re, s5_c_im, s5_d, s5_w_glu, w_s5_branch, q_norm_g, k_norm_g, rel_bias_table, attn_sinks, w_attn_branch, w_out, ln2_g, peer_w_query, peer_sub_keys, peer_u, peer_v):
    bsz, seq, d = x.shape
    h = x
    for layer in range(ln1_g.shape[0]):
        w_uqkv = w_in[layer, :, :UQKV_WIDTH].astype(BF16)
        w_gate = w_in[layer, :, UQKV_WIDTH:].astype(BF16)
        u_tm, q, k, v = _in_proj(h, ln1_g[layer][None], w_uqkv)

        a8, bblk, cblk = _s5_params(s5_lambda_re[layer], s5_lambda_im[layer], s5_log_step[layer],
                                    s5_b_re[layer], s5_b_im[layer], s5_c_re[layer], s5_c_im[layer], bsz)
        ys5 = _s5(u_tm, a8, bblk, cblk, s5_d[layer].reshape(1, S5_WIDTH),
                  s5_w_glu[layer].astype(BF16), w_s5_branch[layer].astype(BF16), bsz)

        attn = _attention(q, k, v, q_norm_g[layer][None], k_norm_g[layer][None],
                          _t5_bucket_matrix(), rel_bias_table, attn_sinks[layer])

        h1, xn2, qp = _merge(h, attn, ys5, ln1_g[layer][None], w_gate,
                             b_gate[layer][None], w_attn_branch[layer].astype(BF16),
                             w_out[layer].astype(BF16), ln2_g[layer][None],
                             peer_w_query[layer].astype(BF16))

        sub_keys = peer_sub_keys[layer].astype(BF16)
        pos = jnp.asarray(_CAND_POS)
        ids = jnp.asarray(_KEY_IDS)
        e0, g0 = _topk(qp, sub_keys, ids, pos)
        out = _peer(xn2, peer_u[layer].astype(BF16), peer_v[layer].astype(BF16), e0, g0, h1,
                    qp, sub_keys, ids, pos)
        h = out.reshape(bsz, seq, d)
    return h
```

```python
import functools
import math

import jax
import jax.numpy as jnp
import numpy as np
from jax import lax
from jax.experimental import pallas as pl
from jax.experimental.pallas import tpu as pltpu

F32 = jnp.float32
BF16 = jnp.bfloat16

D_MODEL = 1024
RMS_EPS = 1e-6
NEG_INF = -1e30

S5_WIDTH = 512
S5_GROUP = 16
S5_GROUPS = 32
S5_STATE = 64
S5_LANES = S5_GROUPS * S5_STATE

N_Q_HEADS = 16
N_KV_HEADS = 4
HEAD_DIM = 64
Q_PER_KV = 4
ATTN_BLOCK = 128
N_BUCKETS = 32
MAX_DISTANCE = 128
Q_WIDTH = 1024
KV_WIDTH = 256
UQKV_WIDTH = S5_WIDTH + Q_WIDTH + 2 * KV_WIDTH

PEER_HEADS = 8
PEER_HALF = 64
N_SUB_KEYS = 128
PEER_TOPK = 16
N_PICKS = PEER_HEADS * PEER_TOPK

VMEM_LIMIT = 56 * 1024 * 1024

SQRT_HALF = 0.7071067811865476


def _rms(x, g):
    r = lax.rsqrt(jnp.mean(x * x, axis=-1, keepdims=True) + RMS_EPS)
    return (x * r) * g


def _gelu(x):
    return 0.5 * x * (1.0 + lax.erf(x * SQRT_HALF))


def _dot(a, b):
    return jnp.dot(a, b, preferred_element_type=F32)


def _dot_nt(a, b):
    return lax.dot_general(a, b, (((1,), (1,)), ((), ())), preferred_element_type=F32)


S5_TC = 64


def _in_proj_kernel(bsz, x_ref, g_ref, w_ref, u_ref, q_ref, k_ref, v_ref):
    d = x_ref.shape[-1]
    xn = _rms(x_ref[...].reshape(bsz * S5_TC, d), g_ref[...]).astype(BF16)
    p = _dot(xn, w_ref[...])
    u = p[:, :S5_WIDTH].reshape(bsz, S5_TC, S5_WIDTH)
    u_ref[...] = jnp.transpose(u, (1, 0, 2)).reshape(S5_TC * bsz, S5_WIDTH)
    q_ref[...] = p[:, S5_WIDTH:S5_WIDTH + Q_WIDTH].astype(BF16).reshape(bsz, S5_TC, Q_WIDTH)
    k_ref[...] = p[:, S5_WIDTH + Q_WIDTH:S5_WIDTH + Q_WIDTH + KV_WIDTH].astype(BF16).reshape(
        bsz, S5_TC, KV_WIDTH)
    v_ref[...] = p[:, S5_WIDTH + Q_WIDTH + KV_WIDTH:].astype(BF16).reshape(bsz, S5_TC, KV_WIDTH)


def _in_proj(x, ln1_g, w_uqkv):
    bsz, seq, d = x.shape
    return pl.pallas_call(
        functools.partial(_in_proj_kernel, bsz),
        grid=(seq // S5_TC,),
        in_specs=[
            pl.BlockSpec((bsz, S5_TC, d), lambda i: (0, i, 0)),
            pl.BlockSpec((1, d), lambda i: (0, 0)),
            pl.BlockSpec((d, UQKV_WIDTH), lambda i: (0, 0)),
        ],
        out_specs=[
            pl.BlockSpec((S5_TC * bsz, S5_WIDTH), lambda i: (i, 0)),
            pl.BlockSpec((bsz, S5_TC, Q_WIDTH), lambda i: (0, i, 0)),
            pl.BlockSpec((bsz, S5_TC, KV_WIDTH), lambda i: (0, i, 0)),
            pl.BlockSpec((bsz, S5_TC, KV_WIDTH), lambda i: (0, i, 0)),
        ],
        out_shape=[
            jax.ShapeDtypeStruct((seq * bsz, S5_WIDTH), F32),
            jax.ShapeDtypeStruct((bsz, seq, Q_WIDTH), BF16),
            jax.ShapeDtypeStruct((bsz, seq, KV_WIDTH), BF16),
            jax.ShapeDtypeStruct((bsz, seq, KV_WIDTH), BF16),
        ],
        compiler_params=pltpu.CompilerParams(
            dimension_semantics=("arbitrary",), vmem_limit_bytes=VMEM_LIMIT),
        name="in_proj",
    )(x, ln1_g, w_uqkv)


S5_LC = 512
S5_CH = S5_LC // S5_STATE * S5_GROUP


def _s5_kernel(bsz, u_ref, a_ref, bblk_ref, cblk_ref, d_ref, wglu_ref, wbr_ref, o_ref,
               h_ref, bu_ref):
    @pl.when(pl.program_id(0) == 0)
    def _():
        h_ref[...] = jnp.zeros_like(h_ref)

    u = u_ref[...]
    ub = u.astype(BF16)
    for lc in range(S5_LANES // S5_LC):
        ch = slice(lc * S5_CH, (lc + 1) * S5_CH)
        for part in range(2):
            cols = pl.ds(part * S5_LANES + lc * S5_LC, S5_LC)
            bu_ref[:, cols] = _dot(ub[:, ch], bblk_ref[ch, cols])

    for lc in range(S5_LANES // S5_LC):
        re = pl.ds(lc * S5_LC, S5_LC)
        im = pl.ds(S5_LANES + lc * S5_LC, S5_LC)
        ar = a_ref[0, :, re]
        ai = a_ref[1, :, re]

        hr, hi = h_ref[0, :, re], h_ref[1, :, re]
        for t in range(S5_TC):
            rows = pl.ds(t * bsz, bsz)
            nr = ar * hr - ai * hi + bu_ref[rows, re]
            ni = ar * hi + ai * hr + bu_ref[rows, im]
            bu_ref[rows, re] = nr
            bu_ref[rows, im] = ni
            hr, hi = nr, ni
        h_ref[0, :, re] = hr
        h_ref[1, :, re] = hi

    ys = []
    for lc in range(S5_LANES // S5_LC):
        ch = slice(lc * S5_CH, (lc + 1) * S5_CH)
        re = pl.ds(lc * S5_LC, S5_LC)
        im = pl.ds(S5_LANES + lc * S5_LC, S5_LC)
        ys.append(_dot(bu_ref[:, re].astype(BF16), cblk_ref[re, ch])
                  + _dot(bu_ref[:, im].astype(BF16), cblk_ref[im, ch]))
    y = jnp.concatenate(ys, axis=1) + d_ref[...] * u
    yg = _gelu(y).astype(BF16)
    ab = _dot(yg, wglu_ref[...])
    glu = ab[:, :S5_WIDTH] * jax.nn.sigmoid(ab[:, S5_WIDTH:])
    glu = jnp.transpose(glu.reshape(S5_TC, bsz, S5_WIDTH), (1, 0, 2)).reshape(bsz * S5_TC, S5_WIDTH)
    o_ref[...] = _dot(glu.astype(BF16), wbr_ref[...]).astype(BF16).reshape(bsz, S5_TC, D_MODEL)


def _s5(u_tm, a8, bblk, cblk, d_row, w_glu, w_br, bsz):
    rows = u_tm.shape[0]
    tr = S5_TC * bsz
    const = lambda shape: pl.BlockSpec(shape, lambda i: (0,) * len(shape))
    return pl.pallas_call(
        functools.partial(_s5_kernel, bsz),
        grid=(rows // tr,),
        in_specs=[
            pl.BlockSpec((tr, S5_WIDTH), lambda i: (i, 0)),
            const((2, bsz, S5_LANES)),
            const((S5_WIDTH, 2 * S5_LANES)),
            const((2 * S5_LANES, S5_WIDTH)),
            const((1, S5_WIDTH)),
            const((S5_WIDTH, 2 * S5_WIDTH)),
            const((S5_WIDTH, D_MODEL)),
        ],
        out_specs=pl.BlockSpec((bsz, S5_TC, D_MODEL), lambda i: (0, i, 0)),
        out_shape=jax.ShapeDtypeStruct((bsz, rows // bsz, D_MODEL), BF16),
        scratch_shapes=[
            pltpu.VMEM((2, bsz, S5_LANES), F32),
            pltpu.VMEM((tr, 2 * S5_LANES), F32),
        ],
        compiler_params=pltpu.CompilerParams(
            dimension_semantics=("arbitrary",), vmem_limit_bytes=VMEM_LIMIT),
        name="s5",
    )(u_tm, a8, bblk, cblk, d_row, w_glu, w_br)


def _attn_kernel(q_ref, kc_ref, kp_ref, vc_ref, vp_ref, qg_ref, kg_ref, bucket_ref,
                 table_ref, sink_ref, o_ref, bias_ref):
    first = jnp.logical_and(pl.program_id(0) == 0, pl.program_id(1) == 0)

    @pl.when(first)
    def _():
        bucket = bucket_ref[...]
        for h in range(N_Q_HEADS):
            acc = jnp.zeros((ATTN_BLOCK, 2 * ATTN_BLOCK), F32)
            for b in range(N_BUCKETS):
                acc = jnp.where(bucket == b, table_ref[b, h], acc)
            bias_ref[h] = acc

    blk = pl.program_id(1)
    qi = lax.broadcasted_iota(jnp.int32, (ATTN_BLOCK, 2 * ATTN_BLOCK), 0)
    si = lax.broadcasted_iota(jnp.int32, (ATTN_BLOCK, 2 * ATTN_BLOCK), 1)
    dist = ATTN_BLOCK + qi - si
    valid = (dist >= 0) & (dist < ATTN_BLOCK) & ((si >= ATTN_BLOCK) | (blk > 0))
    qg = qg_ref[...]
    kg = kg_ref[...]
    scale = HEAD_DIM ** -0.5

    grp_rows = Q_PER_KV * ATTN_BLOCK
    row_head = lax.broadcasted_iota(jnp.int32, (grp_rows, 1), 0) // ATTN_BLOCK

    ones_d = jnp.ones((HEAD_DIM, LANES), BF16)
    ones_s = jnp.ones((2 * ATTN_BLOCK, LANES), BF16)

    def rms_rows(x, g):
        ssq = _dot((x * x).astype(BF16), ones_d)[:, :HEAD_DIM]
        return (x * lax.rsqrt(ssq * (1.0 / HEAD_DIM) + RMS_EPS)) * g

    groups = range(N_KV_HEADS)
    kbands, vbands, qss = [], [], []
    for kh in groups:
        cols = slice(kh * HEAD_DIM, (kh + 1) * HEAD_DIM)
        kband = jnp.concatenate([kp_ref[0, :, cols], kc_ref[0, :, cols]], axis=0).astype(F32)
        kbands.append(rms_rows(kband, kg).astype(BF16))
        vbands.append(jnp.concatenate([vp_ref[0, :, cols], vc_ref[0, :, cols]], axis=0))
        h0 = kh * Q_PER_KV
        qs = jnp.concatenate(
            [q_ref[0, :, (h0 + g) * HEAD_DIM:(h0 + g + 1) * HEAD_DIM] for g in range(Q_PER_KV)],
            axis=0).astype(F32)
        qss.append((rms_rows(qs, qg) * scale).astype(BF16))
    ss, sinks = [], []
    for kh in groups:
        h0 = kh * Q_PER_KV
        s = _dot_nt(qss[kh], kbands[kh]).reshape(Q_PER_KV, ATTN_BLOCK, 2 * ATTN_BLOCK)
        ss.append(jnp.where(valid[None], s + bias_ref[h0:h0 + Q_PER_KV], NEG_INF).reshape(
            grp_rows, 2 * ATTN_BLOCK))
        sink = jnp.full((grp_rows, 1), sink_ref[h0], F32)
        for g in range(1, Q_PER_KV):
            sink = jnp.where(row_head == g, sink_ref[h0 + g], sink)
        sinks.append(sink)
    ps, dens = [], []
    for kh in groups:
        m = jnp.maximum(jnp.max(ss[kh], axis=-1, keepdims=True), sinks[kh])
        p = jnp.exp(ss[kh] - m).astype(BF16)
        ps.append(p)
        dens.append(_dot(p, ones_s)[:, :HEAD_DIM] + jnp.exp(sinks[kh] - m))
    for kh in groups:
        h0 = kh * Q_PER_KV
        o = (_dot(ps[kh], vbands[kh]) / dens[kh]).astype(BF16)
        for g in range(Q_PER_KV):
            o_ref[0, :, (h0 + g) * HEAD_DIM:(h0 + g + 1) * HEAD_DIM] = o[g * ATTN_BLOCK:(g + 1) * ATTN_BLOCK]


def _attention(q, k, v, qg, kg, bucket, table, sinks):
    bsz, seq, _ = q.shape
    nb = seq // ATTN_BLOCK
    cur = lambda b, i: (b, i, 0)
    prev = lambda b, i: (b, jnp.maximum(i - 1, 0), 0)
    const2 = lambda b, i: (0, 0)
    return pl.pallas_call(
        _attn_kernel,
        grid=(bsz, nb),
        in_specs=[
            pl.BlockSpec((1, ATTN_BLOCK, Q_WIDTH), cur),
            pl.BlockSpec((1, ATTN_BLOCK, KV_WIDTH), cur),
            pl.BlockSpec((1, ATTN_BLOCK, KV_WIDTH), prev),
            pl.BlockSpec((1, ATTN_BLOCK, KV_WIDTH), cur),
            pl.BlockSpec((1, ATTN_BLOCK, KV_WIDTH), prev),
            pl.BlockSpec((1, HEAD_DIM), const2),
            pl.BlockSpec((1, HEAD_DIM), const2),
            pl.BlockSpec((ATTN_BLOCK, 2 * ATTN_BLOCK), const2),
            pl.BlockSpec(memory_space=pltpu.SMEM),
            pl.BlockSpec(memory_space=pltpu.SMEM),
        ],
        out_specs=pl.BlockSpec((1, ATTN_BLOCK, Q_WIDTH), cur),
        out_shape=jax.ShapeDtypeStruct((bsz, seq, Q_WIDTH), BF16),
        scratch_shapes=[pltpu.VMEM((N_Q_HEADS, ATTN_BLOCK, 2 * ATTN_BLOCK), F32)],
        compiler_params=pltpu.CompilerParams(
            dimension_semantics=("arbitrary", "arbitrary"), vmem_limit_bytes=VMEM_LIMIT),
        name="attn",
    )(q, k, k, v, v, qg, kg, bucket, table, sinks)


def _t5_bucket_matrix():
    qi = np.arange(ATTN_BLOCK)[:, None]
    si = np.arange(2 * ATTN_BLOCK)[None, :]
    dist = np.maximum(ATTN_BLOCK + qi - si, 0)
    max_exact = N_BUCKETS // 2
    d_f = np.maximum(dist, 1).astype(np.float64)
    large = max_exact + np.floor(np.log(d_f / max_exact) / math.log(MAX_DISTANCE / max_exact)
                                 * (N_BUCKETS - max_exact)).astype(np.int32)
    large = np.minimum(large, N_BUCKETS - 1)
    return jnp.asarray(np.where(dist < max_exact, dist, large).astype(np.int32))


MERGE_TM = 512


def _merge_kernel(x_ref, attn_ref, ys5_ref, g1_ref, wg_ref, bg_ref, wab_ref, wout_ref,
                  g2_ref, wpq_ref, h1_ref, xn2_ref, qp_ref):
    x = x_ref[0]
    xn = _rms(x, g1_ref[...]).astype(BF16)
    gates = jax.nn.sigmoid(_dot(xn, wg_ref[...]) + bg_ref[...])
    y_attn = _dot(attn_ref[0], wab_ref[...])
    y_s5 = ys5_ref[0].astype(F32)
    mix = gates[:, :D_MODEL] * y_s5 + gates[:, D_MODEL:] * y_attn
    h1 = x + _dot(mix.astype(BF16), wout_ref[...])
    h1_ref[...] = h1
    xn2 = _rms(h1, g2_ref[...]).astype(BF16)
    xn2_ref[...] = xn2
    qp_ref[...] = _dot(xn2, wpq_ref[...]).astype(BF16)


def _merge(x, attn, ys5_tm, ln1_g, w_gate, b_gate, w_ab, w_out, ln2_g, w_pq):
    bsz, seq, d = x.shape
    nt = seq // MERGE_TM
    tok = lambda b, i: (b * nt + i, 0)
    const2 = lambda b, i: (0, 0)
    return pl.pallas_call(
        _merge_kernel,
        grid=(bsz, nt),
        in_specs=[
            pl.BlockSpec((1, MERGE_TM, d), lambda b, i: (b, i, 0)),
            pl.BlockSpec((1, MERGE_TM, d), lambda b, i: (b, i, 0)),
            pl.BlockSpec((1, MERGE_TM, d), lambda b, i: (b, i, 0)),
            pl.BlockSpec((1, d), const2),
            pl.BlockSpec((d, 2 * d), const2),
            pl.BlockSpec((1, 2 * d), const2),
            pl.BlockSpec((d, d), const2),
            pl.BlockSpec((d, d), const2),
            pl.BlockSpec((1, d), const2),
            pl.BlockSpec((d, d), const2),
        ],
        out_specs=[
            pl.BlockSpec((MERGE_TM, d), tok),
            pl.BlockSpec((MERGE_TM, d), tok),
            pl.BlockSpec((MERGE_TM, d), tok),
        ],
        out_shape=[
            jax.ShapeDtypeStruct((bsz * seq, d), F32),
            jax.ShapeDtypeStruct((bsz * seq, d), BF16),
            jax.ShapeDtypeStruct((bsz * seq, d), BF16),
        ],
        compiler_params=pltpu.CompilerParams(
            dimension_semantics=("arbitrary", "arbitrary"), vmem_limit_bytes=VMEM_LIMIT),
        name="merge",
    )(x, attn, ys5_tm, ln1_g, w_gate, b_gate, w_ab, w_out, ln2_g, w_pq)


TOPK_TK = 512
LANES = 128
SUBLANES = 8
CAND_ROWS = 80
POS_INVALID = 1.0e9


def _cand_layout():
    pos = np.full((CAND_ROWS,), POS_INVALID, np.float32)
    blocks = [(0, 0), (0, 8), (1, 0), (2, 0), (3, 0), (4, 0), (5, 0), (6, 0), (7, 0)]
    for r, (a, b0) in enumerate(blocks):
        for j in range(8):
            b = b0 + j
            if (a + 1) * (b + 1) <= PEER_TOPK:
                pos[r * 8 + j] = a * PEER_TOPK + b
    for j in range(8):
        pos[72 + j] = (8 + j) * PEER_TOPK
    return blocks, np.broadcast_to(pos[:, None], (CAND_ROWS, LANES)).copy()


_CAND_BLOCKS, _CAND_POS = _cand_layout()
_KEY_IDS = np.broadcast_to(np.arange(N_SUB_KEYS, dtype=np.float32)[:, None], (N_SUB_KEYS, LANES)).copy()


def _extract_round(s, ids, big):
    m = jnp.max(s, axis=0, keepdims=True)
    sel = jnp.min(jnp.where(s == m, ids, big), axis=0, keepdims=True)
    return m, sel, ids == sel


def _topk_stages(sk_ref, chunks, ids_ref, pos_ref, scratch):
    v1_ref, i1_ref, v2_ref, i2_ref, best_ref, s_ref, cand_ref, expert_ref, picked_ref = scratch
    nkeys = float(N_SUB_KEYS)
    tile = slice(0, SUBLANES)
    for qp, lanes, _ in chunks:
        s_ref[0, :, lanes] = _dot_nt(sk_ref[0, 0], qp[:, :PEER_HALF])
        s_ref[1, :, lanes] = _dot_nt(sk_ref[0, 1], qp[:, PEER_HALF:])
    live = [(s_ref, (half, tile, lanes)) for _, lanes, _ in chunks for half in range(2)]
    yield live
    for j in range(PEER_TOPK):
        for _, lanes, _ in chunks:
            for half, (val_ref, id_ref) in enumerate(((v1_ref, i1_ref), (v2_ref, i2_ref))):
                s = s_ref[half, :, lanes]
                m, sel, hit = _extract_round(s, ids_ref[...], nkeys)
                val_ref[j:j + 1, lanes] = m
                id_ref[j:j + 1, lanes] = sel
                s_ref[half, :, lanes] = jnp.where(hit, -jnp.inf, s)
        yield live

    for _, lanes, _ in chunks:
        for r, (a, b0) in enumerate(_CAND_BLOCKS):
            rows = slice(r * SUBLANES, (r + 1) * SUBLANES)
            cand_ref[rows, lanes] = v1_ref[a:a + 1, lanes] + v2_ref[b0:b0 + 8, lanes]
            expert_ref[rows, lanes] = i1_ref[a:a + 1, lanes] * nkeys + i2_ref[b0:b0 + 8, lanes]
        rows = slice(CAND_ROWS - SUBLANES, CAND_ROWS)
        cand_ref[rows, lanes] = v1_ref[8:16, lanes] + v2_ref[0:1, lanes]
        expert_ref[rows, lanes] = i1_ref[8:16, lanes] * nkeys + i2_ref[0:1, lanes]
        cand_ref[:, lanes] = jnp.where(pos_ref[...] < POS_INVALID, cand_ref[:, lanes], -jnp.inf)
    live = [(cand_ref, (tile, lanes)) for _, lanes, _ in chunks]
    yield live
    for j in range(PEER_TOPK):
        for _, lanes, _ in chunks:
            cand = cand_ref[:, lanes]
            m, _, hit = _extract_round(cand, pos_ref[...], POS_INVALID)
            best_ref[j:j + 1, lanes] = m
            picked_ref[j:j + 1, lanes] = jnp.sum(jnp.where(hit, expert_ref[:, lanes], 0.0), axis=0,
                                                 keepdims=True)
            cand_ref[:, lanes] = jnp.where(hit, -jnp.inf, cand)
        if j % 2 == 1:
            yield live
    for _, lanes, store in chunks:
        best = best_ref[:, lanes]
        p = jnp.exp(best - best_ref[0:1, lanes])
        store(picked_ref[:, lanes].astype(jnp.int32), p / jnp.sum(p, axis=0, keepdims=True))
    yield []


def _order_after(live, value_tile):
    zero = jnp.minimum(jnp.abs(value_tile), 0.0)
    for ref, idx in live:
        ref[idx] = ref[idx] + zero


def _topk_scratch(n_chunks):
    w = n_chunks * LANES
    return ([pltpu.VMEM((PEER_TOPK, w), F32) for _ in range(5)]
            + [pltpu.VMEM((2, N_SUB_KEYS, w), F32), pltpu.VMEM((CAND_ROWS, w), F32),
               pltpu.VMEM((CAND_ROWS, w), F32), pltpu.VMEM((PEER_TOPK, w), F32)])


def _topk_kernel(qp_ref, sk_ref, ids_ref, pos_ref, e_ref, g_ref, *scratch):
    def store(c):
        def fn(e, g):
            e_ref[c] = e
            g_ref[c] = g
        return fn

    chunks = [(qp_ref[pl.ds(c * LANES, LANES), :], pl.ds(c * LANES, LANES), store(c))
              for c in range(TOPK_TK // LANES)]
    for _ in _topk_stages(sk_ref, chunks, ids_ref, pos_ref, scratch):
        pass


def _topk(qp, sub_keys, ids, pos):
    n_lc = TOPK_TK // LANES
    return pl.pallas_call(
        _topk_kernel,
        grid=(1, PEER_HEADS),
        in_specs=[
            pl.BlockSpec((TOPK_TK, 2 * PEER_HALF), lambda i, h: (i, h)),
            pl.BlockSpec((1, 2, N_SUB_KEYS, PEER_HALF), lambda i, h: (h, 0, 0, 0)),
            pl.BlockSpec((N_SUB_KEYS, LANES), lambda i, h: (0, 0)),
            pl.BlockSpec((CAND_ROWS, LANES), lambda i, h: (0, 0)),
        ],
        out_specs=[
            pl.BlockSpec((n_lc, PEER_TOPK, LANES), lambda i, h: (0, h, 0)),
            pl.BlockSpec((n_lc, PEER_TOPK, LANES), lambda i, h: (0, h, 0)),
        ],
        out_shape=[
            jax.ShapeDtypeStruct((n_lc, N_PICKS, LANES), jnp.int32),
            jax.ShapeDtypeStruct((n_lc, N_PICKS, LANES), F32),
        ],
        scratch_shapes=_topk_scratch(n_lc),
        compiler_params=pltpu.CompilerParams(
            dimension_semantics=("arbitrary", "arbitrary"), vmem_limit_bytes=VMEM_LIMIT),
        name="topk",
    )(qp, sub_keys, ids, pos)


PEER_TM = 512
PEER_CHUNK = 2048
PEER_SLABS = PEER_CHUNK // N_SUB_KEYS
PEER_NCHUNK = (N_SUB_KEYS * N_SUB_KEYS) // PEER_CHUNK
PEER_TG = 16


PEER_SUB = 4
PEER_SELECT_LEAD = 3
PEER_SELECT_PER_STAGE = 3
PEER_LC = PEER_TM // LANES
PEER_LC_STEP = PEER_LC * PEER_HEADS // PEER_NCHUNK
assert PEER_NCHUNK * PEER_LC_STEP == PEER_LC * PEER_HEADS and PEER_TM == TOPK_TK


def _peer_kernel(xn_ref, u_ref, v_ref, e0_ref, g0_ref, h1_ref, qpn_ref, sk_ref, ids_ref, pos_ref,
                 o_ref, hg_ref, et_ref, gt_ref, wa_ref, wb_ref, ebuf_ref, gbuf_ref, *topk_scratch):
    i = pl.program_id(0)
    j = pl.program_id(1)
    cur = i % 2

    @pl.when(jnp.logical_and(i == 0, j == 0))
    def _():
        ebuf_ref[0] = e0_ref[...]
        gbuf_ref[0] = g0_ref[...]

    @pl.when(j == 0)
    def _():
        for c in range(PEER_LC):
            et_ref[c * LANES:(c + 1) * LANES, :] = ebuf_ref[cur, c].T
            gt_ref[c * LANES:(c + 1) * LANES, :] = gbuf_ref[cur, c].T
        sub = lax.broadcasted_iota(jnp.int32, (N_SUB_KEYS, N_PICKS), 0)

        def scatter(gi, w_ref):
            t0 = gi * PEER_TG if isinstance(gi, int) else pl.multiple_of(gi * PEER_TG, PEER_TG)
            for tt in range(PEER_TG):
                e_row = et_ref[pl.ds(t0 + tt, 1), :]
                g_row = gt_ref[pl.ds(t0 + tt, 1), :]
                first = jnp.where(sub == (e_row >> 7), g_row, 0.0).astype(BF16)
                second = jnp.where(sub == (e_row & (N_SUB_KEYS - 1)), 1.0, 0.0).astype(BF16)
                w = _dot_nt(first, second)
                for v in range(N_SUB_KEYS // SUBLANES):
                    w_ref[v, pl.ds(tt * SUBLANES, SUBLANES), :] = w[v * SUBLANES:(v + 1) * SUBLANES, :]

        def relayout(gi, w_ref):
            t0 = gi * PEER_TG if isinstance(gi, int) else pl.multiple_of(gi * PEER_TG, PEER_TG)
            for i1 in range(N_SUB_KEYS):
                v, r = divmod(i1, SUBLANES)
                rows = w_ref[v, pl.ds(r, PEER_TG, stride=SUBLANES), :]
                c, s = divmod(i1, PEER_SLABS)
                hg_ref[c, pl.ds(t0, PEER_TG), s * N_SUB_KEYS:(s + 1) * N_SUB_KEYS] = rows.astype(BF16)

        def pair(k, carry):
            scatter(2 * k + 1, wb_ref)
            relayout(2 * k, wa_ref)
            scatter(2 * k + 2, wa_ref)
            relayout(2 * k + 1, wb_ref)
            return carry

        n_groups = PEER_TM // PEER_TG
        scatter(0, wa_ref)
        lax.fori_loop(0, n_groups // 2 - 1, pair, 0)
        scatter(n_groups - 1, wb_ref)
        relayout(n_groups - 2, wa_ref)
        relayout(n_groups - 1, wb_ref)
        o_ref[...] = h1_ref[...]

    def expert_stages():
        xn = xn_ref[...]
        sub = PEER_CHUNK // PEER_SUB
        hs = []
        for q in range(PEER_SUB):
            es = slice(q * sub, (q + 1) * sub)
            a = _dot_nt(xn, u_ref[es, :])
            hq = _gelu(a) * hg_ref[j, :, es].astype(F32)
            hs.append(hq.astype(BF16))
            yield hq[:SUBLANES, :LANES] + hq[-SUBLANES:, -LANES:]
        for q in range(PEER_SUB):
            es = slice(q * sub, (q + 1) * sub)
            r = _dot(hs[q], v_ref[es, :])
            o_ref[...] += r
            yield r[:SUBLANES, :LANES]

    head_rows = pl.ds(pl.multiple_of((j * PEER_LC_STEP // PEER_LC) * PEER_TOPK, PEER_TOPK), PEER_TOPK)

    def store(lc):
        def fn(e, g):
            ebuf_ref[1 - cur, lc, head_rows, :] = e
            gbuf_ref[1 - cur, lc, head_rows, :] = g
        return fn

    chunks = []
    for c in range(PEER_LC_STEP):
        lc = (j * PEER_LC_STEP) % PEER_LC + c
        qp = qpn_ref[pl.ds(pl.multiple_of(lc * LANES, LANES), LANES), :]
        chunks.append((qp, pl.ds(c * LANES, LANES), store(lc)))
    select = _topk_stages(sk_ref, chunks, ids_ref, pos_ref, topk_scratch)
    experts = expert_stages()
    live = []
    for _ in range(PEER_SELECT_LEAD):
        live = next(select, [])
    for result_tile in experts:
        _order_after(live, result_tile)
        for _ in range(PEER_SELECT_PER_STAGE):
            live = next(select, [])
    for _ in select:
        pass


def _peer(xn2, u_tab, v_tab, e0, g0, h1, qp, sub_keys, ids, pos):
    t, d = xn2.shape
    nblk = t // PEER_TM
    steps_per_head = PEER_LC // PEER_LC_STEP
    const3 = lambda i, j: (0, 0, 0)
    return pl.pallas_call(
        _peer_kernel,
        grid=(nblk, PEER_NCHUNK),
        in_specs=[
            pl.BlockSpec((PEER_TM, d), lambda i, j: (i, 0)),
            pl.BlockSpec((PEER_CHUNK, d), lambda i, j: (j, 0)),
            pl.BlockSpec((PEER_CHUNK, d), lambda i, j: (j, 0)),
            pl.BlockSpec((PEER_LC, N_PICKS, LANES), const3),
            pl.BlockSpec((PEER_LC, N_PICKS, LANES), const3),
            pl.BlockSpec((PEER_TM, d), lambda i, j: (i, 0)),
            pl.BlockSpec((PEER_TM, 2 * PEER_HALF),
                         lambda i, j: (jnp.minimum(i + 1, nblk - 1), j // steps_per_head)),
            pl.BlockSpec((1, 2, N_SUB_KEYS, PEER_HALF), lambda i, j: (j // steps_per_head, 0, 0, 0)),
            pl.BlockSpec((N_SUB_KEYS, LANES), lambda i, j: (0, 0)),
            pl.BlockSpec((CAND_ROWS, LANES), lambda i, j: (0, 0)),
        ],
        out_specs=pl.BlockSpec((PEER_TM, d), lambda i, j: (i, 0)),
        out_shape=jax.ShapeDtypeStruct((t, d), F32),
        scratch_shapes=[
            pltpu.VMEM((PEER_NCHUNK, PEER_TM, PEER_CHUNK), BF16),
            pltpu.VMEM((PEER_TM, N_PICKS), jnp.int32),
            pltpu.VMEM((PEER_TM, N_PICKS), F32),
            pltpu.VMEM((N_SUB_KEYS // SUBLANES, PEER_TG * SUBLANES, N_SUB_KEYS), F32),
            pltpu.VMEM((N_SUB_KEYS // SUBLANES, PEER_TG * SUBLANES, N_SUB_KEYS), F32),
            pltpu.VMEM((2, PEER_LC, N_PICKS, LANES), jnp.int32),
            pltpu.VMEM((2, PEER_LC, N_PICKS, LANES), F32),
        ] + _topk_scratch(PEER_LC_STEP),
        compiler_params=pltpu.CompilerParams(
            dimension_semantics=("arbitrary", "arbitrary"), vmem_limit_bytes=VMEM_LIMIT),
        name="peer",
    )(xn2, u_tab, v_tab, e0, g0, h1, qp, sub_keys, ids, pos)


def _s5_discretise_kernel(lr_ref, li_ref, ls_ref, br_ref, bi_ref, ar_ref, ai_ref, bbr_ref, bbi_ref):
    lr, li = lr_ref[...], li_ref[...]
    step = jnp.exp(ls_ref[...])
    mag = jnp.exp(lr * step)
    abar_re = mag * jnp.cos(li * step)
    abar_im = mag * jnp.sin(li * step)
    den = lr * lr + li * li
    nr, ni = abar_re - 1.0, abar_im
    coef_re = (nr * lr + ni * li) / den
    coef_im = (ni * lr - nr * li) / den
    ar_ref[...] = abar_re
    ai_ref[...] = abar_im
    br, bi = br_ref[...], bi_ref[...]
    bbr_ref[...] = coef_re * br - coef_im * bi
    bbi_ref[...] = coef_re * bi + coef_im * br


def _s5_params(lam_re, lam_im, log_step, b_re, b_im, c_re, c_im, bsz):
    col = lambda m: m.reshape(S5_LANES, 1)
    pole = jax.ShapeDtypeStruct((S5_LANES, 1), F32)
    pole_in = jax.ShapeDtypeStruct((S5_LANES, S5_GROUP), F32)
    abar_re, abar_im, bbar_re, bbar_im = pl.pallas_call(
        _s5_discretise_kernel, out_shape=[pole, pole, pole_in, pole_in], name="s5_discretise",
    )(col(lam_re), col(lam_im), col(jnp.repeat(log_step, S5_STATE)),
      b_re.reshape(S5_LANES, S5_GROUP), b_im.reshape(S5_LANES, S5_GROUP))
    bbar_re = bbar_re.reshape(S5_GROUPS, S5_STATE, S5_GROUP)
    bbar_im = bbar_im.reshape(S5_GROUPS, S5_STATE, S5_GROUP)
    eye = jnp.eye(S5_GROUPS, dtype=F32)
    to_b = lambda m: jnp.einsum('gpc,gh->gchp', m, eye).reshape(S5_WIDTH, S5_LANES)
    to_c = lambda m: jnp.einsum('gcp,gh->gphc', m, eye).reshape(S5_LANES, S5_WIDTH)
    bblk = jnp.concatenate([to_b(bbar_re), to_b(bbar_im)], axis=1).astype(BF16)
    cblk = jnp.concatenate([to_c(c_re), -to_c(c_im)], axis=0).astype(BF16)
    a = jnp.stack([abar_re.reshape(-1), abar_im.reshape(-1)])
    a8 = jnp.broadcast_to(a[:, None, :], (2, bsz, S5_LANES))
    return a8, bblk, cblk


def kernel(x, ln1_g, w_in, b_gate, s5_lambda_re, s5_lambda_im, s5_log_step, s5_b_re, s5_b_im, s5_c_re, s5_c_im, s5_d, s5_w_glu, w_s5_branch, q_norm_g, k_norm_g, rel_bias_table, attn_sinks, w_attn_branch, w_out, ln2_g, peer_w_query, peer_sub_keys, peer_u, peer_v):
    bsz, seq, d = x.shape
    h = x
    for layer in range(ln1_g.shape[0]):
        w_uqkv = w_in[layer, :, :UQKV_WIDTH].astype(BF16)
        w_gate = w_in[layer, :, UQKV_WIDTH:].astype(BF16)
        u_tm, q, k, v = _in_proj(h, ln1_g[layer][None], w_uqkv)

        a8, bblk, cblk = _s5_params(s5_lambda_re[layer], s5_lambda_im[layer], s5_log_step[layer],
                                    s5_b_re[layer], s5_b_im[layer], s5_c_re[layer], s5_c_im[layer], bsz)
        ys5 = _s5(u_tm, a8, bblk, cblk, s5_d[layer].reshape(1, S5_WIDTH),
                  s5_w_glu[layer].astype(BF16), w_s5_branch[layer].astype(BF16), bsz)

        attn = _attention(q, k, v, q_norm_g[layer][None], k_norm_g[layer][None],
                          _t5_bucket_matrix(), rel_bias_table, attn_sinks[layer])

        h1, xn2, qp = _merge(h, attn, ys5, ln1_g[layer][None], w_gate,
                             b_gate[layer][None], w_attn_branch[layer].astype(BF16),
                             w_out[layer].astype(BF16), ln2_g[layer][None],
                             peer_w_query[layer].astype(BF16))

        sub_keys = peer_sub_keys[layer].astype(BF16)
        pos = jnp.asarray(_CAND_POS)
        ids = jnp.asarray(_KEY_IDS)
        e0, g0 = _topk(qp, sub_keys, ids, pos)
        out = _peer(xn2, peer_u[layer].astype(BF16), peer_v[layer].astype(BF16), e0, g0, h1,
                    qp, sub_keys, ids, pos)
        h = out.reshape(bsz, seq, d)
    return h
```

```python
import functools
import math

import jax
import jax.numpy as jnp
import numpy as np
from jax import lax
from jax.experimental import pallas as pl
from jax.experimental.pallas import tpu as pltpu

F32 = jnp.float32
BF16 = jnp.bfloat16

D_MODEL = 1024
RMS_EPS = 1e-6
NEG_INF = -1e30

S5_WIDTH = 512
S5_GROUP = 16
S5_GROUPS = 32
S5_STATE = 64
S5_LANES = S5_GROUPS * S5_STATE

N_Q_HEADS = 16
N_KV_HEADS = 4
HEAD_DIM = 64
Q_PER_KV = 4
ATTN_BLOCK = 128
N_BUCKETS = 32
MAX_DISTANCE = 128
Q_WIDTH = 1024
KV_WIDTH = 256
UQKV_WIDTH = S5_WIDTH + Q_WIDTH + 2 * KV_WIDTH

PEER_HEADS = 8
PEER_HALF = 64
N_SUB_KEYS = 128
PEER_TOPK = 16
N_PICKS = PEER_HEADS * PEER_TOPK

VMEM_LIMIT = 56 * 1024 * 1024

SQRT_HALF = 0.7071067811865476


def _rms(x, g):
    r = lax.rsqrt(jnp.mean(x * x, axis=-1, keepdims=True) + RMS_EPS)
    return (x * r) * g


def _gelu(x):
    return 0.5 * x * (1.0 + lax.erf(x * SQRT_HALF))


def _dot(a, b):
    return jnp.dot(a, b, preferred_element_type=F32)


def _dot_nt(a, b):
    return lax.dot_general(a, b, (((1,), (1,)), ((), ())), preferred_element_type=F32)


S5_TC = 64


def _in_proj_kernel(bsz, x_ref, g_ref, w_ref, u_ref, q_ref, k_ref, v_ref):
    d = x_ref.shape[-1]
    xn = _rms(x_ref[...].reshape(bsz * S5_TC, d), g_ref[...]).astype(BF16)
    p = _dot(xn, w_ref[...])
    u = p[:, :S5_WIDTH].reshape(bsz, S5_TC, S5_WIDTH)
    u_ref[...] = jnp.transpose(u, (1, 0, 2)).reshape(S5_TC * bsz, S5_WIDTH)
    q_ref[...] = p[:, S5_WIDTH:S5_WIDTH + Q_WIDTH].astype(BF16).reshape(bsz, S5_TC, Q_WIDTH)
    k_ref[...] = p[:, S5_WIDTH + Q_WIDTH:S5_WIDTH + Q_WIDTH + KV_WIDTH].astype(BF16).reshape(
        bsz, S5_TC, KV_WIDTH)
    v_ref[...] = p[:, S5_WIDTH + Q_WIDTH + KV_WIDTH:].astype(BF16).reshape(bsz, S5_TC, KV_WIDTH)


def _in_proj(x, ln1_g, w_uqkv):
    bsz, seq, d = x.shape
    return pl.pallas_call(
        functools.partial(_in_proj_kernel, bsz),
        grid=(seq // S5_TC,),
        in_specs=[
            pl.BlockSpec((bsz, S5_TC, d), lambda i: (0, i, 0)),
            pl.BlockSpec((1, d), lambda i: (0, 0)),
            pl.BlockSpec((d, UQKV_WIDTH), lambda i: (0, 0)),
        ],
        out_specs=[
            pl.BlockSpec((S5_TC * bsz, S5_WIDTH), lambda i: (i, 0)),
            pl.BlockSpec((bsz, S5_TC, Q_WIDTH), lambda i: (0, i, 0)),
            pl.BlockSpec((bsz, S5_TC, KV_WIDTH), lambda i: (0, i, 0)),
            pl.BlockSpec((bsz, S5_TC, KV_WIDTH), lambda i: (0, i, 0)),
        ],
        out_shape=[
            jax.ShapeDtypeStruct((seq * bsz, S5_WIDTH), F32),
            jax.ShapeDtypeStruct((bsz, seq, Q_WIDTH), BF16),
            jax.ShapeDtypeStruct((bsz, seq, KV_WIDTH), BF16),
            jax.ShapeDtypeStruct((bsz, seq, KV_WIDTH), BF16),
        ],
        compiler_params=pltpu.CompilerParams(
            dimension_semantics=("arbitrary",), vmem_limit_bytes=VMEM_LIMIT),
        name="in_proj",
    )(x, ln1_g, w_uqkv)


S5_LC = 512
S5_CH = S5_LC // S5_STATE * S5_GROUP


def _s5_kernel(bsz, u_ref, a_ref, bblk_ref, cblk_ref, d_ref, wglu_ref, wbr_ref, o_ref,
               h_ref, bu_ref):
    @pl.when(pl.program_id(0) == 0)
    def _():
        h_ref[...] = jnp.zeros_like(h_ref)

    u = u_ref[...]
    ub = u.astype(BF16)
    for lc in range(S5_LANES // S5_LC):
        ch = slice(lc * S5_CH, (lc + 1) * S5_CH)
        for part in range(2):
            cols = pl.ds(part * S5_LANES + lc * S5_LC, S5_LC)
            bu_ref[:, cols] = _dot(ub[:, ch], bblk_ref[ch, cols])

    for lc in range(S5_LANES // S5_LC):
        re = pl.ds(lc * S5_LC, S5_LC)
        im = pl.ds(S5_LANES + lc * S5_LC, S5_LC)
        ar = a_ref[0, :, re]
        ai = a_ref[1, :, re]

        hr, hi = h_ref[0, :, re], h_ref[1, :, re]
        for t in range(S5_TC):
            rows = pl.ds(t * bsz, bsz)
            nr = ar * hr - ai * hi + bu_ref[rows, re]
            ni = ar * hi + ai * hr + bu_ref[rows, im]
            bu_ref[rows, re] = nr
            bu_ref[rows, im] = ni
            hr, hi = nr, ni
        h_ref[0, :, re] = hr
        h_ref[1, :, re] = hi

    ys = []
    for lc in range(S5_LANES // S5_LC):
        ch = slice(lc * S5_CH, (lc + 1) * S5_CH)
        re = pl.ds(lc * S5_LC, S5_LC)
        im = pl.ds(S5_LANES + lc * S5_LC, S5_LC)
        ys.append(_dot(bu_ref[:, re].astype(BF16), cblk_ref[re, ch])
                  + _dot(bu_ref[:, im].astype(BF16), cblk_ref[im, ch]))
    y = jnp.concatenate(ys, axis=1) + d_ref[...] * u
    yg = _gelu(y).astype(BF16)
    ab = _dot(yg, wglu_ref[...])
    glu = ab[:, :S5_WIDTH] * jax.nn.sigmoid(ab[:, S5_WIDTH:])
    glu = jnp.transpose(glu.reshape(S5_TC, bsz, S5_WIDTH), (1, 0, 2)).reshape(bsz * S5_TC, S5_WIDTH)
    o_ref[...] = _dot(glu.astype(BF16), wbr_ref[...]).astype(BF16).reshape(bsz, S5_TC, D_MODEL)


def _s5(u_tm, a8, bblk, cblk, d_row, w_glu, w_br, bsz):
    rows = u_tm.shape[0]
    tr = S5_TC * bsz
    const = lambda shape: pl.BlockSpec(shape, lambda i: (0,) * len(shape))
    return pl.pallas_call(
        functools.partial(_s5_kernel, bsz),
        grid=(rows // tr,),
        in_specs=[
            pl.BlockSpec((tr, S5_WIDTH), lambda i: (i, 0)),
            const((2, bsz, S5_LANES)),
            const((S5_WIDTH, 2 * S5_LANES)),
            const((2 * S5_LANES, S5_WIDTH)),
            const((1, S5_WIDTH)),
            const((S5_WIDTH, 2 * S5_WIDTH)),
            const((S5_WIDTH, D_MODEL)),
        ],
        out_specs=pl.BlockSpec((bsz, S5_TC, D_MODEL), lambda i: (0, i, 0)),
        out_shape=jax.ShapeDtypeStruct((bsz, rows // bsz, D_MODEL), BF16),
        scratch_shapes=[
            pltpu.VMEM((2, bsz, S5_LANES), F32),
            pltpu.VMEM((tr, 2 * S5_LANES), F32),
        ],
        compiler_params=pltpu.CompilerParams(
            dimension_semantics=("arbitrary",), vmem_limit_bytes=VMEM_LIMIT),
        name="s5",
    )(u_tm, a8, bblk, cblk, d_row, w_glu, w_br)


def _attn_kernel(q_ref, kc_ref, kp_ref, vc_ref, vp_ref, qg_ref, kg_ref, bucket_ref,
                 table_ref, sink_ref, o_ref, bias_ref):
    first = jnp.logical_and(pl.program_id(0) == 0, pl.program_id(1) == 0)

    @pl.when(first)
    def _():
        bucket = bucket_ref[...]
        for h in range(N_Q_HEADS):
            acc = jnp.zeros((ATTN_BLOCK, 2 * ATTN_BLOCK), F32)
            for b in range(N_BUCKETS):
                acc = jnp.where(bucket == b, table_ref[b, h], acc)
            bias_ref[h] = acc

    blk = pl.program_id(1)
    qi = lax.broadcasted_iota(jnp.int32, (ATTN_BLOCK, 2 * ATTN_BLOCK), 0)
    si = lax.broadcasted_iota(jnp.int32, (ATTN_BLOCK, 2 * ATTN_BLOCK), 1)
    dist = ATTN_BLOCK + qi - si
    band = (dist >= 0) & (dist < ATTN_BLOCK)
    valids = [band & ((si >= ATTN_BLOCK) | (blk > 0))] + [band] * (ATTN_QB - 1)
    qg = qg_ref[...]
    kg = kg_ref[...]
    scale = HEAD_DIM ** -0.5

    grp_rows = Q_PER_KV * ATTN_BLOCK
    row_head = lax.broadcasted_iota(jnp.int32, (grp_rows, 1), 0) // ATTN_BLOCK

    ones_d = jnp.ones((HEAD_DIM, LANES), BF16)
    ones_s = jnp.ones((2 * ATTN_BLOCK, LANES), BF16)

    def rms_rows(x, g):
        ssq = _dot((x * x).astype(BF16), ones_d)[:, :HEAD_DIM]
        return (x * lax.rsqrt(ssq * (1.0 / HEAD_DIM) + RMS_EPS)) * g

    groups = [(sb, kh) for sb in range(ATTN_QB) for kh in range(N_KV_HEADS)]
    knorm, vall = [], []
    for kh in range(N_KV_HEADS):
        cols = slice(kh * HEAD_DIM, (kh + 1) * HEAD_DIM)
        kall = jnp.concatenate([kp_ref[0, :, cols], kc_ref[0, :, cols]], axis=0).astype(F32)
        knorm.append(rms_rows(kall, kg).astype(BF16))
        vall.append(jnp.concatenate([vp_ref[0, :, cols], vc_ref[0, :, cols]], axis=0))
    qss = []
    for sb, kh in groups:
        h0 = kh * Q_PER_KV
        rows = slice(sb * ATTN_BLOCK, (sb + 1) * ATTN_BLOCK)
        qs = jnp.concatenate(
            [q_ref[0, rows, (h0 + g) * HEAD_DIM:(h0 + g + 1) * HEAD_DIM] for g in range(Q_PER_KV)],
            axis=0).astype(F32)
        qss.append((rms_rows(qs, qg) * scale).astype(BF16))
    ss, sinks = [], []
    for gi, (sb, kh) in enumerate(groups):
        h0 = kh * Q_PER_KV
        keys = slice(sb * ATTN_BLOCK, (sb + 2) * ATTN_BLOCK)
        s = _dot_nt(qss[gi], knorm[kh][keys]).reshape(Q_PER_KV, ATTN_BLOCK, 2 * ATTN_BLOCK)
        ss.append(jnp.where(valids[sb][None], s + bias_ref[h0:h0 + Q_PER_KV], NEG_INF).reshape(
            grp_rows, 2 * ATTN_BLOCK))
        sink = jnp.full((grp_rows, 1), sink_ref[h0], F32)
        for g in range(1, Q_PER_KV):
            sink = jnp.where(row_head == g, sink_ref[h0 + g], sink)
        sinks.append(sink)
    ps, dens = [], []
    for gi in range(len(groups)):
        m = jnp.maximum(jnp.max(ss[gi], axis=-1, keepdims=True), sinks[gi])
        p = jnp.exp(ss[gi] - m).astype(BF16)
        ps.append(p)
        dens.append(_dot(p, ones_s)[:, :HEAD_DIM] + jnp.exp(sinks[gi] - m))
    for gi, (sb, kh) in enumerate(groups):
        h0 = kh * Q_PER_KV
        keys = slice(sb * ATTN_BLOCK, (sb + 2) * ATTN_BLOCK)
        o = (_dot(ps[gi], vall[kh][keys]) / dens[gi]).astype(BF16)
        for g in range(Q_PER_KV):
            o_ref[0, sb * ATTN_BLOCK:(sb + 1) * ATTN_BLOCK, (h0 + g) * HEAD_DIM:(h0 + g + 1) * HEAD_DIM] = (
                o[g * ATTN_BLOCK:(g + 1) * ATTN_BLOCK])


ATTN_QB = 2


def _attention(q, k, v, qg, kg, bucket, table, sinks):
    bsz, seq, _ = q.shape
    rows = ATTN_QB * ATTN_BLOCK
    cur = lambda b, i: (b, i, 0)
    prev = lambda b, i: (b, jnp.maximum(i * ATTN_QB - 1, 0), 0)
    const2 = lambda b, i: (0, 0)
    return pl.pallas_call(
        _attn_kernel,
        grid=(bsz, seq // rows),
        in_specs=[
            pl.BlockSpec((1, rows, Q_WIDTH), cur),
            pl.BlockSpec((1, rows, KV_WIDTH), cur),
            pl.BlockSpec((1, ATTN_BLOCK, KV_WIDTH), prev),
            pl.BlockSpec((1, rows, KV_WIDTH), cur),
            pl.BlockSpec((1, ATTN_BLOCK, KV_WIDTH), prev),
            pl.BlockSpec((1, HEAD_DIM), const2),
            pl.BlockSpec((1, HEAD_DIM), const2),
            pl.BlockSpec((ATTN_BLOCK, 2 * ATTN_BLOCK), const2),
            pl.BlockSpec(memory_space=pltpu.SMEM),
            pl.BlockSpec(memory_space=pltpu.SMEM),
        ],
        out_specs=pl.BlockSpec((1, rows, Q_WIDTH), cur),
        out_shape=jax.ShapeDtypeStruct((bsz, seq, Q_WIDTH), BF16),
        scratch_shapes=[pltpu.VMEM((N_Q_HEADS, ATTN_BLOCK, 2 * ATTN_BLOCK), F32)],
        compiler_params=pltpu.CompilerParams(
            dimension_semantics=("arbitrary", "arbitrary"), vmem_limit_bytes=VMEM_LIMIT),
        name="attn",
    )(q, k, k, v, v, qg, kg, bucket, table, sinks)


def _t5_bucket_matrix():
    qi = np.arange(ATTN_BLOCK)[:, None]
    si = np.arange(2 * ATTN_BLOCK)[None, :]
    dist = np.maximum(ATTN_BLOCK + qi - si, 0)
    max_exact = N_BUCKETS // 2
    d_f = np.maximum(dist, 1).astype(np.float64)
    large = max_exact + np.floor(np.log(d_f / max_exact) / math.log(MAX_DISTANCE / max_exact)
                                 * (N_BUCKETS - max_exact)).astype(np.int32)
    large = np.minimum(large, N_BUCKETS - 1)
    return jnp.asarray(np.where(dist < max_exact, dist, large).astype(np.int32))


MERGE_TM = 512


def _merge_kernel(x_ref, attn_ref, ys5_ref, g1_ref, wg_ref, bg_ref, wab_ref, wout_ref,
                  g2_ref, wpq_ref, h1_ref, xn2_ref, qp_ref):
    x = x_ref[0]
    xn = _rms(x, g1_ref[...]).astype(BF16)
    gates = jax.nn.sigmoid(_dot(xn, wg_ref[...]) + bg_ref[...])
    y_attn = _dot(attn_ref[0], wab_ref[...])
    y_s5 = ys5_ref[0].astype(F32)
    mix = gates[:, :D_MODEL] * y_s5 + gates[:, D_MODEL:] * y_attn
    h1 = x + _dot(mix.astype(BF16), wout_ref[...])
    h1_ref[...] = h1
    xn2 = _rms(h1, g2_ref[...]).astype(BF16)
    xn2_ref[...] = xn2
    qp_ref[...] = _dot(xn2, wpq_ref[...]).astype(BF16)


def _merge(x, attn, ys5_tm, ln1_g, w_gate, b_gate, w_ab, w_out, ln2_g, w_pq):
    bsz, seq, d = x.shape
    nt = seq // MERGE_TM
    tok = lambda b, i: (b * nt + i, 0)
    const2 = lambda b, i: (0, 0)
    return pl.pallas_call(
        _merge_kernel,
        grid=(bsz, nt),
        in_specs=[
            pl.BlockSpec((1, MERGE_TM, d), lambda b, i: (b, i, 0)),
            pl.BlockSpec((1, MERGE_TM, d), lambda b, i: (b, i, 0)),
            pl.BlockSpec((1, MERGE_TM, d), lambda b, i: (b, i, 0)),
            pl.BlockSpec((1, d), const2),
            pl.BlockSpec((d, 2 * d), const2),
            pl.BlockSpec((1, 2 * d), const2),
            pl.BlockSpec((d, d), const2),
            pl.BlockSpec((d, d), const2),
            pl.BlockSpec((1, d), const2),
            pl.BlockSpec((d, d), const2),
        ],
        out_specs=[
            pl.BlockSpec((MERGE_TM, d), tok),
            pl.BlockSpec((MERGE_TM, d), tok),
            pl.BlockSpec((MERGE_TM, d), tok),
        ],
        out_shape=[
            jax.ShapeDtypeStruct((bsz * seq, d), F32),
            jax.ShapeDtypeStruct((bsz * seq, d), BF16),
            jax.ShapeDtypeStruct((bsz * seq, d), BF16),
        ],
        compiler_params=pltpu.CompilerParams(
            dimension_semantics=("arbitrary", "arbitrary"), vmem_limit_bytes=VMEM_LIMIT),
        name="merge",
    )(x, attn, ys5_tm, ln1_g, w_gate, b_gate, w_ab, w_out, ln2_g, w_pq)


TOPK_TK = 512
LANES = 128
SUBLANES = 8
CAND_ROWS = 80
POS_INVALID = 1.0e9


def _cand_layout():
    pos = np.full((CAND_ROWS,), POS_INVALID, np.float32)
    blocks = [(0, 0), (0, 8), (1, 0), (2, 0), (3, 0), (4, 0), (5, 0), (6, 0), (7, 0)]
    for r, (a, b0) in enumerate(blocks):
        for j in range(8):
            b = b0 + j
            if (a + 1) * (b + 1) <= PEER_TOPK:
                pos[r * 8 + j] = a * PEER_TOPK + b
    for j in range(8):
        pos[72 + j] = (8 + j) * PEER_TOPK
    return blocks, np.broadcast_to(pos[:, None], (CAND_ROWS, LANES)).copy()


_CAND_BLOCKS, _CAND_POS = _cand_layout()
_KEY_IDS = np.broadcast_to(np.arange(N_SUB_KEYS, dtype=np.float32)[:, None], (N_SUB_KEYS, LANES)).copy()


def _extract_round(s, ids, big):
    m = jnp.max(s, axis=0, keepdims=True)
    sel = jnp.min(jnp.where(s == m, ids, big), axis=0, keepdims=True)
    return m, sel, ids == sel


def _topk_stages(sk_ref, chunks, ids_ref, pos_ref, scratch):
    v1_ref, i1_ref, v2_ref, i2_ref, best_ref, s_ref, cand_ref, expert_ref, picked_ref = scratch
    nkeys = float(N_SUB_KEYS)
    tile = slice(0, SUBLANES)
    for qp, lanes, _ in chunks:
        s_ref[0, :, lanes] = _dot_nt(sk_ref[0, 0], qp[:, :PEER_HALF])
        s_ref[1, :, lanes] = _dot_nt(sk_ref[0, 1], qp[:, PEER_HALF:])
    live = [(s_ref, (half, tile, lanes)) for _, lanes, _ in chunks for half in range(2)]
    yield live
    for j in range(PEER_TOPK):
        for _, lanes, _ in chunks:
            for half, (val_ref, id_ref) in enumerate(((v1_ref, i1_ref), (v2_ref, i2_ref))):
                s = s_ref[half, :, lanes]
                m, sel, hit = _extract_round(s, ids_ref[...], nkeys)
                val_ref[j:j + 1, lanes] = m
                id_ref[j:j + 1, lanes] = sel
                s_ref[half, :, lanes] = jnp.where(hit, -jnp.inf, s)
        yield live

    for _, lanes, _ in chunks:
        for r, (a, b0) in enumerate(_CAND_BLOCKS):
            rows = slice(r * SUBLANES, (r + 1) * SUBLANES)
            cand_ref[rows, lanes] = v1_ref[a:a + 1, lanes] + v2_ref[b0:b0 + 8, lanes]
            expert_ref[rows, lanes] = i1_ref[a:a + 1, lanes] * nkeys + i2_ref[b0:b0 + 8, lanes]
        rows = slice(CAND_ROWS - SUBLANES, CAND_ROWS)
        cand_ref[rows, lanes] = v1_ref[8:16, lanes] + v2_ref[0:1, lanes]
        expert_ref[rows, lanes] = i1_ref[8:16, lanes] * nkeys + i2_ref[0:1, lanes]
        cand_ref[:, lanes] = jnp.where(pos_ref[...] < POS_INVALID, cand_ref[:, lanes], -jnp.inf)
    live = [(cand_ref, (tile, lanes)) for _, lanes, _ in chunks]
    yield live
    for j in range(PEER_TOPK):
        for _, lanes, _ in chunks:
            cand = cand_ref[:, lanes]
            m, _, hit = _extract_round(cand, pos_ref[...], POS_INVALID)
            best_ref[j:j + 1, lanes] = m
            picked_ref[j:j + 1, lanes] = jnp.sum(jnp.where(hit, expert_ref[:, lanes], 0.0), axis=0,
                                                 keepdims=True)
            cand_ref[:, lanes] = jnp.where(hit, -jnp.inf, cand)
        if j % 2 == 1:
            yield live
    for _, lanes, store in chunks:
        best = best_ref[:, lanes]
        p = jnp.exp(best - best_ref[0:1, lanes])
        store(picked_ref[:, lanes].astype(jnp.int32), p / jnp.sum(p, axis=0, keepdims=True))
    yield []


def _order_after(live, value_tile):
    zero = jnp.minimum(jnp.abs(value_tile), 0.0)
    for ref, idx in live:
        ref[idx] = ref[idx] + zero


def _topk_scratch(n_chunks):
    w = n_chunks * LANES
    return ([pltpu.VMEM((PEER_TOPK, w), F32) for _ in range(5)]
            + [pltpu.VMEM((2, N_SUB_KEYS, w), F32), pltpu.VMEM((CAND_ROWS, w), F32),
               pltpu.VMEM((CAND_ROWS, w), F32), pltpu.VMEM((PEER_TOPK, w), F32)])


def _topk_kernel(qp_ref, sk_ref, ids_ref, pos_ref, e_ref, g_ref, *scratch):
    def store(c):
        def fn(e, g):
            e_ref[c] = e
            g_ref[c] = g
        return fn

    chunks = [(qp_ref[pl.ds(c * LANES, LANES), :], pl.ds(c * LANES, LANES), store(c))
              for c in range(TOPK_TK // LANES)]
    for _ in _topk_stages(sk_ref, chunks, ids_ref, pos_ref, scratch):
        pass


def _topk(qp, sub_keys, ids, pos):
    n_lc = TOPK_TK // LANES
    return pl.pallas_call(
        _topk_kernel,
        grid=(1, PEER_HEADS),
        in_specs=[
            pl.BlockSpec((TOPK_TK, 2 * PEER_HALF), lambda i, h: (i, h)),
            pl.BlockSpec((1, 2, N_SUB_KEYS, PEER_HALF), lambda i, h: (h, 0, 0, 0)),
            pl.BlockSpec((N_SUB_KEYS, LANES), lambda i, h: (0, 0)),
            pl.BlockSpec((CAND_ROWS, LANES), lambda i, h: (0, 0)),
        ],
        out_specs=[
            pl.BlockSpec((n_lc, PEER_TOPK, LANES), lambda i, h: (0, h, 0)),
            pl.BlockSpec((n_lc, PEER_TOPK, LANES), lambda i, h: (0, h, 0)),
        ],
        out_shape=[
            jax.ShapeDtypeStruct((n_lc, N_PICKS, LANES), jnp.int32),
            jax.ShapeDtypeStruct((n_lc, N_PICKS, LANES), F32),
        ],
        scratch_shapes=_topk_scratch(n_lc),
        compiler_params=pltpu.CompilerParams(
            dimension_semantics=("arbitrary", "arbitrary"), vmem_limit_bytes=VMEM_LIMIT),
        name="topk",
    )(qp, sub_keys, ids, pos)


PEER_TM = 512
PEER_CHUNK = 2048
PEER_SLABS = PEER_CHUNK // N_SUB_KEYS
PEER_NCHUNK = (N_SUB_KEYS * N_SUB_KEYS) // PEER_CHUNK
PEER_TG = 16


PEER_SUB = 4
PEER_SELECT_LEAD = 3
PEER_SELECT_PER_STAGE = 3
PEER_LC = PEER_TM // LANES
PEER_LC_STEP = PEER_LC * PEER_HEADS // PEER_NCHUNK
assert PEER_NCHUNK * PEER_LC_STEP == PEER_LC * PEER_HEADS and PEER_TM == TOPK_TK


def _peer_kernel(xn_ref, u_ref, v_ref, e0_ref, g0_ref, h1_ref, qpn_ref, sk_ref, ids_ref, pos_ref,
                 o_ref, hg_ref, et_ref, gt_ref, wa_ref, wb_ref, ebuf_ref, gbuf_ref, *topk_scratch):
    i = pl.program_id(0)
    j = pl.program_id(1)
    cur = i % 2

    @pl.when(jnp.logical_and(i == 0, j == 0))
    def _():
        ebuf_ref[0] = e0_ref[...]
        gbuf_ref[0] = g0_ref[...]

    @pl.when(j == 0)
    def _():
        for c in range(PEER_LC):
            et_ref[c * LANES:(c + 1) * LANES, :] = ebuf_ref[cur, c].T
            gt_ref[c * LANES:(c + 1) * LANES, :] = gbuf_ref[cur, c].T
        sub = lax.broadcasted_iota(jnp.int32, (N_SUB_KEYS, N_PICKS), 0)

        def scatter(gi, w_ref):
            t0 = gi * PEER_TG if isinstance(gi, int) else pl.multiple_of(gi * PEER_TG, PEER_TG)
            for tt in range(PEER_TG):
                e_row = et_ref[pl.ds(t0 + tt, 1), :]
                g_row = gt_ref[pl.ds(t0 + tt, 1), :]
                first = jnp.where(sub == (e_row >> 7), g_row, 0.0).astype(BF16)
                second = jnp.where(sub == (e_row & (N_SUB_KEYS - 1)), 1.0, 0.0).astype(BF16)
                w = _dot_nt(first, second)
                for v in range(N_SUB_KEYS // SUBLANES):
                    w_ref[v, pl.ds(tt * SUBLANES, SUBLANES), :] = w[v * SUBLANES:(v + 1) * SUBLANES, :]

        def relayout(gi, w_ref):
            t0 = gi * PEER_TG if isinstance(gi, int) else pl.multiple_of(gi * PEER_TG, PEER_TG)
            for i1 in range(N_SUB_KEYS):
                v, r = divmod(i1, SUBLANES)
                rows = w_ref[v, pl.ds(r, PEER_TG, stride=SUBLANES), :]
                c, s = divmod(i1, PEER_SLABS)
                hg_ref[c, pl.ds(t0, PEER_TG), s * N_SUB_KEYS:(s + 1) * N_SUB_KEYS] = rows.astype(BF16)

        def pair(k, carry):
            scatter(2 * k + 1, wb_ref)
            relayout(2 * k, wa_ref)
            scatter(2 * k + 2, wa_ref)
            relayout(2 * k + 1, wb_ref)
            return carry

        n_groups = PEER_TM // PEER_TG
        scatter(0, wa_ref)
        lax.fori_loop(0, n_groups // 2 - 1, pair, 0)
        scatter(n_groups - 1, wb_ref)
        relayout(n_groups - 2, wa_ref)
        relayout(n_groups - 1, wb_ref)
        o_ref[...] = h1_ref[...]

    def expert_stages():
        xn = xn_ref[...]
        sub = PEER_CHUNK // PEER_SUB
        hs = []
        for q in range(PEER_SUB):
            es = slice(q * sub, (q + 1) * sub)
            a = _dot_nt(xn, u_ref[es, :])
            hq = _gelu(a) * hg_ref[j, :, es].astype(F32)
            hs.append(hq.astype(BF16))
            yield hq[:SUBLANES, :LANES] + hq[-SUBLANES:, -LANES:]
        for q in range(PEER_SUB):
            es = slice(q * sub, (q + 1) * sub)
            r = _dot(hs[q], v_ref[es, :])
            o_ref[...] += r
            yield r[:SUBLANES, :LANES]

    head_rows = pl.ds(pl.multiple_of((j * PEER_LC_STEP // PEER_LC) * PEER_TOPK, PEER_TOPK), PEER_TOPK)

    def store(lc):
        def fn(e, g):
            ebuf_ref[1 - cur, lc, head_rows, :] = e
            gbuf_ref[1 - cur, lc, head_rows, :] = g
        return fn

    chunks = []
    for c in range(PEER_LC_STEP):
        lc = (j * PEER_LC_STEP) % PEER_LC + c
        qp = qpn_ref[pl.ds(pl.multiple_of(lc * LANES, LANES), LANES), :]
        chunks.append((qp, pl.ds(c * LANES, LANES), store(lc)))
    select = _topk_stages(sk_ref, chunks, ids_ref, pos_ref, topk_scratch)
    experts = expert_stages()
    live = []
    for _ in range(PEER_SELECT_LEAD):
        live = next(select, [])
    for result_tile in experts:
        _order_after(live, result_tile)
        for _ in range(PEER_SELECT_PER_STAGE):
            live = next(select, [])
    for _ in select:
        pass


def _peer(xn2, u_tab, v_tab, e0, g0, h1, qp, sub_keys, ids, pos):
    t, d = xn2.shape
    nblk = t // PEER_TM
    steps_per_head = PEER_LC // PEER_LC_STEP
    const3 = lambda i, j: (0, 0, 0)
    return pl.pallas_call(
        _peer_kernel,
        grid=(nblk, PEER_NCHUNK),
        in_specs=[
            pl.BlockSpec((PEER_TM, d), lambda i, j: (i, 0)),
            pl.BlockSpec((PEER_CHUNK, d), lambda i, j: (j, 0)),
            pl.BlockSpec((PEER_CHUNK, d), lambda i, j: (j, 0)),
            pl.BlockSpec((PEER_LC, N_PICKS, LANES), const3),
            pl.BlockSpec((PEER_LC, N_PICKS, LANES), const3),
            pl.BlockSpec((PEER_TM, d), lambda i, j: (i, 0)),
            pl.BlockSpec((PEER_TM, 2 * PEER_HALF),
                         lambda i, j: (jnp.minimum(i + 1, nblk - 1), j // steps_per_head)),
            pl.BlockSpec((1, 2, N_SUB_KEYS, PEER_HALF), lambda i, j: (j // steps_per_head, 0, 0, 0)),
            pl.BlockSpec((N_SUB_KEYS, LANES), lambda i, j: (0, 0)),
            pl.BlockSpec((CAND_ROWS, LANES), lambda i, j: (0, 0)),
        ],
        out_specs=pl.BlockSpec((PEER_TM, d), lambda i, j: (i, 0)),
        out_shape=jax.ShapeDtypeStruct((t, d), F32),
        scratch_shapes=[
            pltpu.VMEM((PEER_NCHUNK, PEER_TM, PEER_CHUNK), BF16),
            pltpu.VMEM((PEER_TM, N_PICKS), jnp.int32),
            pltpu.VMEM((PEER_TM, N_PICKS), F32),
            pltpu.VMEM((N_SUB_KEYS // SUBLANES, PEER_TG * SUBLANES, N_SUB_KEYS), F32),
            pltpu.VMEM((N_SUB_KEYS // SUBLANES, PEER_TG * SUBLANES, N_SUB_KEYS), F32),
            pltpu.VMEM((2, PEER_LC, N_PICKS, LANES), jnp.int32),
            pltpu.VMEM((2, PEER_LC, N_PICKS, LANES), F32),
        ] + _topk_scratch(PEER_LC_STEP),
        compiler_params=pltpu.CompilerParams(
            dimension_semantics=("arbitrary", "arbitrary"), vmem_limit_bytes=VMEM_LIMIT),
        name="peer",
    )(xn2, u_tab, v_tab, e0, g0, h1, qp, sub_keys, ids, pos)


def _s5_discretise_kernel(lr_ref, li_ref, ls_ref, br_ref, bi_ref, ar_ref, ai_ref, bbr_ref, bbi_ref):
    lr, li = lr_ref[...], li_ref[...]
    step = jnp.exp(ls_ref[...])
    mag = jnp.exp(lr * step)
    abar_re = mag * jnp.cos(li * step)
    abar_im = mag * jnp.sin(li * step)
    den = lr * lr + li * li
    nr, ni = abar_re - 1.0, abar_im
    coef_re = (nr * lr + ni * li) / den
    coef_im = (ni * lr - nr * li) / den
    ar_ref[...] = abar_re
    ai_ref[...] = abar_im
    br, bi = br_ref[...], bi_ref[...]
    bbr_ref[...] = coef_re * br - coef_im * bi
    bbi_ref[...] = coef_re * bi + coef_im * br


def _s5_params(lam_re, lam_im, log_step, b_re, b_im, c_re, c_im, bsz):
    col = lambda m: m.reshape(S5_LANES, 1)
    pole = jax.ShapeDtypeStruct((S5_LANES, 1), F32)
    pole_in = jax.ShapeDtypeStruct((S5_LANES, S5_GROUP), F32)
    abar_re, abar_im, bbar_re, bbar_im = pl.pallas_call(
        _s5_discretise_kernel, out_shape=[pole, pole, pole_in, pole_in], name="s5_discretise",
    )(col(lam_re), col(lam_im), col(jnp.repeat(log_step, S5_STATE)),
      b_re.reshape(S5_LANES, S5_GROUP), b_im.reshape(S5_LANES, S5_GROUP))
    bbar_re = bbar_re.reshape(S5_GROUPS, S5_STATE, S5_GROUP)
    bbar_im = bbar_im.reshape(S5_GROUPS, S5_STATE, S5_GROUP)
    eye = jnp.eye(S5_GROUPS, dtype=F32)
    to_b = lambda m: jnp.einsum('gpc,gh->gchp', m, eye).reshape(S5_WIDTH, S5_LANES)
    to_c = lambda m: jnp.einsum('gcp,gh->gphc', m, eye).reshape(S5_LANES, S5_WIDTH)
    bblk = jnp.concatenate([to_b(bbar_re), to_b(bbar_im)], axis=1).astype(BF16)
    cblk = jnp.concatenate([to_c(c_re), -to_c(c_im)], axis=0).astype(BF16)
    a = jnp.stack([abar_re.reshape(-1), abar_im.reshape(-1)])
    a8 = jnp.broadcast_to(a[:, None, :], (2, bsz, S5_LANES))
    return a8, bblk, cblk


def kernel(x, ln1_g, w_in, b_gate, s5_lambda_re, s5_lambda_im, s5_log_step, s5_b_re, s5_b_im, s5_c_re, s5_c_im, s5_d, s5_w_glu, w_s5_branch, q_norm_g, k_norm_g, rel_bias_table, attn_sinks, w_attn_branch, w_out, ln2_g, peer_w_query, peer_sub_keys, peer_u, peer_v):
    bsz, seq, d = x.shape
    h = x
    for layer in range(ln1_g.shape[0]):
        w_uqkv = w_in[layer, :, :UQKV_WIDTH].astype(BF16)
        w_gate = w_in[layer, :, UQKV_WIDTH:].astype(BF16)
        u_tm, q, k, v = _in_proj(h, ln1_g[layer][None], w_uqkv)

        a8, bblk, cblk = _s5_params(s5_lambda_re[layer], s5_lambda_im[layer], s5_log_step[layer],
                                    s5_b_re[layer], s5_b_im[layer], s5_c_re[layer], s5_c_im[layer], bsz)
        ys5 = _s5(u_tm, a8, bblk, cblk, s5_d[layer].reshape(1, S5_WIDTH),
                  s5_w_glu[layer].astype(BF16), w_s5_branch[layer].astype(BF16), bsz)

        attn = _attention(q, k, v, q_norm_g[layer][None], k_norm_g[layer][None],
                          _t5_bucket_matrix(), rel_bias_table, attn_sinks[layer])

        h1, xn2, qp = _merge(h, attn, ys5, ln1_g[layer][None], w_gate,
                             b_gate[layer][None], w_attn_branch[layer].astype(BF16),
                             w_out[layer].astype(BF16), ln2_g[layer][None],
                             peer_w_query[layer].astype(BF16))

        sub_keys = peer_sub_keys[layer].astype(BF16)
        pos = jnp.asarray(_CAND_POS)
        ids = jnp.asarray(_KEY_IDS)
        e0, g0 = _topk(qp, sub_keys, ids, pos)
        out = _peer(xn2, peer_u[layer].astype(BF16), peer_v[layer].astype(BF16), e0, g0, h1,
                    qp, sub_keys, ids, pos)
        h = out.reshape(bsz, seq, d)
    return h
```

```python
import functools
import math

import jax
import jax.numpy as jnp
import numpy as np
from jax import lax
from jax.experimental import pallas as pl
from jax.experimental.pallas import tpu as pltpu

F32 = jnp.float32
BF16 = jnp.bfloat16

D_MODEL = 1024
RMS_EPS = 1e-6
NEG_INF = -1e30

S5_WIDTH = 512
S5_GROUP = 16
S5_GROUPS = 32
S5_STATE = 64
S5_LANES = S5_GROUPS * S5_STATE

N_Q_HEADS = 16
N_KV_HEADS = 4
HEAD_DIM = 64
Q_PER_KV = 4
ATTN_BLOCK = 128
N_BUCKETS = 32
MAX_DISTANCE = 128
Q_WIDTH = 1024
KV_WIDTH = 256
UQKV_WIDTH = S5_WIDTH + Q_WIDTH + 2 * KV_WIDTH

PEER_HEADS = 8
PEER_HALF = 64
N_SUB_KEYS = 128
PEER_TOPK = 16
N_PICKS = PEER_HEADS * PEER_TOPK

VMEM_LIMIT = 56 * 1024 * 1024

SQRT_HALF = 0.7071067811865476


def _rms(x, g):
    r = lax.rsqrt(jnp.mean(x * x, axis=-1, keepdims=True) + RMS_EPS)
    return (x * r) * g


def _gelu(x):
    return 0.5 * x * (1.0 + lax.erf(x * SQRT_HALF))


def _dot(a, b):
    return jnp.dot(a, b, preferred_element_type=F32)


def _dot_nt(a, b):
    return lax.dot_general(a, b, (((1,), (1,)), ((), ())), preferred_element_type=F32)


S5_TC = 128


def _in_proj_kernel(bsz, x_ref, g_ref, w_ref, u_ref, q_ref, k_ref, v_ref):
    d = x_ref.shape[-1]
    xn = _rms(x_ref[...].reshape(bsz * S5_TC, d), g_ref[...]).astype(BF16)
    p = _dot(xn, w_ref[...])
    u = p[:, :S5_WIDTH].reshape(bsz, S5_TC, S5_WIDTH)
    u_ref[...] = jnp.transpose(u, (1, 0, 2)).reshape(S5_TC * bsz, S5_WIDTH)
    q_ref[...] = p[:, S5_WIDTH:S5_WIDTH + Q_WIDTH].astype(BF16).reshape(bsz, S5_TC, Q_WIDTH)
    k_ref[...] = p[:, S5_WIDTH + Q_WIDTH:S5_WIDTH + Q_WIDTH + KV_WIDTH].astype(BF16).reshape(
        bsz, S5_TC, KV_WIDTH)
    v_ref[...] = p[:, S5_WIDTH + Q_WIDTH + KV_WIDTH:].astype(BF16).reshape(bsz, S5_TC, KV_WIDTH)


def _in_proj(x, ln1_g, w_uqkv):
    bsz, seq, d = x.shape
    return pl.pallas_call(
        functools.partial(_in_proj_kernel, bsz),
        grid=(seq // S5_TC,),
        in_specs=[
            pl.BlockSpec((bsz, S5_TC, d), lambda i: (0, i, 0)),
            pl.BlockSpec((1, d), lambda i: (0, 0)),
            pl.BlockSpec((d, UQKV_WIDTH), lambda i: (0, 0)),
        ],
        out_specs=[
            pl.BlockSpec((S5_TC * bsz, S5_WIDTH), lambda i: (i, 0)),
            pl.BlockSpec((bsz, S5_TC, Q_WIDTH), lambda i: (0, i, 0)),
            pl.BlockSpec((bsz, S5_TC, KV_WIDTH), lambda i: (0, i, 0)),
            pl.BlockSpec((bsz, S5_TC, KV_WIDTH), lambda i: (0, i, 0)),
        ],
        out_shape=[
            jax.ShapeDtypeStruct((seq * bsz, S5_WIDTH), F32),
            jax.ShapeDtypeStruct((bsz, seq, Q_WIDTH), BF16),
            jax.ShapeDtypeStruct((bsz, seq, KV_WIDTH), BF16),
            jax.ShapeDtypeStruct((bsz, seq, KV_WIDTH), BF16),
        ],
        compiler_params=pltpu.CompilerParams(
            dimension_semantics=("arbitrary",), vmem_limit_bytes=VMEM_LIMIT),
        name="in_proj",
    )(x, ln1_g, w_uqkv)


S5_LC = 512
S5_CH = S5_LC // S5_STATE * S5_GROUP


def _s5_kernel(bsz, u_ref, a_ref, bblk_ref, cblk_ref, d_ref, wglu_ref, wbr_ref, o_ref,
               h_ref, bu_ref):
    @pl.when(pl.program_id(0) == 0)
    def _():
        h_ref[...] = jnp.zeros_like(h_ref)

    u = u_ref[...]
    ub = u.astype(BF16)
    for lc in range(S5_LANES // S5_LC):
        ch = slice(lc * S5_CH, (lc + 1) * S5_CH)
        for part in range(2):
            cols = pl.ds(part * S5_LANES + lc * S5_LC, S5_LC)
            bu_ref[:, cols] = _dot(ub[:, ch], bblk_ref[lc, :, part * S5_LC:(part + 1) * S5_LC])

    for lc in range(S5_LANES // S5_LC):
        re = pl.ds(lc * S5_LC, S5_LC)
        im = pl.ds(S5_LANES + lc * S5_LC, S5_LC)
        ar = a_ref[0, :, re]
        ai = a_ref[1, :, re]

        hr, hi = h_ref[0, :, re], h_ref[1, :, re]
        for t in range(S5_TC):
            rows = pl.ds(t * bsz, bsz)
            nr = ar * hr - ai * hi + bu_ref[rows, re]
            ni = ar * hi + ai * hr + bu_ref[rows, im]
            bu_ref[rows, re] = nr
            bu_ref[rows, im] = ni
            hr, hi = nr, ni
        h_ref[0, :, re] = hr
        h_ref[1, :, re] = hi

    ys = []
    for lc in range(S5_LANES // S5_LC):
        ch = slice(lc * S5_CH, (lc + 1) * S5_CH)
        re = pl.ds(lc * S5_LC, S5_LC)
        im = pl.ds(S5_LANES + lc * S5_LC, S5_LC)
        ys.append(_dot(bu_ref[:, re].astype(BF16), cblk_ref[lc, :S5_LC, :])
                  + _dot(bu_ref[:, im].astype(BF16), cblk_ref[lc, S5_LC:, :]))
    y = jnp.concatenate(ys, axis=1) + d_ref[...] * u
    yg = _gelu(y).astype(BF16)
    ab = _dot(yg, wglu_ref[...])
    glu = ab[:, :S5_WIDTH] * jax.nn.sigmoid(ab[:, S5_WIDTH:])
    glu = jnp.transpose(glu.reshape(S5_TC, bsz, S5_WIDTH), (1, 0, 2)).reshape(bsz * S5_TC, S5_WIDTH)
    o_ref[...] = _dot(glu.astype(BF16), wbr_ref[...]).astype(BF16).reshape(bsz, S5_TC, D_MODEL)


def _s5(u_tm, a8, bblk, cblk, d_row, w_glu, w_br, bsz):
    rows = u_tm.shape[0]
    tr = S5_TC * bsz
    const = lambda shape: pl.BlockSpec(shape, lambda i: (0,) * len(shape))
    return pl.pallas_call(
        functools.partial(_s5_kernel, bsz),
        grid=(rows // tr,),
        in_specs=[
            pl.BlockSpec((tr, S5_WIDTH), lambda i: (i, 0)),
            const((2, bsz, S5_LANES)),
            const((S5_LANES // S5_LC, S5_CH, 2 * S5_LC)),
            const((S5_LANES // S5_LC, 2 * S5_LC, S5_CH)),
            const((1, S5_WIDTH)),
            const((S5_WIDTH, 2 * S5_WIDTH)),
            const((S5_WIDTH, D_MODEL)),
        ],
        out_specs=pl.BlockSpec((bsz, S5_TC, D_MODEL), lambda i: (0, i, 0)),
        out_shape=jax.ShapeDtypeStruct((bsz, rows // bsz, D_MODEL), BF16),
        scratch_shapes=[
            pltpu.VMEM((2, bsz, S5_LANES), F32),
            pltpu.VMEM((tr, 2 * S5_LANES), F32),
        ],
        compiler_params=pltpu.CompilerParams(
            dimension_semantics=("arbitrary",), vmem_limit_bytes=VMEM_LIMIT),
        name="s5",
    )(u_tm, a8, bblk, cblk, d_row, w_glu, w_br)


def _attn_kernel(q_ref, kc_ref, kp_ref, vc_ref, vp_ref, qg_ref, kg_ref, bucket_ref,
                 table_ref, sink_ref, o_ref, bias_ref):
    first = jnp.logical_and(pl.program_id(0) == 0, pl.program_id(1) == 0)

    @pl.when(first)
    def _():
        bucket = bucket_ref[...]
        for h in range(N_Q_HEADS):
            acc = jnp.zeros((ATTN_BLOCK, 2 * ATTN_BLOCK), F32)
            for b in range(N_BUCKETS):
                acc = jnp.where(bucket == b, table_ref[b, h], acc)
            bias_ref[h] = acc

    blk = pl.program_id(1)
    qi = lax.broadcasted_iota(jnp.int32, (ATTN_BLOCK, 2 * ATTN_BLOCK), 0)
    si = lax.broadcasted_iota(jnp.int32, (ATTN_BLOCK, 2 * ATTN_BLOCK), 1)
    dist = ATTN_BLOCK + qi - si
    band = (dist >= 0) & (dist < ATTN_BLOCK)
    valids = [band & ((si >= ATTN_BLOCK) | (blk > 0))] + [band] * (ATTN_QB - 1)
    qg = qg_ref[...]
    kg = kg_ref[...]
    scale = HEAD_DIM ** -0.5

    grp_rows = Q_PER_KV * ATTN_BLOCK
    row_head = lax.broadcasted_iota(jnp.int32, (grp_rows, 1), 0) // ATTN_BLOCK

    ones_d = jnp.ones((HEAD_DIM, LANES), BF16)
    ones_s = jnp.ones((2 * ATTN_BLOCK, LANES), BF16)

    def rms_rows(x, g):
        ssq = _dot((x * x).astype(BF16), ones_d)[:, :HEAD_DIM]
        return (x * lax.rsqrt(ssq * (1.0 / HEAD_DIM) + RMS_EPS)) * g

    groups = [(sb, kh) for sb in range(ATTN_QB) for kh in range(N_KV_HEADS)]
    knorm, vall = [], []
    for kh in range(N_KV_HEADS):
        cols = slice(kh * HEAD_DIM, (kh + 1) * HEAD_DIM)
        kall = jnp.concatenate([kp_ref[0, :, cols], kc_ref[0, :, cols]], axis=0).astype(F32)
        knorm.append(rms_rows(kall, kg).astype(BF16))
        vall.append(jnp.concatenate([vp_ref[0, :, cols], vc_ref[0, :, cols]], axis=0))
    qss = []
    for sb, kh in groups:
        h0 = kh * Q_PER_KV
        rows = slice(sb * ATTN_BLOCK, (sb + 1) * ATTN_BLOCK)
        qs = jnp.concatenate(
            [q_ref[0, rows, (h0 + g) * HEAD_DIM:(h0 + g + 1) * HEAD_DIM] for g in range(Q_PER_KV)],
            axis=0).astype(F32)
        qss.append((rms_rows(qs, qg) * scale).astype(BF16))
    ss, sinks = [], []
    for gi, (sb, kh) in enumerate(groups):
        h0 = kh * Q_PER_KV
        keys = slice(sb * ATTN_BLOCK, (sb + 2) * ATTN_BLOCK)
        s = _dot_nt(qss[gi], knorm[kh][keys]).reshape(Q_PER_KV, ATTN_BLOCK, 2 * ATTN_BLOCK)
        ss.append(jnp.where(valids[sb][None], s + bias_ref[h0:h0 + Q_PER_KV], NEG_INF).reshape(
            grp_rows, 2 * ATTN_BLOCK))
        sink = jnp.full((grp_rows, 1), sink_ref[h0], F32)
        for g in range(1, Q_PER_KV):
            sink = jnp.where(row_head == g, sink_ref[h0 + g], sink)
        sinks.append(sink)
    ps, dens = [], []
    for gi in range(len(groups)):
        m = jnp.maximum(jnp.max(ss[gi], axis=-1, keepdims=True), sinks[gi])
        p = jnp.exp(ss[gi] - m).astype(BF16)
        ps.append(p)
        dens.append(_dot(p, ones_s)[:, :HEAD_DIM] + jnp.exp(sinks[gi] - m))
    for gi, (sb, kh) in enumerate(groups):
        h0 = kh * Q_PER_KV
        keys = slice(sb * ATTN_BLOCK, (sb + 2) * ATTN_BLOCK)
        o = (_dot(ps[gi], vall[kh][keys]) / dens[gi]).astype(BF16)
        for g in range(Q_PER_KV):
            o_ref[0, sb * ATTN_BLOCK:(sb + 1) * ATTN_BLOCK, (h0 + g) * HEAD_DIM:(h0 + g + 1) * HEAD_DIM] = (
                o[g * ATTN_BLOCK:(g + 1) * ATTN_BLOCK])


ATTN_QB = 2


def _attention(q, k, v, qg, kg, bucket, table, sinks):
    bsz, seq, _ = q.shape
    rows = ATTN_QB * ATTN_BLOCK
    cur = lambda b, i: (b, i, 0)
    prev = lambda b, i: (b, jnp.maximum(i * ATTN_QB - 1, 0), 0)
    const2 = lambda b, i: (0, 0)
    return pl.pallas_call(
        _attn_kernel,
        grid=(bsz, seq // rows),
        in_specs=[
            pl.BlockSpec((1, rows, Q_WIDTH), cur),
            pl.BlockSpec((1, rows, KV_WIDTH), cur),
            pl.BlockSpec((1, ATTN_BLOCK, KV_WIDTH), prev),
            pl.BlockSpec((1, rows, KV_WIDTH), cur),
            pl.BlockSpec((1, ATTN_BLOCK, KV_WIDTH), prev),
            pl.BlockSpec((1, HEAD_DIM), const2),
            pl.BlockSpec((1, HEAD_DIM), const2),
            pl.BlockSpec((ATTN_BLOCK, 2 * ATTN_BLOCK), const2),
            pl.BlockSpec(memory_space=pltpu.SMEM),
            pl.BlockSpec(memory_space=pltpu.SMEM),
        ],
        out_specs=pl.BlockSpec((1, rows, Q_WIDTH), cur),
        out_shape=jax.ShapeDtypeStruct((bsz, seq, Q_WIDTH), BF16),
        scratch_shapes=[pltpu.VMEM((N_Q_HEADS, ATTN_BLOCK, 2 * ATTN_BLOCK), F32)],
        compiler_params=pltpu.CompilerParams(
            dimension_semantics=("arbitrary", "arbitrary"), vmem_limit_bytes=VMEM_LIMIT),
        name="attn",
    )(q, k, k, v, v, qg, kg, bucket, table, sinks)


def _t5_bucket_matrix():
    qi = np.arange(ATTN_BLOCK)[:, None]
    si = np.arange(2 * ATTN_BLOCK)[None, :]
    dist = np.maximum(ATTN_BLOCK + qi - si, 0)
    max_exact = N_BUCKETS // 2
    d_f = np.maximum(dist, 1).astype(np.float64)
    large = max_exact + np.floor(np.log(d_f / max_exact) / math.log(MAX_DISTANCE / max_exact)
                                 * (N_BUCKETS - max_exact)).astype(np.int32)
    large = np.minimum(large, N_BUCKETS - 1)
    return jnp.asarray(np.where(dist < max_exact, dist, large).astype(np.int32))


MERGE_TM = 512


def _merge_kernel(x_ref, attn_ref, ys5_ref, g1_ref, wg_ref, bg_ref, wab_ref, wout_ref,
                  g2_ref, wpq_ref, h1_ref, xn2_ref, qp_ref):
    x = x_ref[0]
    xn = _rms(x, g1_ref[...]).astype(BF16)
    gates = jax.nn.sigmoid(_dot(xn, wg_ref[...]) + bg_ref[...])
    y_attn = _dot(attn_ref[0], wab_ref[...])
    y_s5 = ys5_ref[0].astype(F32)
    mix = gates[:, :D_MODEL] * y_s5 + gates[:, D_MODEL:] * y_attn
    h1 = x + _dot(mix.astype(BF16), wout_ref[...])
    h1_ref[...] = h1
    xn2 = _rms(h1, g2_ref[...]).astype(BF16)
    xn2_ref[...] = xn2
    qp_ref[...] = _dot(xn2, wpq_ref[...]).astype(BF16)


def _merge(x, attn, ys5_tm, ln1_g, w_gate, b_gate, w_ab, w_out, ln2_g, w_pq):
    bsz, seq, d = x.shape
    nt = seq // MERGE_TM
    tok = lambda b, i: (b * nt + i, 0)
    const2 = lambda b, i: (0, 0)
    return pl.pallas_call(
        _merge_kernel,
        grid=(bsz, nt),
        in_specs=[
            pl.BlockSpec((1, MERGE_TM, d), lambda b, i: (b, i, 0)),
            pl.BlockSpec((1, MERGE_TM, d), lambda b, i: (b, i, 0)),
            pl.BlockSpec((1, MERGE_TM, d), lambda b, i: (b, i, 0)),
            pl.BlockSpec((1, d), const2),
            pl.BlockSpec((d, 2 * d), const2),
            pl.BlockSpec((1, 2 * d), const2),
            pl.BlockSpec((d, d), const2),
            pl.BlockSpec((d, d), const2),
            pl.BlockSpec((1, d), const2),
            pl.BlockSpec((d, d), const2),
        ],
        out_specs=[
            pl.BlockSpec((MERGE_TM, d), tok),
            pl.BlockSpec((MERGE_TM, d), tok),
            pl.BlockSpec((MERGE_TM, d), tok),
        ],
        out_shape=[
            jax.ShapeDtypeStruct((bsz * seq, d), F32),
            jax.ShapeDtypeStruct((bsz * seq, d), BF16),
            jax.ShapeDtypeStruct((bsz * seq, d), BF16),
        ],
        compiler_params=pltpu.CompilerParams(
            dimension_semantics=("arbitrary", "arbitrary"), vmem_limit_bytes=VMEM_LIMIT),
        name="merge",
    )(x, attn, ys5_tm, ln1_g, w_gate, b_gate, w_ab, w_out, ln2_g, w_pq)


TOPK_TK = 512
LANES = 128
SUBLANES = 8
CAND_ROWS = 80
POS_INVALID = 1.0e9


def _cand_layout():
    pos = np.full((CAND_ROWS,), POS_INVALID, np.float32)
    blocks = [(0, 0), (0, 8), (1, 0), (2, 0), (3, 0), (4, 0), (5, 0), (6, 0), (7, 0)]
    for r, (a, b0) in enumerate(blocks):
        for j in range(8):
            b = b0 + j
            if (a + 1) * (b + 1) <= PEER_TOPK:
                pos[r * 8 + j] = a * PEER_TOPK + b
    for j in range(8):
        pos[72 + j] = (8 + j) * PEER_TOPK
    return blocks, np.broadcast_to(pos[:, None], (CAND_ROWS, LANES)).copy()


_CAND_BLOCKS, _CAND_POS = _cand_layout()
_KEY_IDS = np.broadcast_to(np.arange(N_SUB_KEYS, dtype=np.float32)[:, None], (N_SUB_KEYS, LANES)).copy()


def _extract_round(s, ids, big):
    m = jnp.max(s, axis=0, keepdims=True)
    sel = jnp.min(jnp.where(s == m, ids, big), axis=0, keepdims=True)
    return m, sel, ids == sel


def _topk_stages(sk_ref, chunks, ids_ref, pos_ref, scratch):
    v1_ref, i1_ref, v2_ref, i2_ref, best_ref, s_ref, cand_ref, expert_ref, picked_ref = scratch
    nkeys = float(N_SUB_KEYS)
    tile = slice(0, SUBLANES)
    for qp, lanes, _ in chunks:
        s_ref[0, :, lanes] = _dot_nt(sk_ref[0, 0], qp[:, :PEER_HALF])
        s_ref[1, :, lanes] = _dot_nt(sk_ref[0, 1], qp[:, PEER_HALF:])
    live = [(s_ref, (half, tile, lanes)) for _, lanes, _ in chunks for half in range(2)]
    yield live
    for j in range(PEER_TOPK):
        for _, lanes, _ in chunks:
            for half, (val_ref, id_ref) in enumerate(((v1_ref, i1_ref), (v2_ref, i2_ref))):
                s = s_ref[half, :, lanes]
                m, sel, hit = _extract_round(s, ids_ref[...], nkeys)
                val_ref[j:j + 1, lanes] = m
                id_ref[j:j + 1, lanes] = sel
                s_ref[half, :, lanes] = jnp.where(hit, -jnp.inf, s)
        yield live

    for _, lanes, _ in chunks:
        for r, (a, b0) in enumerate(_CAND_BLOCKS):
            rows = slice(r * SUBLANES, (r + 1) * SUBLANES)
            cand_ref[rows, lanes] = v1_ref[a:a + 1, lanes] + v2_ref[b0:b0 + 8, lanes]
            expert_ref[rows, lanes] = i1_ref[a:a + 1, lanes] * nkeys + i2_ref[b0:b0 + 8, lanes]
        rows = slice(CAND_ROWS - SUBLANES, CAND_ROWS)
        cand_ref[rows, lanes] = v1_ref[8:16, lanes] + v2_ref[0:1, lanes]
        expert_ref[rows, lanes] = i1_ref[8:16, lanes] * nkeys + i2_ref[0:1, lanes]
        cand_ref[:, lanes] = jnp.where(pos_ref[...] < POS_INVALID, cand_ref[:, lanes], -jnp.inf)
    live = [(cand_ref, (tile, lanes)) for _, lanes, _ in chunks]
    yield live
    for j in range(PEER_TOPK):
        for _, lanes, _ in chunks:
            cand = cand_ref[:, lanes]
            m, _, hit = _extract_round(cand, pos_ref[...], POS_INVALID)
            best_ref[j:j + 1, lanes] = m
            picked_ref[j:j + 1, lanes] = jnp.sum(jnp.where(hit, expert_ref[:, lanes], 0.0), axis=0,
                                                 keepdims=True)
            cand_ref[:, lanes] = jnp.where(hit, -jnp.inf, cand)
        if j % 2 == 1:
            yield live
    for _, lanes, store in chunks:
        best = best_ref[:, lanes]
        p = jnp.exp(best - best_ref[0:1, lanes])
        store(picked_ref[:, lanes].astype(jnp.int32), p / jnp.sum(p, axis=0, keepdims=True))
    yield []


def _order_after(live, value_tile):
    zero = jnp.minimum(jnp.abs(value_tile), 0.0)
    for ref, idx in live:
        ref[idx] = ref[idx] + zero


def _topk_scratch(n_chunks):
    w = n_chunks * LANES
    return ([pltpu.VMEM((PEER_TOPK, w), F32) for _ in range(5)]
            + [pltpu.VMEM((2, N_SUB_KEYS, w), F32), pltpu.VMEM((CAND_ROWS, w), F32),
               pltpu.VMEM((CAND_ROWS, w), F32), pltpu.VMEM((PEER_TOPK, w), F32)])


def _topk_kernel(qp_ref, sk_ref, ids_ref, pos_ref, e_ref, g_ref, *scratch):
    def store(c):
        def fn(e, g):
            e_ref[c] = e
            g_ref[c] = g
        return fn

    chunks = [(qp_ref[pl.ds(c * LANES, LANES), :], pl.ds(c * LANES, LANES), store(c))
              for c in range(TOPK_TK // LANES)]
    for _ in _topk_stages(sk_ref, chunks, ids_ref, pos_ref, scratch):
        pass


def _topk(qp, sub_keys, ids, pos):
    n_lc = TOPK_TK // LANES
    return pl.pallas_call(
        _topk_kernel,
        grid=(1, PEER_HEADS),
        in_specs=[
            pl.BlockSpec((TOPK_TK, 2 * PEER_HALF), lambda i, h: (i, h)),
            pl.BlockSpec((1, 2, N_SUB_KEYS, PEER_HALF), lambda i, h: (h, 0, 0, 0)),
            pl.BlockSpec((N_SUB_KEYS, LANES), lambda i, h: (0, 0)),
            pl.BlockSpec((CAND_ROWS, LANES), lambda i, h: (0, 0)),
        ],
        out_specs=[
            pl.BlockSpec((n_lc, PEER_TOPK, LANES), lambda i, h: (0, h, 0)),
            pl.BlockSpec((n_lc, PEER_TOPK, LANES), lambda i, h: (0, h, 0)),
        ],
        out_shape=[
            jax.ShapeDtypeStruct((n_lc, N_PICKS, LANES), jnp.int32),
            jax.ShapeDtypeStruct((n_lc, N_PICKS, LANES), F32),
        ],
        scratch_shapes=_topk_scratch(n_lc),
        compiler_params=pltpu.CompilerParams(
            dimension_semantics=("arbitrary", "arbitrary"), vmem_limit_bytes=VMEM_LIMIT),
        name="topk",
    )(qp, sub_keys, ids, pos)


PEER_TM = 512
PEER_CHUNK = 2048
PEER_SLABS = PEER_CHUNK // N_SUB_KEYS
PEER_NCHUNK = (N_SUB_KEYS * N_SUB_KEYS) // PEER_CHUNK
PEER_TG = 16


PEER_SUB = 4
PEER_SELECT_LEAD = 3
PEER_SELECT_PER_STAGE = 3
PEER_LC = PEER_TM // LANES
PEER_LC_STEP = PEER_LC * PEER_HEADS // PEER_NCHUNK
assert PEER_NCHUNK * PEER_LC_STEP == PEER_LC * PEER_HEADS and PEER_TM == TOPK_TK


def _peer_kernel(xn_ref, u_ref, v_ref, e0_ref, g0_ref, h1_ref, qpn_ref, sk_ref, ids_ref, pos_ref,
                 o_ref, hg_ref, et_ref, gt_ref, wa_ref, wb_ref, ebuf_ref, gbuf_ref, *topk_scratch):
    i = pl.program_id(0)
    j = pl.program_id(1)
    cur = i % 2

    @pl.when(jnp.logical_and(i == 0, j == 0))
    def _():
        ebuf_ref[0] = e0_ref[...]
        gbuf_ref[0] = g0_ref[...]

    @pl.when(j == 0)
    def _():
        for c in range(PEER_LC):
            et_ref[c * LANES:(c + 1) * LANES, :] = ebuf_ref[cur, c].T
            gt_ref[c * LANES:(c + 1) * LANES, :] = gbuf_ref[cur, c].T
        sub = lax.broadcasted_iota(jnp.int32, (N_SUB_KEYS, N_PICKS), 0)

        def scatter(gi, w_ref):
            t0 = gi * PEER_TG if isinstance(gi, int) else pl.multiple_of(gi * PEER_TG, PEER_TG)
            for tt in range(PEER_TG):
                e_row = et_ref[pl.ds(t0 + tt, 1), :]
                g_row = gt_ref[pl.ds(t0 + tt, 1), :]
                first = jnp.where(sub == (e_row >> 7), g_row, 0.0).astype(BF16)
                second = jnp.where(sub == (e_row & (N_SUB_KEYS - 1)), 1.0, 0.0).astype(BF16)
                w = _dot_nt(first, second)
                for v in range(N_SUB_KEYS // SUBLANES):
                    w_ref[v, pl.ds(tt * SUBLANES, SUBLANES), :] = w[v * SUBLANES:(v + 1) * SUBLANES, :]

        def relayout(gi, w_ref):
            t0 = gi * PEER_TG if isinstance(gi, int) else pl.multiple_of(gi * PEER_TG, PEER_TG)
            for i1 in range(N_SUB_KEYS):
                v, r = divmod(i1, SUBLANES)
                rows = w_ref[v, pl.ds(r, PEER_TG, stride=SUBLANES), :]
                c, s = divmod(i1, PEER_SLABS)
                hg_ref[c, pl.ds(t0, PEER_TG), s * N_SUB_KEYS:(s + 1) * N_SUB_KEYS] = rows.astype(BF16)

        def pair(k, carry):
            scatter(2 * k + 1, wb_ref)
            relayout(2 * k, wa_ref)
            scatter(2 * k + 2, wa_ref)
            relayout(2 * k + 1, wb_ref)
            return carry

        n_groups = PEER_TM // PEER_TG
        scatter(0, wa_ref)
        lax.fori_loop(0, n_groups // 2 - 1, pair, 0)
        scatter(n_groups - 1, wb_ref)
        relayout(n_groups - 2, wa_ref)
        relayout(n_groups - 1, wb_ref)
        o_ref[...] = h1_ref[...]

    def expert_stages():
        xn = xn_ref[...]
        sub = PEER_CHUNK // PEER_SUB
        hs = []
        for q in range(PEER_SUB):
            es = slice(q * sub, (q + 1) * sub)
            a = _dot_nt(xn, u_ref[es, :])
            hq = _gelu(a) * hg_ref[j, :, es].astype(F32)
            hs.append(hq.astype(BF16))
            yield hq[:SUBLANES, :LANES] + hq[-SUBLANES:, -LANES:]
        for q in range(PEER_SUB):
            es = slice(q * sub, (q + 1) * sub)
            r = _dot(hs[q], v_ref[es, :])
            o_ref[...] += r
            yield r[:SUBLANES, :LANES]

    head_rows = pl.ds(pl.multiple_of((j * PEER_LC_STEP // PEER_LC) * PEER_TOPK, PEER_TOPK), PEER_TOPK)

    def store(lc):
        def fn(e, g):
            ebuf_ref[1 - cur, lc, head_rows, :] = e
            gbuf_ref[1 - cur, lc, head_rows, :] = g
        return fn

    chunks = []
    for c in range(PEER_LC_STEP):
        lc = (j * PEER_LC_STEP) % PEER_LC + c
        qp = qpn_ref[pl.ds(pl.multiple_of(lc * LANES, LANES), LANES), :]
        chunks.append((qp, pl.ds(c * LANES, LANES), store(lc)))
    select = _topk_stages(sk_ref, chunks, ids_ref, pos_ref, topk_scratch)
    experts = expert_stages()
    live = []
    for _ in range(PEER_SELECT_LEAD):
        live = next(select, [])
    for result_tile in experts:
        _order_after(live, result_tile)
        for _ in range(PEER_SELECT_PER_STAGE):
            live = next(select, [])
    for _ in select:
        pass


def _peer(xn2, u_tab, v_tab, e0, g0, h1, qp, sub_keys, ids, pos):
    t, d = xn2.shape
    nblk = t // PEER_TM
    steps_per_head = PEER_LC // PEER_LC_STEP
    const3 = lambda i, j: (0, 0, 0)
    return pl.pallas_call(
        _peer_kernel,
        grid=(nblk, PEER_NCHUNK),
        in_specs=[
            pl.BlockSpec((PEER_TM, d), lambda i, j: (i, 0)),
            pl.BlockSpec((PEER_CHUNK, d), lambda i, j: (j, 0)),
            pl.BlockSpec((PEER_CHUNK, d), lambda i, j: (j, 0)),
            pl.BlockSpec((PEER_LC, N_PICKS, LANES), const3),
            pl.BlockSpec((PEER_LC, N_PICKS, LANES), const3),
            pl.BlockSpec((PEER_TM, d), lambda i, j: (i, 0)),
            pl.BlockSpec((PEER_TM, 2 * PEER_HALF),
                         lambda i, j: (jnp.minimum(i + 1, nblk - 1), j // steps_per_head)),
            pl.BlockSpec((1, 2, N_SUB_KEYS, PEER_HALF), lambda i, j: (j // steps_per_head, 0, 0, 0)),
            pl.BlockSpec((N_SUB_KEYS, LANES), lambda i, j: (0, 0)),
            pl.BlockSpec((CAND_ROWS, LANES), lambda i, j: (0, 0)),
        ],
        out_specs=pl.BlockSpec((PEER_TM, d), lambda i, j: (i, 0)),
        out_shape=jax.ShapeDtypeStruct((t, d), F32),
        scratch_shapes=[
            pltpu.VMEM((PEER_NCHUNK, PEER_TM, PEER_CHUNK), BF16),
            pltpu.VMEM((PEER_TM, N_PICKS), jnp.int32),
            pltpu.VMEM((PEER_TM, N_PICKS), F32),
            pltpu.VMEM((N_SUB_KEYS // SUBLANES, PEER_TG * SUBLANES, N_SUB_KEYS), F32),
            pltpu.VMEM((N_SUB_KEYS // SUBLANES, PEER_TG * SUBLANES, N_SUB_KEYS), F32),
            pltpu.VMEM((2, PEER_LC, N_PICKS, LANES), jnp.int32),
            pltpu.VMEM((2, PEER_LC, N_PICKS, LANES), F32),
        ] + _topk_scratch(PEER_LC_STEP),
        compiler_params=pltpu.CompilerParams(
            dimension_semantics=("arbitrary", "arbitrary"), vmem_limit_bytes=VMEM_LIMIT),
        name="peer",
    )(xn2, u_tab, v_tab, e0, g0, h1, qp, sub_keys, ids, pos)


def _s5_discretise_kernel(lr_ref, li_ref, ls_ref, br_ref, bi_ref, ar_ref, ai_ref, bbr_ref, bbi_ref):
    lr, li = lr_ref[...], li_ref[...]
    step = jnp.exp(ls_ref[...])
    mag = jnp.exp(lr * step)
    abar_re = mag * jnp.cos(li * step)
    abar_im = mag * jnp.sin(li * step)
    den = lr * lr + li * li
    nr, ni = abar_re - 1.0, abar_im
    coef_re = (nr * lr + ni * li) / den
    coef_im = (ni * lr - nr * li) / den
    ar_ref[...] = abar_re
    ai_ref[...] = abar_im
    br, bi = br_ref[...], bi_ref[...]
    bbr_ref[...] = coef_re * br - coef_im * bi
    bbi_ref[...] = coef_re * bi + coef_im * br


def _s5_params(lam_re, lam_im, log_step, b_re, b_im, c_re, c_im, bsz):
    col = lambda m: m.reshape(S5_LANES, 1)
    pole = jax.ShapeDtypeStruct((S5_LANES, 1), F32)
    pole_in = jax.ShapeDtypeStruct((S5_LANES, S5_GROUP), F32)
    abar_re, abar_im, bbar_re, bbar_im = pl.pallas_call(
        _s5_discretise_kernel, out_shape=[pole, pole, pole_in, pole_in], name="s5_discretise",
    )(col(lam_re), col(lam_im), col(jnp.repeat(log_step, S5_STATE)),
      b_re.reshape(S5_LANES, S5_GROUP), b_im.reshape(S5_LANES, S5_GROUP))
    nlc = S5_LANES // S5_LC
    gpc = S5_GROUPS // nlc
    eye = jnp.eye(gpc, dtype=F32)
    to_b = lambda m: jnp.einsum('lgpc,gh->lgchp', m.reshape(nlc, gpc, S5_STATE, S5_GROUP),
                                eye).reshape(nlc, S5_CH, S5_LC)
    to_c = lambda m: jnp.einsum('lgcp,gh->lgphc', m.reshape(nlc, gpc, S5_GROUP, S5_STATE),
                                eye).reshape(nlc, S5_LC, S5_CH)
    bblk = jnp.concatenate([to_b(bbar_re), to_b(bbar_im)], axis=2).astype(BF16)
    cblk = jnp.concatenate([to_c(c_re), -to_c(c_im)], axis=1).astype(BF16)
    a = jnp.stack([abar_re.reshape(-1), abar_im.reshape(-1)])
    a8 = jnp.broadcast_to(a[:, None, :], (2, bsz, S5_LANES))
    return a8, bblk, cblk


def kernel(x, ln1_g, w_in, b_gate, s5_lambda_re, s5_lambda_im, s5_log_step, s5_b_re, s5_b_im, s5_c_re, s5_c_im, s5_d, s5_w_glu, w_s5_branch, q_norm_g, k_norm_g, rel_bias_table, attn_sinks, w_attn_branch, w_out, ln2_g, peer_w_query, peer_sub_keys, peer_u, peer_v):
    bsz, seq, d = x.shape
    h = x
    for layer in range(ln1_g.shape[0]):
        w_uqkv = w_in[layer, :, :UQKV_WIDTH].astype(BF16)
        w_gate = w_in[layer, :, UQKV_WIDTH:].astype(BF16)
        u_tm, q, k, v = _in_proj(h, ln1_g[layer][None], w_uqkv)

        a8, bblk, cblk = _s5_params(s5_lambda_re[layer], s5_lambda_im[layer], s5_log_step[layer],
                                    s5_b_re[layer], s5_b_im[layer], s5_c_re[layer], s5_c_im[layer], bsz)
        ys5 = _s5(u_tm, a8, bblk, cblk, s5_d[layer].reshape(1, S5_WIDTH),
                  s5_w_glu[layer].astype(BF16), w_s5_branch[layer].astype(BF16), bsz)

        attn = _attention(q, k, v, q_norm_g[layer][None], k_norm_g[layer][None],
                          _t5_bucket_matrix(), rel_bias_table, attn_sinks[layer])

        h1, xn2, qp = _merge(h, attn, ys5, ln1_g[layer][None], w_gate,
                             b_gate[layer][None], w_attn_branch[layer].astype(BF16),
                             w_out[layer].astype(BF16), ln2_g[layer][None],
                             peer_w_query[layer].astype(BF16))

        sub_keys = peer_sub_keys[layer].astype(BF16)
        pos = jnp.asarray(_CAND_POS)
        ids = jnp.asarray(_KEY_IDS)
        e0, g0 = _topk(qp, sub_keys, ids, pos)
        out = _peer(xn2, peer_u[layer].astype(BF16), peer_v[layer].astype(BF16), e0, g0, h1,
                    qp, sub_keys, ids, pos)
        h = out.reshape(bsz, seq, d)
    return h
```

```python
import functools
import math

import jax
import jax.numpy as jnp
import numpy as np
from jax import lax
from jax.experimental import pallas as pl
from jax.experimental.pallas import tpu as pltpu

F32 = jnp.float32
BF16 = jnp.bfloat16

D_MODEL = 1024
RMS_EPS = 1e-6
NEG_INF = -1e30

S5_WIDTH = 512
S5_GROUP = 16
S5_GROUPS = 32
S5_STATE = 64
S5_LANES = S5_GROUPS * S5_STATE

N_Q_HEADS = 16
N_KV_HEADS = 4
HEAD_DIM = 64
Q_PER_KV = 4
ATTN_BLOCK = 128
N_BUCKETS = 32
MAX_DISTANCE = 128
Q_WIDTH = 1024
KV_WIDTH = 256
UQKV_WIDTH = S5_WIDTH + Q_WIDTH + 2 * KV_WIDTH

PEER_HEADS = 8
PEER_HALF = 64
N_SUB_KEYS = 128
PEER_TOPK = 16
N_PICKS = PEER_HEADS * PEER_TOPK

VMEM_LIMIT = 56 * 1024 * 1024

SQRT_HALF = 0.7071067811865476


def _rms(x, g):
    r = lax.rsqrt(jnp.mean(x * x, axis=-1, keepdims=True) + RMS_EPS)
    return (x * r) * g


def _gelu(x):
    return 0.5 * x * (1.0 + lax.erf(x * SQRT_HALF))


def _dot(a, b):
    return jnp.dot(a, b, preferred_element_type=F32)


def _dot_nt(a, b):
    return lax.dot_general(a, b, (((1,), (1,)), ((), ())), preferred_element_type=F32)


S5_TC = 128


def _in_proj_kernel(bsz, x_ref, g_ref, w_ref, u_ref, q_ref, k_ref, v_ref):
    d = x_ref.shape[-1]
    xn = _rms(x_ref[...].reshape(bsz * S5_TC, d), g_ref[...]).astype(BF16)
    p = _dot(xn, w_ref[...])
    u = p[:, :S5_WIDTH].reshape(bsz, S5_TC, S5_WIDTH)
    u_ref[...] = jnp.transpose(u, (1, 0, 2)).reshape(S5_TC * bsz, S5_WIDTH)
    q_ref[...] = p[:, S5_WIDTH:S5_WIDTH + Q_WIDTH].astype(BF16).reshape(bsz, S5_TC, Q_WIDTH)
    k_ref[...] = p[:, S5_WIDTH + Q_WIDTH:S5_WIDTH + Q_WIDTH + KV_WIDTH].astype(BF16).reshape(
        bsz, S5_TC, KV_WIDTH)
    v_ref[...] = p[:, S5_WIDTH + Q_WIDTH + KV_WIDTH:].astype(BF16).reshape(bsz, S5_TC, KV_WIDTH)


def _in_proj(x, ln1_g, w_uqkv):
    bsz, seq, d = x.shape
    return pl.pallas_call(
        functools.partial(_in_proj_kernel, bsz),
        grid=(seq // S5_TC,),
        in_specs=[
            pl.BlockSpec((bsz, S5_TC, d), lambda i: (0, i, 0)),
            pl.BlockSpec((1, d), lambda i: (0, 0)),
            pl.BlockSpec((d, UQKV_WIDTH), lambda i: (0, 0)),
        ],
        out_specs=[
            pl.BlockSpec((S5_TC * bsz, S5_WIDTH), lambda i: (i, 0)),
            pl.BlockSpec((bsz, S5_TC, Q_WIDTH), lambda i: (0, i, 0)),
            pl.BlockSpec((bsz, S5_TC, KV_WIDTH), lambda i: (0, i, 0)),
            pl.BlockSpec((bsz, S5_TC, KV_WIDTH), lambda i: (0, i, 0)),
        ],
        out_shape=[
            jax.ShapeDtypeStruct((seq * bsz, S5_WIDTH), F32),
            jax.ShapeDtypeStruct((bsz, seq, Q_WIDTH), BF16),
            jax.ShapeDtypeStruct((bsz, seq, KV_WIDTH), BF16),
            jax.ShapeDtypeStruct((bsz, seq, KV_WIDTH), BF16),
        ],
        compiler_params=pltpu.CompilerParams(
            dimension_semantics=("arbitrary",), vmem_limit_bytes=VMEM_LIMIT),
        name="in_proj",
    )(x, ln1_g, w_uqkv)


S5_LC = 512
S5_CH = S5_LC // S5_STATE * S5_GROUP


def _s5_kernel(bsz, u_ref, a_ref, bblk_ref, cblk_ref, d_ref, wglu_ref, wbr_ref, o_ref,
               h_ref, bu_ref):
    @pl.when(pl.program_id(0) == 0)
    def _():
        h_ref[...] = jnp.zeros_like(h_ref)

    u = u_ref[...]
    ub = u.astype(BF16)
    for lc in range(S5_LANES // S5_LC):
        ch = slice(lc * S5_CH, (lc + 1) * S5_CH)
        for part in range(2):
            cols = pl.ds(part * S5_LANES + lc * S5_LC, S5_LC)
            bu_ref[:, cols] = _dot(ub[:, ch], bblk_ref[lc, :, part * S5_LC:(part + 1) * S5_LC])

    for lc in range(S5_LANES // S5_LC):
        re = pl.ds(lc * S5_LC, S5_LC)
        im = pl.ds(S5_LANES + lc * S5_LC, S5_LC)
        ar = a_ref[0, :, re]
        ai = a_ref[1, :, re]

        hr, hi = h_ref[0, :, re], h_ref[1, :, re]
        for t in range(S5_TC):
            rows = pl.ds(t * bsz, bsz)
            nr = ar * hr - ai * hi + bu_ref[rows, re]
            ni = ar * hi + ai * hr + bu_ref[rows, im]
            bu_ref[rows, re] = nr
            bu_ref[rows, im] = ni
            hr, hi = nr, ni
        h_ref[0, :, re] = hr
        h_ref[1, :, re] = hi

    ys = []
    for lc in range(S5_LANES // S5_LC):
        ch = slice(lc * S5_CH, (lc + 1) * S5_CH)
        re = pl.ds(lc * S5_LC, S5_LC)
        im = pl.ds(S5_LANES + lc * S5_LC, S5_LC)
        ys.append(_dot(bu_ref[:, re].astype(BF16), cblk_ref[lc, :S5_LC, :])
                  + _dot(bu_ref[:, im].astype(BF16), cblk_ref[lc, S5_LC:, :]))
    y = jnp.concatenate(ys, axis=1) + d_ref[...] * u
    yg = _gelu(y).astype(BF16)
    ab = _dot(yg, wglu_ref[...])
    glu = ab[:, :S5_WIDTH] * jax.nn.sigmoid(ab[:, S5_WIDTH:])
    glu = jnp.transpose(glu.reshape(S5_TC, bsz, S5_WIDTH), (1, 0, 2)).reshape(bsz * S5_TC, S5_WIDTH)
    o_ref[...] = _dot(glu.astype(BF16), wbr_ref[...]).astype(BF16).reshape(bsz, S5_TC, D_MODEL)


def _s5(u_tm, a8, bblk, cblk, d_row, w_glu, w_br, bsz):
    rows = u_tm.shape[0]
    tr = S5_TC * bsz
    const = lambda shape: pl.BlockSpec(shape, lambda i: (0,) * len(shape))
    return pl.pallas_call(
        functools.partial(_s5_kernel, bsz),
        grid=(rows // tr,),
        in_specs=[
            pl.BlockSpec((tr, S5_WIDTH), lambda i: (i, 0)),
            const((2, bsz, S5_LANES)),
            const((S5_LANES // S5_LC, S5_CH, 2 * S5_LC)),
            const((S5_LANES // S5_LC, 2 * S5_LC, S5_CH)),
            const((1, S5_WIDTH)),
            const((S5_WIDTH, 2 * S5_WIDTH)),
            const((S5_WIDTH, D_MODEL)),
        ],
        out_specs=pl.BlockSpec((bsz, S5_TC, D_MODEL), lambda i: (0, i, 0)),
        out_shape=jax.ShapeDtypeStruct((bsz, rows // bsz, D_MODEL), BF16),
        scratch_shapes=[
            pltpu.VMEM((2, bsz, S5_LANES), F32),
            pltpu.VMEM((tr, 2 * S5_LANES), F32),
        ],
        compiler_params=pltpu.CompilerParams(
            dimension_semantics=("arbitrary",), vmem_limit_bytes=VMEM_LIMIT),
        name="s5",
    )(u_tm, a8, bblk, cblk, d_row, w_glu, w_br)


def _attn_kernel(q_ref, kc_ref, kp_ref, vc_ref, vp_ref, qg_ref, kg_ref, bucket_ref,
                 table_ref, sink_ref, o_ref, bias_ref):
    first = jnp.logical_and(pl.program_id(0) == 0, pl.program_id(1) == 0)

    @pl.when(first)
    def _():
        bucket = bucket_ref[...]
        for h in range(N_Q_HEADS):
            acc = jnp.zeros((ATTN_BLOCK, 2 * ATTN_BLOCK), F32)
            for b in range(N_BUCKETS):
                acc = jnp.where(bucket == b, table_ref[b, h], acc)
            bias_ref[h] = acc

    blk = pl.program_id(1)
    qi = lax.broadcasted_iota(jnp.int32, (ATTN_BLOCK, 2 * ATTN_BLOCK), 0)
    si = lax.broadcasted_iota(jnp.int32, (ATTN_BLOCK, 2 * ATTN_BLOCK), 1)
    dist = ATTN_BLOCK + qi - si
    band = (dist >= 0) & (dist < ATTN_BLOCK)
    valids = [band & ((si >= ATTN_BLOCK) | (blk > 0))] + [band] * (ATTN_QB - 1)
    qg = qg_ref[...]
    kg = kg_ref[...]
    scale = HEAD_DIM ** -0.5

    grp_rows = Q_PER_KV * ATTN_BLOCK
    row_head = lax.broadcasted_iota(jnp.int32, (grp_rows, 1), 0) // ATTN_BLOCK

    ones_d = jnp.ones((HEAD_DIM, LANES), BF16)
    ones_s = jnp.ones((2 * ATTN_BLOCK, LANES), BF16)

    def rms_rows(x, g):
        ssq = _dot((x * x).astype(BF16), ones_d)[:, :HEAD_DIM]
        return (x * lax.rsqrt(ssq * (1.0 / HEAD_DIM) + RMS_EPS)) * g

    groups = [(sb, kh) for sb in range(ATTN_QB) for kh in range(N_KV_HEADS)]
    knorm, vall = [], []
    for kh in range(N_KV_HEADS):
        cols = slice(kh * HEAD_DIM, (kh + 1) * HEAD_DIM)
        kall = jnp.concatenate([kp_ref[0, :, cols], kc_ref[0, :, cols]], axis=0).astype(F32)
        knorm.append(rms_rows(kall, kg).astype(BF16))
        vall.append(jnp.concatenate([vp_ref[0, :, cols], vc_ref[0, :, cols]], axis=0))
    qss = []
    for sb, kh in groups:
        h0 = kh * Q_PER_KV
        rows = slice(sb * ATTN_BLOCK, (sb + 1) * ATTN_BLOCK)
        qs = jnp.concatenate(
            [q_ref[0, rows, (h0 + g) * HEAD_DIM:(h0 + g + 1) * HEAD_DIM] for g in range(Q_PER_KV)],
            axis=0).astype(F32)
        qss.append((rms_rows(qs, qg) * scale).astype(BF16))
    ss, sinks = [], []
    for gi, (sb, kh) in enumerate(groups):
        h0 = kh * Q_PER_KV
        keys = slice(sb * ATTN_BLOCK, (sb + 2) * ATTN_BLOCK)
        s = _dot_nt(qss[gi], knorm[kh][keys]).reshape(Q_PER_KV, ATTN_BLOCK, 2 * ATTN_BLOCK)
        ss.append(jnp.where(valids[sb][None], s + bias_ref[h0:h0 + Q_PER_KV], NEG_INF).reshape(
            grp_rows, 2 * ATTN_BLOCK))
        sink = jnp.full((grp_rows, 1), sink_ref[h0], F32)
        for g in range(1, Q_PER_KV):
            sink = jnp.where(row_head == g, sink_ref[h0 + g], sink)
        sinks.append(sink)
    ps, dens = [], []
    for gi in range(len(groups)):
        m = jnp.maximum(jnp.max(ss[gi], axis=-1, keepdims=True), sinks[gi])
        p = jnp.exp(ss[gi] - m).astype(BF16)
        ps.append(p)
        dens.append(_dot(p, ones_s)[:, :HEAD_DIM] + jnp.exp(sinks[gi] - m))
    for gi, (sb, kh) in enumerate(groups):
        h0 = kh * Q_PER_KV
        keys = slice(sb * ATTN_BLOCK, (sb + 2) * ATTN_BLOCK)
        o = (_dot(ps[gi], vall[kh][keys]) / dens[gi]).astype(BF16)
        for g in range(Q_PER_KV):
            o_ref[0, sb * ATTN_BLOCK:(sb + 1) * ATTN_BLOCK, (h0 + g) * HEAD_DIM:(h0 + g + 1) * HEAD_DIM] = (
                o[g * ATTN_BLOCK:(g + 1) * ATTN_BLOCK])


ATTN_QB = 2


def _attention(q, k, v, qg, kg, bucket, table, sinks):
    bsz, seq, _ = q.shape
    rows = ATTN_QB * ATTN_BLOCK
    cur = lambda b, i: (b, i, 0)
    prev = lambda b, i: (b, jnp.maximum(i * ATTN_QB - 1, 0), 0)
    const2 = lambda b, i: (0, 0)
    return pl.pallas_call(
        _attn_kernel,
        grid=(bsz, seq // rows),
        in_specs=[
            pl.BlockSpec((1, rows, Q_WIDTH), cur),
            pl.BlockSpec((1, rows, KV_WIDTH), cur),
            pl.BlockSpec((1, ATTN_BLOCK, KV_WIDTH), prev),
            pl.BlockSpec((1, rows, KV_WIDTH), cur),
            pl.BlockSpec((1, ATTN_BLOCK, KV_WIDTH), prev),
            pl.BlockSpec((1, HEAD_DIM), const2),
            pl.BlockSpec((1, HEAD_DIM), const2),
            pl.BlockSpec((ATTN_BLOCK, 2 * ATTN_BLOCK), const2),
            pl.BlockSpec(memory_space=pltpu.SMEM),
            pl.BlockSpec(memory_space=pltpu.SMEM),
        ],
        out_specs=pl.BlockSpec((1, rows, Q_WIDTH), cur),
        out_shape=jax.ShapeDtypeStruct((bsz, seq, Q_WIDTH), BF16),
        scratch_shapes=[pltpu.VMEM((N_Q_HEADS, ATTN_BLOCK, 2 * ATTN_BLOCK), F32)],
        compiler_params=pltpu.CompilerParams(
            dimension_semantics=("arbitrary", "arbitrary"), vmem_limit_bytes=VMEM_LIMIT),
        name="attn",
    )(q, k, k, v, v, qg, kg, bucket, table, sinks)


def _t5_bucket_matrix():
    qi = np.arange(ATTN_BLOCK)[:, None]
    si = np.arange(2 * ATTN_BLOCK)[None, :]
    dist = np.maximum(ATTN_BLOCK + qi - si, 0)
    max_exact = N_BUCKETS // 2
    d_f = np.maximum(dist, 1).astype(np.float64)
    large = max_exact + np.floor(np.log(d_f / max_exact) / math.log(MAX_DISTANCE / max_exact)
                                 * (N_BUCKETS - max_exact)).astype(np.int32)
    large = np.minimum(large, N_BUCKETS - 1)
    return jnp.asarray(np.where(dist < max_exact, dist, large).astype(np.int32))


MERGE_TM = 512


def _merge_kernel(x_ref, attn_ref, ys5_ref, g1_ref, wg_ref, bg_ref, wab_ref, wout_ref,
                  g2_ref, wpq_ref, u_ref, v_ref, h1_ref, xn2_ref, qp_ref, ub_ref, vb_ref):
    ub_ref[...] = u_ref[...].astype(BF16)
    vb_ref[...] = v_ref[...].astype(BF16)
    x = x_ref[0]
    xn = _rms(x, g1_ref[...]).astype(BF16)
    gates = jax.nn.sigmoid(_dot(xn, wg_ref[...]) + bg_ref[...])
    y_attn = _dot(attn_ref[0], wab_ref[...])
    y_s5 = ys5_ref[0].astype(F32)
    mix = gates[:, :D_MODEL] * y_s5 + gates[:, D_MODEL:] * y_attn
    h1 = x + _dot(mix.astype(BF16), wout_ref[...])
    h1_ref[...] = h1
    xn2 = _rms(h1, g2_ref[...]).astype(BF16)
    xn2_ref[...] = xn2
    qp_ref[...] = _dot(xn2, wpq_ref[...]).astype(BF16)


def _merge(x, attn, ys5_tm, ln1_g, w_gate, b_gate, w_ab, w_out, ln2_g, w_pq, u_tab, v_tab):
    bsz, seq, d = x.shape
    nt = seq // MERGE_TM
    tok = lambda b, i: (b * nt + i, 0)
    const2 = lambda b, i: (0, 0)
    n_experts = u_tab.shape[0]
    tab_rows = n_experts // (bsz * nt)
    assert tab_rows * bsz * nt == n_experts and v_tab.shape == u_tab.shape == (n_experts, d)
    return pl.pallas_call(
        _merge_kernel,
        grid=(bsz, nt),
        in_specs=[
            pl.BlockSpec((1, MERGE_TM, d), lambda b, i: (b, i, 0)),
            pl.BlockSpec((1, MERGE_TM, d), lambda b, i: (b, i, 0)),
            pl.BlockSpec((1, MERGE_TM, d), lambda b, i: (b, i, 0)),
            pl.BlockSpec((1, d), const2),
            pl.BlockSpec((d, 2 * d), const2),
            pl.BlockSpec((1, 2 * d), const2),
            pl.BlockSpec((d, d), const2),
            pl.BlockSpec((d, d), const2),
            pl.BlockSpec((1, d), const2),
            pl.BlockSpec((d, d), const2),
            pl.BlockSpec((tab_rows, d), tok),
            pl.BlockSpec((tab_rows, d), tok),
        ],
        out_specs=[
            pl.BlockSpec((MERGE_TM, d), tok),
            pl.BlockSpec((MERGE_TM, d), tok),
            pl.BlockSpec((MERGE_TM, d), tok),
            pl.BlockSpec((tab_rows, d), tok),
            pl.BlockSpec((tab_rows, d), tok),
        ],
        out_shape=[
            jax.ShapeDtypeStruct((bsz * seq, d), F32),
            jax.ShapeDtypeStruct((bsz * seq, d), BF16),
            jax.ShapeDtypeStruct((bsz * seq, d), BF16),
            jax.ShapeDtypeStruct((n_experts, d), BF16),
            jax.ShapeDtypeStruct((n_experts, d), BF16),
        ],
        compiler_params=pltpu.CompilerParams(
            dimension_semantics=("arbitrary", "arbitrary"), vmem_limit_bytes=VMEM_LIMIT),
        name="merge",
    )(x, attn, ys5_tm, ln1_g, w_gate, b_gate, w_ab, w_out, ln2_g, w_pq, u_tab, v_tab)


TOPK_TK = 512
LANES = 128
SUBLANES = 8
CAND_ROWS = 80
POS_INVALID = 1.0e9


def _cand_layout():
    pos = np.full((CAND_ROWS,), POS_INVALID, np.float32)
    blocks = [(0, 0), (0, 8), (1, 0), (2, 0), (3, 0), (4, 0), (5, 0), (6, 0), (7, 0)]
    for r, (a, b0) in enumerate(blocks):
        for j in range(8):
            b = b0 + j
            if (a + 1) * (b + 1) <= PEER_TOPK:
                pos[r * 8 + j] = a * PEER_TOPK + b
    for j in range(8):
        pos[72 + j] = (8 + j) * PEER_TOPK
    return blocks, np.broadcast_to(pos[:, None], (CAND_ROWS, LANES)).copy()


_CAND_BLOCKS, _CAND_POS = _cand_layout()
_KEY_IDS = np.broadcast_to(np.arange(N_SUB_KEYS, dtype=np.float32)[:, None], (N_SUB_KEYS, LANES)).copy()


def _extract_round(s, ids, big):
    m = jnp.max(s, axis=0, keepdims=True)
    sel = jnp.min(jnp.where(s == m, ids, big), axis=0, keepdims=True)
    return m, sel, ids == sel


def _topk_stages(sk_ref, chunks, ids_ref, pos_ref, scratch):
    v1_ref, i1_ref, v2_ref, i2_ref, best_ref, s_ref, cand_ref, expert_ref, picked_ref = scratch
    nkeys = float(N_SUB_KEYS)
    tile = slice(0, SUBLANES)
    for qp, lanes, _ in chunks:
        s_ref[0, :, lanes] = _dot_nt(sk_ref[0, 0], qp[:, :PEER_HALF])
        s_ref[1, :, lanes] = _dot_nt(sk_ref[0, 1], qp[:, PEER_HALF:])
    live = [(s_ref, (half, tile, lanes)) for _, lanes, _ in chunks for half in range(2)]
    yield live
    for j in range(PEER_TOPK):
        for _, lanes, _ in chunks:
            for half, (val_ref, id_ref) in enumerate(((v1_ref, i1_ref), (v2_ref, i2_ref))):
                s = s_ref[half, :, lanes]
                m, sel, hit = _extract_round(s, ids_ref[...], nkeys)
                val_ref[j:j + 1, lanes] = m
                id_ref[j:j + 1, lanes] = sel
                s_ref[half, :, lanes] = jnp.where(hit, -jnp.inf, s)
        yield live

    for _, lanes, _ in chunks:
        for r, (a, b0) in enumerate(_CAND_BLOCKS):
            rows = slice(r * SUBLANES, (r + 1) * SUBLANES)
            cand_ref[rows, lanes] = v1_ref[a:a + 1, lanes] + v2_ref[b0:b0 + 8, lanes]
            expert_ref[rows, lanes] = i1_ref[a:a + 1, lanes] * nkeys + i2_ref[b0:b0 + 8, lanes]
        rows = slice(CAND_ROWS - SUBLANES, CAND_ROWS)
        cand_ref[rows, lanes] = v1_ref[8:16, lanes] + v2_ref[0:1, lanes]
        expert_ref[rows, lanes] = i1_ref[8:16, lanes] * nkeys + i2_ref[0:1, lanes]
        cand_ref[:, lanes] = jnp.where(pos_ref[...] < POS_INVALID, cand_ref[:, lanes], -jnp.inf)
    live = [(cand_ref, (tile, lanes)) for _, lanes, _ in chunks]
    yield live
    for j in range(PEER_TOPK):
        for _, lanes, _ in chunks:
            cand = cand_ref[:, lanes]
            m, _, hit = _extract_round(cand, pos_ref[...], POS_INVALID)
            best_ref[j:j + 1, lanes] = m
            picked_ref[j:j + 1, lanes] = jnp.sum(jnp.where(hit, expert_ref[:, lanes], 0.0), axis=0,
                                                 keepdims=True)
            cand_ref[:, lanes] = jnp.where(hit, -jnp.inf, cand)
        if j % 2 == 1:
            yield live
    for _, lanes, store in chunks:
        best = best_ref[:, lanes]
        p = jnp.exp(best - best_ref[0:1, lanes])
        store(picked_ref[:, lanes].astype(jnp.int32), p / jnp.sum(p, axis=0, keepdims=True))
    yield []


def _order_after(live, value_tile):
    zero = jnp.minimum(jnp.abs(value_tile), 0.0)
    for ref, idx in live:
        ref[idx] = ref[idx] + zero


def _topk_scratch(n_chunks):
    w = n_chunks * LANES
    return ([pltpu.VMEM((PEER_TOPK, w), F32) for _ in range(5)]
            + [pltpu.VMEM((2, N_SUB_KEYS, w), F32), pltpu.VMEM((CAND_ROWS, w), F32),
               pltpu.VMEM((CAND_ROWS, w), F32), pltpu.VMEM((PEER_TOPK, w), F32)])


def _topk_kernel(qp_ref, sk_ref, ids_ref, pos_ref, e_ref, g_ref, *scratch):
    def store(c):
        def fn(e, g):
            e_ref[c] = e
            g_ref[c] = g
        return fn

    chunks = [(qp_ref[pl.ds(c * LANES, LANES), :], pl.ds(c * LANES, LANES), store(c))
              for c in range(TOPK_TK // LANES)]
    for _ in _topk_stages(sk_ref, chunks, ids_ref, pos_ref, scratch):
        pass


def _topk(qp, sub_keys, ids, pos):
    n_lc = TOPK_TK // LANES
    return pl.pallas_call(
        _topk_kernel,
        grid=(1, PEER_HEADS),
        in_specs=[
            pl.BlockSpec((TOPK_TK, 2 * PEER_HALF), lambda i, h: (i, h)),
            pl.BlockSpec((1, 2, N_SUB_KEYS, PEER_HALF), lambda i, h: (h, 0, 0, 0)),
            pl.BlockSpec((N_SUB_KEYS, LANES), lambda i, h: (0, 0)),
            pl.BlockSpec((CAND_ROWS, LANES), lambda i, h: (0, 0)),
        ],
        out_specs=[
            pl.BlockSpec((n_lc, PEER_TOPK, LANES), lambda i, h: (0, h, 0)),
            pl.BlockSpec((n_lc, PEER_TOPK, LANES), lambda i, h: (0, h, 0)),
        ],
        out_shape=[
            jax.ShapeDtypeStruct((n_lc, N_PICKS, LANES), jnp.int32),
            jax.ShapeDtypeStruct((n_lc, N_PICKS, LANES), F32),
        ],
        scratch_shapes=_topk_scratch(n_lc),
        compiler_params=pltpu.CompilerParams(
            dimension_semantics=("arbitrary", "arbitrary"), vmem_limit_bytes=VMEM_LIMIT),
        name="topk",
    )(qp, sub_keys, ids, pos)


PEER_TM = 512
PEER_CHUNK = 2048
PEER_SLABS = PEER_CHUNK // N_SUB_KEYS
PEER_NCHUNK = (N_SUB_KEYS * N_SUB_KEYS) // PEER_CHUNK
PEER_TG = 16


PEER_SUB = 4
PEER_SELECT_LEAD = 3
PEER_SELECT_PER_STAGE = 3
PEER_LC = PEER_TM // LANES
PEER_LC_STEP = PEER_LC * PEER_HEADS // PEER_NCHUNK
assert PEER_NCHUNK * PEER_LC_STEP == PEER_LC * PEER_HEADS and PEER_TM == TOPK_TK


def _peer_kernel(xn_ref, u_ref, v_ref, e0_ref, g0_ref, h1_ref, qpn_ref, sk_ref, ids_ref, pos_ref,
                 o_ref, hg_ref, et_ref, gt_ref, wa_ref, wb_ref, ebuf_ref, gbuf_ref, *topk_scratch):
    i = pl.program_id(0)
    j = pl.program_id(1)
    cur = i % 2

    @pl.when(jnp.logical_and(i == 0, j == 0))
    def _():
        ebuf_ref[0] = e0_ref[...]
        gbuf_ref[0] = g0_ref[...]

    @pl.when(j == 0)
    def _():
        for c in range(PEER_LC):
            et_ref[c * LANES:(c + 1) * LANES, :] = ebuf_ref[cur, c].T
            gt_ref[c * LANES:(c + 1) * LANES, :] = gbuf_ref[cur, c].T
        sub = lax.broadcasted_iota(jnp.int32, (N_SUB_KEYS, N_PICKS), 0)

        def scatter(gi, w_ref):
            t0 = gi * PEER_TG if isinstance(gi, int) else pl.multiple_of(gi * PEER_TG, PEER_TG)
            for tt in range(PEER_TG):
                e_row = et_ref[pl.ds(t0 + tt, 1), :]
                g_row = gt_ref[pl.ds(t0 + tt, 1), :]
                first = jnp.where(sub == (e_row >> 7), g_row, 0.0).astype(BF16)
                second = jnp.where(sub == (e_row & (N_SUB_KEYS - 1)), 1.0, 0.0).astype(BF16)
                w = _dot_nt(first, second)
                for v in range(N_SUB_KEYS // SUBLANES):
                    w_ref[v, pl.ds(tt * SUBLANES, SUBLANES), :] = w[v * SUBLANES:(v + 1) * SUBLANES, :]

        def relayout(gi, w_ref):
            t0 = gi * PEER_TG if isinstance(gi, int) else pl.multiple_of(gi * PEER_TG, PEER_TG)
            for i1 in range(N_SUB_KEYS):
                v, r = divmod(i1, SUBLANES)
                rows = w_ref[v, pl.ds(r, PEER_TG, stride=SUBLANES), :]
                c, s = divmod(i1, PEER_SLABS)
                hg_ref[c, pl.ds(t0, PEER_TG), s * N_SUB_KEYS:(s + 1) * N_SUB_KEYS] = rows.astype(BF16)

        def pair(k, carry):
            scatter(2 * k + 1, wb_ref)
            relayout(2 * k, wa_ref)
            scatter(2 * k + 2, wa_ref)
            relayout(2 * k + 1, wb_ref)
            return carry

        n_groups = PEER_TM // PEER_TG
        scatter(0, wa_ref)
        lax.fori_loop(0, n_groups // 2 - 1, pair, 0)
        scatter(n_groups - 1, wb_ref)
        relayout(n_groups - 2, wa_ref)
        relayout(n_groups - 1, wb_ref)
        o_ref[...] = h1_ref[...]

    def expert_stages():
        xn = xn_ref[...]
        sub = PEER_CHUNK // PEER_SUB
        hs = []
        for q in range(PEER_SUB):
            es = slice(q * sub, (q + 1) * sub)
            a = _dot_nt(xn, u_ref[es, :])
            hq = _gelu(a) * hg_ref[j, :, es].astype(F32)
            hs.append(hq.astype(BF16))
            yield hq[:SUBLANES, :LANES] + hq[-SUBLANES:, -LANES:]
        for q in range(PEER_SUB):
            es = slice(q * sub, (q + 1) * sub)
            r = _dot(hs[q], v_ref[es, :])
            o_ref[...] += r
            yield r[:SUBLANES, :LANES]

    head_rows = pl.ds(pl.multiple_of((j * PEER_LC_STEP // PEER_LC) * PEER_TOPK, PEER_TOPK), PEER_TOPK)

    def store(lc):
        def fn(e, g):
            ebuf_ref[1 - cur, lc, head_rows, :] = e
            gbuf_ref[1 - cur, lc, head_rows, :] = g
        return fn

    chunks = []
    for c in range(PEER_LC_STEP):
        lc = (j * PEER_LC_STEP) % PEER_LC + c
        qp = qpn_ref[pl.ds(pl.multiple_of(lc * LANES, LANES), LANES), :]
        chunks.append((qp, pl.ds(c * LANES, LANES), store(lc)))
    select = _topk_stages(sk_ref, chunks, ids_ref, pos_ref, topk_scratch)
    experts = expert_stages()
    live = []
    for _ in range(PEER_SELECT_LEAD):
        live = next(select, [])
    for result_tile in experts:
        _order_after(live, result_tile)
        for _ in range(PEER_SELECT_PER_STAGE):
            live = next(select, [])
    for _ in select:
        pass


def _peer(xn2, u_tab, v_tab, e0, g0, h1, qp, sub_keys, ids, pos):
    t, d = xn2.shape
    nblk = t // PEER_TM
    steps_per_head = PEER_LC // PEER_LC_STEP
    const3 = lambda i, j: (0, 0, 0)
    return pl.pallas_call(
        _peer_kernel,
        grid=(nblk, PEER_NCHUNK),
        in_specs=[
            pl.BlockSpec((PEER_TM, d), lambda i, j: (i, 0)),
            pl.BlockSpec((PEER_CHUNK, d), lambda i, j: (j, 0)),
            pl.BlockSpec((PEER_CHUNK, d), lambda i, j: (j, 0)),
            pl.BlockSpec((PEER_LC, N_PICKS, LANES), const3),
            pl.BlockSpec((PEER_LC, N_PICKS, LANES), const3),
            pl.BlockSpec((PEER_TM, d), lambda i, j: (i, 0)),
            pl.BlockSpec((PEER_TM, 2 * PEER_HALF),
                         lambda i, j: (jnp.minimum(i + 1, nblk - 1), j // steps_per_head)),
            pl.BlockSpec((1, 2, N_SUB_KEYS, PEER_HALF), lambda i, j: (j // steps_per_head, 0, 0, 0)),
            pl.BlockSpec((N_SUB_KEYS, LANES), lambda i, j: (0, 0)),
            pl.BlockSpec((CAND_ROWS, LANES), lambda i, j: (0, 0)),
        ],
        out_specs=pl.BlockSpec((PEER_TM, d), lambda i, j: (i, 0)),
        out_shape=jax.ShapeDtypeStruct((t, d), F32),
        scratch_shapes=[
            pltpu.VMEM((PEER_NCHUNK, PEER_TM, PEER_CHUNK), BF16),
            pltpu.VMEM((PEER_TM, N_PICKS), jnp.int32),
            pltpu.VMEM((PEER_TM, N_PICKS), F32),
            pltpu.VMEM((N_SUB_KEYS // SUBLANES, PEER_TG * SUBLANES, N_SUB_KEYS), F32),
            pltpu.VMEM((N_SUB_KEYS // SUBLANES, PEER_TG * SUBLANES, N_SUB_KEYS), F32),
            pltpu.VMEM((2, PEER_LC, N_PICKS, LANES), jnp.int32),
            pltpu.VMEM((2, PEER_LC, N_PICKS, LANES), F32),
        ] + _topk_scratch(PEER_LC_STEP),
        compiler_params=pltpu.CompilerParams(
            dimension_semantics=("arbitrary", "arbitrary"), vmem_limit_bytes=VMEM_LIMIT),
        name="peer",
    )(xn2, u_tab, v_tab, e0, g0, h1, qp, sub_keys, ids, pos)


def _s5_discretise_kernel(lr_ref, li_ref, ls_ref, br_ref, bi_ref, ar_ref, ai_ref, bbr_ref, bbi_ref):
    lr, li = lr_ref[...], li_ref[...]
    step = jnp.exp(ls_ref[...])
    mag = jnp.exp(lr * step)
    abar_re = mag * jnp.cos(li * step)
    abar_im = mag * jnp.sin(li * step)
    den = lr * lr + li * li
    nr, ni = abar_re - 1.0, abar_im
    coef_re = (nr * lr + ni * li) / den
    coef_im = (ni * lr - nr * li) / den
    ar_ref[...] = abar_re
    ai_ref[...] = abar_im
    br, bi = br_ref[...], bi_ref[...]
    bbr_ref[...] = coef_re * br - coef_im * bi
    bbi_ref[...] = coef_re * bi + coef_im * br


def _s5_params(lam_re, lam_im, log_step, b_re, b_im, c_re, c_im, bsz):
    col = lambda m: m.reshape(S5_LANES, 1)
    pole = jax.ShapeDtypeStruct((S5_LANES, 1), F32)
    pole_in = jax.ShapeDtypeStruct((S5_LANES, S5_GROUP), F32)
    abar_re, abar_im, bbar_re, bbar_im = pl.pallas_call(
        _s5_discretise_kernel, out_shape=[pole, pole, pole_in, pole_in], name="s5_discretise",
    )(col(lam_re), col(lam_im), col(jnp.repeat(log_step, S5_STATE)),
      b_re.reshape(S5_LANES, S5_GROUP), b_im.reshape(S5_LANES, S5_GROUP))
    nlc = S5_LANES // S5_LC
    gpc = S5_GROUPS // nlc
    eye = jnp.eye(gpc, dtype=F32)
    to_b = lambda m: jnp.einsum('lgpc,gh->lgchp', m.reshape(nlc, gpc, S5_STATE, S5_GROUP),
                                eye).reshape(nlc, S5_CH, S5_LC)
    to_c = lambda m: jnp.einsum('lgcp,gh->lgphc', m.reshape(nlc, gpc, S5_GROUP, S5_STATE),
                                eye).reshape(nlc, S5_LC, S5_CH)
    bblk = jnp.concatenate([to_b(bbar_re), to_b(bbar_im)], axis=2).astype(BF16)
    cblk = jnp.concatenate([to_c(c_re), -to_c(c_im)], axis=1).astype(BF16)
    a = jnp.stack([abar_re.reshape(-1), abar_im.reshape(-1)])
    a8 = jnp.broadcast_to(a[:, None, :], (2, bsz, S5_LANES))
    return a8, bblk, cblk


def kernel(x, ln1_g, w_in, b_gate, s5_lambda_re, s5_lambda_im, s5_log_step, s5_b_re, s5_b_im, s5_c_re, s5_c_im, s5_d, s5_w_glu, w_s5_branch, q_norm_g, k_norm_g, rel_bias_table, attn_sinks, w_attn_branch, w_out, ln2_g, peer_w_query, peer_sub_keys, peer_u, peer_v):
    bsz, seq, d = x.shape
    h = x
    for layer in range(ln1_g.shape[0]):
        w_uqkv = w_in[layer, :, :UQKV_WIDTH].astype(BF16)
        w_gate = w_in[layer, :, UQKV_WIDTH:].astype(BF16)
        u_tm, q, k, v = _in_proj(h, ln1_g[layer][None], w_uqkv)

        a8, bblk, cblk = _s5_params(s5_lambda_re[layer], s5_lambda_im[layer], s5_log_step[layer],
                                    s5_b_re[layer], s5_b_im[layer], s5_c_re[layer], s5_c_im[layer], bsz)
        ys5 = _s5(u_tm, a8, bblk, cblk, s5_d[layer].reshape(1, S5_WIDTH),
                  s5_w_glu[layer].astype(BF16), w_s5_branch[layer].astype(BF16), bsz)

        attn = _attention(q, k, v, q_norm_g[layer][None], k_norm_g[layer][None],
                          _t5_bucket_matrix(), rel_bias_table, attn_sinks[layer])

        h1, xn2, qp, u_bf, v_bf = _merge(
            h, attn, ys5, ln1_g[layer][None], w_gate, b_gate[layer][None],
            w_attn_branch[layer].astype(BF16), w_out[layer].astype(BF16), ln2_g[layer][None],
            peer_w_query[layer].astype(BF16), peer_u[layer], peer_v[layer])

        sub_keys = peer_sub_keys[layer].astype(BF16)
        pos = jnp.asarray(_CAND_POS)
        ids = jnp.asarray(_KEY_IDS)
        e0, g0 = _topk(qp, sub_keys, ids, pos)
        out = _peer(xn2, u_bf, v_bf, e0, g0, h1, qp, sub_keys, ids, pos)
        h = out.reshape(bsz, seq, d)
    return h
```

```python
import functools
import math

import jax
import jax.numpy as jnp
import numpy as np
from jax import lax
from jax.experimental import pallas as pl
from jax.experimental.pallas import tpu as pltpu

F32 = jnp.float32
BF16 = jnp.bfloat16

D_MODEL = 1024
RMS_EPS = 1e-6
NEG_INF = -1e30

S5_WIDTH = 512
S5_GROUP = 16
S5_GROUPS = 32
S5_STATE = 64
S5_LANES = S5_GROUPS * S5_STATE

N_Q_HEADS = 16
N_KV_HEADS = 4
HEAD_DIM = 64
Q_PER_KV = 4
ATTN_BLOCK = 128
N_BUCKETS = 32
MAX_DISTANCE = 128
Q_WIDTH = 1024
KV_WIDTH = 256
UQKV_WIDTH = S5_WIDTH + Q_WIDTH + 2 * KV_WIDTH

PEER_HEADS = 8
PEER_HALF = 64
N_SUB_KEYS = 128
PEER_TOPK = 16
N_PICKS = PEER_HEADS * PEER_TOPK

VMEM_LIMIT = 56 * 1024 * 1024

SQRT_HALF = 0.7071067811865476


def _rms(x, g):
    r = lax.rsqrt(jnp.mean(x * x, axis=-1, keepdims=True) + RMS_EPS)
    return (x * r) * g


def _gelu(x):
    return 0.5 * x * (1.0 + lax.erf(x * SQRT_HALF))


def _dot(a, b):
    return jnp.dot(a, b, preferred_element_type=F32)


def _dot_nt(a, b):
    return lax.dot_general(a, b, (((1,), (1,)), ((), ())), preferred_element_type=F32)


S5_TC = 128


def _in_proj_kernel(bsz, x_ref, g_ref, w_ref, u_ref, q_ref, k_ref, v_ref):
    d = x_ref.shape[-1]
    xn = _rms(x_ref[...].reshape(bsz * S5_TC, d), g_ref[...]).astype(BF16)
    p = _dot(xn, w_ref[...])
    u = p[:, :S5_WIDTH].reshape(bsz, S5_TC, S5_WIDTH)
    u_ref[...] = jnp.transpose(u, (1, 0, 2)).reshape(S5_TC * bsz, S5_WIDTH)
    q_ref[...] = p[:, S5_WIDTH:S5_WIDTH + Q_WIDTH].astype(BF16).reshape(bsz, S5_TC, Q_WIDTH)
    k_ref[...] = p[:, S5_WIDTH + Q_WIDTH:S5_WIDTH + Q_WIDTH + KV_WIDTH].astype(BF16).reshape(
        bsz, S5_TC, KV_WIDTH)
    v_ref[...] = p[:, S5_WIDTH + Q_WIDTH + KV_WIDTH:].astype(BF16).reshape(bsz, S5_TC, KV_WIDTH)


def _in_proj(x, ln1_g, w_uqkv):
    bsz, seq, d = x.shape
    return pl.pallas_call(
        functools.partial(_in_proj_kernel, bsz),
        grid=(seq // S5_TC,),
        in_specs=[
            pl.BlockSpec((bsz, S5_TC, d), lambda i: (0, i, 0)),
            pl.BlockSpec((1, d), lambda i: (0, 0)),
            pl.BlockSpec((d, UQKV_WIDTH), lambda i: (0, 0)),
        ],
        out_specs=[
            pl.BlockSpec((S5_TC * bsz, S5_WIDTH), lambda i: (i, 0)),
            pl.BlockSpec((bsz, S5_TC, Q_WIDTH), lambda i: (0, i, 0)),
            pl.BlockSpec((bsz, S5_TC, KV_WIDTH), lambda i: (0, i, 0)),
            pl.BlockSpec((bsz, S5_TC, KV_WIDTH), lambda i: (0, i, 0)),
        ],
        out_shape=[
            jax.ShapeDtypeStruct((seq * bsz, S5_WIDTH), F32),
            jax.ShapeDtypeStruct((bsz, seq, Q_WIDTH), BF16),
            jax.ShapeDtypeStruct((bsz, seq, KV_WIDTH), BF16),
            jax.ShapeDtypeStruct((bsz, seq, KV_WIDTH), BF16),
        ],
        compiler_params=pltpu.CompilerParams(
            dimension_semantics=("arbitrary",), vmem_limit_bytes=VMEM_LIMIT),
        name="in_proj",
    )(x, ln1_g, w_uqkv)


S5_LC = 512
S5_CH = S5_LC // S5_STATE * S5_GROUP


def _s5_kernel(bsz, u_ref, a_ref, bblk_ref, cblk_ref, d_ref, wglu_ref, wbr_ref, o_ref,
               h_ref, bu_ref):
    @pl.when(pl.program_id(0) == 0)
    def _():
        h_ref[...] = jnp.zeros_like(h_ref)

    u = u_ref[...]
    ub = u.astype(BF16)
    for lc in range(S5_LANES // S5_LC):
        ch = slice(lc * S5_CH, (lc + 1) * S5_CH)
        for part in range(2):
            cols = pl.ds(part * S5_LANES + lc * S5_LC, S5_LC)
            bu_ref[:, cols] = _dot(ub[:, ch], bblk_ref[lc, :, part * S5_LC:(part + 1) * S5_LC])

    for lc in range(S5_LANES // S5_LC):
        re = pl.ds(lc * S5_LC, S5_LC)
        im = pl.ds(S5_LANES + lc * S5_LC, S5_LC)
        ar = a_ref[0, :, re]
        ai = a_ref[1, :, re]

        hr, hi = h_ref[0, :, re], h_ref[1, :, re]
        for t in range(S5_TC):
            rows = pl.ds(t * bsz, bsz)
            nr = ar * hr - ai * hi + bu_ref[rows, re]
            ni = ar * hi + ai * hr + bu_ref[rows, im]
            bu_ref[rows, re] = nr
            bu_ref[rows, im] = ni
            hr, hi = nr, ni
        h_ref[0, :, re] = hr
        h_ref[1, :, re] = hi

    ys = []
    for lc in range(S5_LANES // S5_LC):
        ch = slice(lc * S5_CH, (lc + 1) * S5_CH)
        re = pl.ds(lc * S5_LC, S5_LC)
        im = pl.ds(S5_LANES + lc * S5_LC, S5_LC)
        ys.append(_dot(bu_ref[:, re].astype(BF16), cblk_ref[lc, :S5_LC, :])
                  + _dot(bu_ref[:, im].astype(BF16), cblk_ref[lc, S5_LC:, :]))
    y = jnp.concatenate(ys, axis=1) + d_ref[...] * u
    yg = _gelu(y).astype(BF16)
    ab = _dot(yg, wglu_ref[...])
    glu = ab[:, :S5_WIDTH] * jax.nn.sigmoid(ab[:, S5_WIDTH:])
    glu = jnp.transpose(glu.reshape(S5_TC, bsz, S5_WIDTH), (1, 0, 2)).reshape(bsz * S5_TC, S5_WIDTH)
    o_ref[...] = _dot(glu.astype(BF16), wbr_ref[...]).astype(BF16).reshape(bsz, S5_TC, D_MODEL)


def _s5(u_tm, a8, bblk, cblk, d_row, w_glu, w_br, bsz):
    rows = u_tm.shape[0]
    tr = S5_TC * bsz
    const = lambda shape: pl.BlockSpec(shape, lambda i: (0,) * len(shape))
    return pl.pallas_call(
        functools.partial(_s5_kernel, bsz),
        grid=(rows // tr,),
        in_specs=[
            pl.BlockSpec((tr, S5_WIDTH), lambda i: (i, 0)),
            const((2, bsz, S5_LANES)),
            const((S5_LANES // S5_LC, S5_CH, 2 * S5_LC)),
            const((S5_LANES // S5_LC, 2 * S5_LC, S5_CH)),
            const((1, S5_WIDTH)),
            const((S5_WIDTH, 2 * S5_WIDTH)),
            const((S5_WIDTH, D_MODEL)),
        ],
        out_specs=pl.BlockSpec((bsz, S5_TC, D_MODEL), lambda i: (0, i, 0)),
        out_shape=jax.ShapeDtypeStruct((bsz, rows // bsz, D_MODEL), BF16),
        scratch_shapes=[
            pltpu.VMEM((2, bsz, S5_LANES), F32),
            pltpu.VMEM((tr, 2 * S5_LANES), F32),
        ],
        compiler_params=pltpu.CompilerParams(
            dimension_semantics=("arbitrary",), vmem_limit_bytes=VMEM_LIMIT),
        name="s5",
    )(u_tm, a8, bblk, cblk, d_row, w_glu, w_br)


def _attn_kernel(q_ref, kc_ref, kp_ref, vc_ref, vp_ref, qg_ref, kg_ref, bucket_ref,
                 table_ref, sink_ref, o_ref, bias_ref):
    first = jnp.logical_and(pl.program_id(0) == 0, pl.program_id(1) == 0)

    @pl.when(first)
    def _():
        bucket = bucket_ref[...]
        qi = lax.broadcasted_iota(jnp.int32, (ATTN_BLOCK, 2 * ATTN_BLOCK), 0)
        si = lax.broadcasted_iota(jnp.int32, (ATTN_BLOCK, 2 * ATTN_BLOCK), 1)
        dist = ATTN_BLOCK + qi - si
        band = (dist >= 0) & (dist < ATTN_BLOCK)
        for h in range(N_Q_HEADS):
            acc = jnp.zeros((ATTN_BLOCK, 2 * ATTN_BLOCK), F32)
            for b in range(N_BUCKETS):
                acc = jnp.where(bucket == b, table_ref[b, h], acc)
            bias_ref[0, h] = jnp.where(band, acc, NEG_INF)
            bias_ref[1, h] = jnp.where(band & (si >= ATTN_BLOCK), acc, NEG_INF)

    blk = pl.program_id(1)
    qg = qg_ref[...] * HEAD_DIM ** -0.5
    kg = kg_ref[...]

    grp_rows = Q_PER_KV * ATTN_BLOCK
    row_head = lax.broadcasted_iota(jnp.int32, (grp_rows, 1), 0) // ATTN_BLOCK

    ones_d = jnp.ones((HEAD_DIM, LANES), BF16)
    ones_s = jnp.ones((2 * ATTN_BLOCK, LANES), BF16)

    def rms_rows(x, g):
        ssq = _dot((x * x).astype(BF16), ones_d)[:, :HEAD_DIM]
        return (x * lax.rsqrt(ssq * (1.0 / HEAD_DIM) + RMS_EPS)) * g

    groups = [(sb, kh) for sb in range(ATTN_QB) for kh in range(N_KV_HEADS)]
    knorm, vall = [], []
    for kh in range(N_KV_HEADS):
        cols = slice(kh * HEAD_DIM, (kh + 1) * HEAD_DIM)
        kall = jnp.concatenate([kp_ref[0, :, cols], kc_ref[0, :, cols]], axis=0).astype(F32)
        knorm.append(rms_rows(kall, kg).astype(BF16))
        vall.append(jnp.concatenate([vp_ref[0, :, cols], vc_ref[0, :, cols]], axis=0))
    qss = []
    for sb, kh in groups:
        h0 = kh * Q_PER_KV
        rows = slice(sb * ATTN_BLOCK, (sb + 1) * ATTN_BLOCK)
        qs = jnp.concatenate(
            [q_ref[0, rows, (h0 + g) * HEAD_DIM:(h0 + g + 1) * HEAD_DIM] for g in range(Q_PER_KV)],
            axis=0).astype(F32)
        qss.append(rms_rows(qs, qg).astype(BF16))
    ss, sinks = [], []
    for gi, (sb, kh) in enumerate(groups):
        h0 = kh * Q_PER_KV
        keys = slice(sb * ATTN_BLOCK, (sb + 2) * ATTN_BLOCK)
        s = _dot_nt(qss[gi], knorm[kh][keys]).reshape(Q_PER_KV, ATTN_BLOCK, 2 * ATTN_BLOCK)
        variant = jnp.where(blk > 0, 0, 1) if sb == 0 else 0
        ss.append((s + bias_ref[variant, h0:h0 + Q_PER_KV]).reshape(grp_rows, 2 * ATTN_BLOCK))
        sink = jnp.full((grp_rows, 1), sink_ref[h0], F32)
        for g in range(1, Q_PER_KV):
            sink = jnp.where(row_head == g, sink_ref[h0 + g], sink)
        sinks.append(sink)
    ps, dens = [], []
    for gi in range(len(groups)):
        m = jnp.maximum(jnp.max(ss[gi], axis=-1, keepdims=True), sinks[gi])
        p = jnp.exp(ss[gi] - m).astype(BF16)
        ps.append(p)
        dens.append(_dot(p, ones_s)[:, :HEAD_DIM] + jnp.exp(sinks[gi] - m))
    for gi, (sb, kh) in enumerate(groups):
        h0 = kh * Q_PER_KV
        keys = slice(sb * ATTN_BLOCK, (sb + 2) * ATTN_BLOCK)
        o = (_dot(ps[gi], vall[kh][keys]) / dens[gi]).astype(BF16)
        for g in range(Q_PER_KV):
            o_ref[0, sb * ATTN_BLOCK:(sb + 1) * ATTN_BLOCK, (h0 + g) * HEAD_DIM:(h0 + g + 1) * HEAD_DIM] = (
                o[g * ATTN_BLOCK:(g + 1) * ATTN_BLOCK])


ATTN_QB = 2


def _attention(q, k, v, qg, kg, bucket, table, sinks):
    bsz, seq, _ = q.shape
    rows = ATTN_QB * ATTN_BLOCK
    cur = lambda b, i: (b, i, 0)
    prev = lambda b, i: (b, jnp.maximum(i * ATTN_QB - 1, 0), 0)
    const2 = lambda b, i: (0, 0)
    return pl.pallas_call(
        _attn_kernel,
        grid=(bsz, seq // rows),
        in_specs=[
            pl.BlockSpec((1, rows, Q_WIDTH), cur),
            pl.BlockSpec((1, rows, KV_WIDTH), cur),
            pl.BlockSpec((1, ATTN_BLOCK, KV_WIDTH), prev),
            pl.BlockSpec((1, rows, KV_WIDTH), cur),
            pl.BlockSpec((1, ATTN_BLOCK, KV_WIDTH), prev),
            pl.BlockSpec((1, HEAD_DIM), const2),
            pl.BlockSpec((1, HEAD_DIM), const2),
            pl.BlockSpec((ATTN_BLOCK, 2 * ATTN_BLOCK), const2),
            pl.BlockSpec(memory_space=pltpu.SMEM),
            pl.BlockSpec(memory_space=pltpu.SMEM),
        ],
        out_specs=pl.BlockSpec((1, rows, Q_WIDTH), cur),
        out_shape=jax.ShapeDtypeStruct((bsz, seq, Q_WIDTH), BF16),
        scratch_shapes=[pltpu.VMEM((2, N_Q_HEADS, ATTN_BLOCK, 2 * ATTN_BLOCK), F32)],
        compiler_params=pltpu.CompilerParams(
            dimension_semantics=("arbitrary", "arbitrary"), vmem_limit_bytes=VMEM_LIMIT),
        name="attn",
    )(q, k, k, v, v, qg, kg, bucket, table, sinks)


def _t5_bucket_matrix():
    qi = np.arange(ATTN_BLOCK)[:, None]
    si = np.arange(2 * ATTN_BLOCK)[None, :]
    dist = np.maximum(ATTN_BLOCK + qi - si, 0)
    max_exact = N_BUCKETS // 2
    d_f = np.maximum(dist, 1).astype(np.float64)
    large = max_exact + np.floor(np.log(d_f / max_exact) / math.log(MAX_DISTANCE / max_exact)
                                 * (N_BUCKETS - max_exact)).astype(np.int32)
    large = np.minimum(large, N_BUCKETS - 1)
    return jnp.asarray(np.where(dist < max_exact, dist, large).astype(np.int32))


MERGE_TM = 512


def _merge_kernel(x_ref, attn_ref, ys5_ref, g1_ref, wg_ref, bg_ref, wab_ref, wout_ref,
                  g2_ref, wpq_ref, u_ref, v_ref, h1_ref, xn2_ref, qp_ref, ub_ref, vb_ref):
    ub_ref[...] = u_ref[...].astype(BF16)
    vb_ref[...] = v_ref[...].astype(BF16)
    x = x_ref[0]
    xn = _rms(x, g1_ref[...]).astype(BF16)
    gates = jax.nn.sigmoid(_dot(xn, wg_ref[...]) + bg_ref[...])
    y_attn = _dot(attn_ref[0], wab_ref[...])
    y_s5 = ys5_ref[0].astype(F32)
    mix = gates[:, :D_MODEL] * y_s5 + gates[:, D_MODEL:] * y_attn
    h1 = x + _dot(mix.astype(BF16), wout_ref[...])
    h1_ref[...] = h1
    xn2 = _rms(h1, g2_ref[...]).astype(BF16)
    xn2_ref[...] = xn2
    qp_ref[...] = _dot(xn2, wpq_ref[...]).astype(BF16)


def _merge(x, attn, ys5_tm, ln1_g, w_gate, b_gate, w_ab, w_out, ln2_g, w_pq, u_tab, v_tab):
    bsz, seq, d = x.shape
    nt = seq // MERGE_TM
    tok = lambda b, i: (b * nt + i, 0)
    const2 = lambda b, i: (0, 0)
    n_experts = u_tab.shape[0]
    tab_rows = n_experts // (bsz * nt)
    assert tab_rows * bsz * nt == n_experts and v_tab.shape == u_tab.shape == (n_experts, d)
    return pl.pallas_call(
        _merge_kernel,
        grid=(bsz, nt),
        in_specs=[
            pl.BlockSpec((1, MERGE_TM, d), lambda b, i: (b, i, 0)),
            pl.BlockSpec((1, MERGE_TM, d), lambda b, i: (b, i, 0)),
            pl.BlockSpec((1, MERGE_TM, d), lambda b, i: (b, i, 0)),
            pl.BlockSpec((1, d), const2),
            pl.BlockSpec((d, 2 * d), const2),
            pl.BlockSpec((1, 2 * d), const2),
            pl.BlockSpec((d, d), const2),
            pl.BlockSpec((d, d), const2),
            pl.BlockSpec((1, d), const2),
            pl.BlockSpec((d, d), const2),
            pl.BlockSpec((tab_rows, d), tok),
            pl.BlockSpec((tab_rows, d), tok),
        ],
        out_specs=[
            pl.BlockSpec((MERGE_TM, d), tok),
            pl.BlockSpec((MERGE_TM, d), tok),
            pl.BlockSpec((MERGE_TM, d), tok),
            pl.BlockSpec((tab_rows, d), tok),
            pl.BlockSpec((tab_rows, d), tok),
        ],
        out_shape=[
            jax.ShapeDtypeStruct((bsz * seq, d), F32),
            jax.ShapeDtypeStruct((bsz * seq, d), BF16),
            jax.ShapeDtypeStruct((bsz * seq, d), BF16),
            jax.ShapeDtypeStruct((n_experts, d), BF16),
            jax.ShapeDtypeStruct((n_experts, d), BF16),
        ],
        compiler_params=pltpu.CompilerParams(
            dimension_semantics=("arbitrary", "arbitrary"), vmem_limit_bytes=VMEM_LIMIT),
        name="merge",
    )(x, attn, ys5_tm, ln1_g, w_gate, b_gate, w_ab, w_out, ln2_g, w_pq, u_tab, v_tab)


TOPK_TK = 512
LANES = 128
SUBLANES = 8
CAND_ROWS = 80
POS_INVALID = 1.0e9


def _cand_layout():
    pos = np.full((CAND_ROWS,), POS_INVALID, np.float32)
    blocks = [(0, 0), (0, 8), (1, 0), (2, 0), (3, 0), (4, 0), (5, 0), (6, 0), (7, 0)]
    for r, (a, b0) in enumerate(blocks):
        for j in range(8):
            b = b0 + j
            if (a + 1) * (b + 1) <= PEER_TOPK:
                pos[r * 8 + j] = a * PEER_TOPK + b
    for j in range(8):
        pos[72 + j] = (8 + j) * PEER_TOPK
    return blocks, np.broadcast_to(pos[:, None], (CAND_ROWS, LANES)).copy()


_CAND_BLOCKS, _CAND_POS = _cand_layout()
_KEY_IDS = np.broadcast_to(np.arange(N_SUB_KEYS, dtype=np.float32)[:, None], (N_SUB_KEYS, LANES)).copy()


def _extract_round(s, ids, big):
    m = jnp.max(s, axis=0, keepdims=True)
    sel = jnp.min(jnp.where(s == m, ids, big), axis=0, keepdims=True)
    return m, sel, ids == sel


def _topk_stages(sk_ref, chunks, ids_ref, pos_ref, scratch):
    v1_ref, i1_ref, v2_ref, i2_ref, best_ref, s_ref, cand_ref, expert_ref, picked_ref = scratch
    nkeys = float(N_SUB_KEYS)
    tile = slice(0, SUBLANES)
    for qp, lanes, _ in chunks:
        s_ref[0, :, lanes] = _dot_nt(sk_ref[0, 0], qp[:, :PEER_HALF])
        s_ref[1, :, lanes] = _dot_nt(sk_ref[0, 1], qp[:, PEER_HALF:])
    live = [(s_ref, (half, tile, lanes)) for _, lanes, _ in chunks for half in range(2)]
    yield live
    for j in range(PEER_TOPK):
        for _, lanes, _ in chunks:
            for half, (val_ref, id_ref) in enumerate(((v1_ref, i1_ref), (v2_ref, i2_ref))):
                s = s_ref[half, :, lanes]
                m, sel, hit = _extract_round(s, ids_ref[...], nkeys)
                val_ref[j:j + 1, lanes] = m
                id_ref[j:j + 1, lanes] = sel
                s_ref[half, :, lanes] = jnp.where(hit, -jnp.inf, s)
        yield live

    for _, lanes, _ in chunks:
        for r, (a, b0) in enumerate(_CAND_BLOCKS):
            rows = slice(r * SUBLANES, (r + 1) * SUBLANES)
            cand_ref[rows, lanes] = v1_ref[a:a + 1, lanes] + v2_ref[b0:b0 + 8, lanes]
            expert_ref[rows, lanes] = i1_ref[a:a + 1, lanes] * nkeys + i2_ref[b0:b0 + 8, lanes]
        rows = slice(CAND_ROWS - SUBLANES, CAND_ROWS)
        cand_ref[rows, lanes] = v1_ref[8:16, lanes] + v2_ref[0:1, lanes]
        expert_ref[rows, lanes] = i1_ref[8:16, lanes] * nkeys + i2_ref[0:1, lanes]
        cand_ref[:, lanes] = jnp.where(pos_ref[...] < POS_INVALID, cand_ref[:, lanes], -jnp.inf)
    live = [(cand_ref, (tile, lanes)) for _, lanes, _ in chunks]
    yield live
    for j in range(PEER_TOPK):
        for _, lanes, _ in chunks:
            cand = cand_ref[:, lanes]
            m, _, hit = _extract_round(cand, pos_ref[...], POS_INVALID)
            best_ref[j:j + 1, lanes] = m
            picked_ref[j:j + 1, lanes] = jnp.sum(jnp.where(hit, expert_ref[:, lanes], 0.0), axis=0,
                                                 keepdims=True)
            cand_ref[:, lanes] = jnp.where(hit, -jnp.inf, cand)
        if j % 2 == 1:
            yield live
    for _, lanes, store in chunks:
        best = best_ref[:, lanes]
        p = jnp.exp(best - best_ref[0:1, lanes])
        store(picked_ref[:, lanes].astype(jnp.int32), p / jnp.sum(p, axis=0, keepdims=True))
    yield []


def _order_after(live, value_tile):
    zero = jnp.minimum(jnp.abs(value_tile), 0.0)
    for ref, idx in live:
        ref[idx] = ref[idx] + zero


def _topk_scratch(n_chunks):
    w = n_chunks * LANES
    return ([pltpu.VMEM((PEER_TOPK, w), F32) for _ in range(5)]
            + [pltpu.VMEM((2, N_SUB_KEYS, w), F32), pltpu.VMEM((CAND_ROWS, w), F32),
               pltpu.VMEM((CAND_ROWS, w), F32), pltpu.VMEM((PEER_TOPK, w), F32)])


def _topk_kernel(qp_ref, sk_ref, ids_ref, pos_ref, e_ref, g_ref, *scratch):
    def store(c):
        def fn(e, g):
            e_ref[c] = e
            g_ref[c] = g
        return fn

    chunks = [(qp_ref[pl.ds(c * LANES, LANES), :], pl.ds(c * LANES, LANES), store(c))
              for c in range(TOPK_TK // LANES)]
    for _ in _topk_stages(sk_ref, chunks, ids_ref, pos_ref, scratch):
        pass


def _topk(qp, sub_keys, ids, pos):
    n_lc = TOPK_TK // LANES
    return pl.pallas_call(
        _topk_kernel,
        grid=(1, PEER_HEADS),
        in_specs=[
            pl.BlockSpec((TOPK_TK, 2 * PEER_HALF), lambda i, h: (i, h)),
            pl.BlockSpec((1, 2, N_SUB_KEYS, PEER_HALF), lambda i, h: (h, 0, 0, 0)),
            pl.BlockSpec((N_SUB_KEYS, LANES), lambda i, h: (0, 0)),
            pl.BlockSpec((CAND_ROWS, LANES), lambda i, h: (0, 0)),
        ],
        out_specs=[
            pl.BlockSpec((n_lc, PEER_TOPK, LANES), lambda i, h: (0, h, 0)),
            pl.BlockSpec((n_lc, PEER_TOPK, LANES), lambda i, h: (0, h, 0)),
        ],
        out_shape=[
            jax.ShapeDtypeStruct((n_lc, N_PICKS, LANES), jnp.int32),
            jax.ShapeDtypeStruct((n_lc, N_PICKS, LANES), F32),
        ],
        scratch_shapes=_topk_scratch(n_lc),
        compiler_params=pltpu.CompilerParams(
            dimension_semantics=("arbitrary", "arbitrary"), vmem_limit_bytes=VMEM_LIMIT),
        name="topk",
    )(qp, sub_keys, ids, pos)


PEER_TM = 512
PEER_CHUNK = 2048
PEER_SLABS = PEER_CHUNK // N_SUB_KEYS
PEER_NCHUNK = (N_SUB_KEYS * N_SUB_KEYS) // PEER_CHUNK
PEER_TG = 16


PEER_SUB = 4
PEER_SELECT_LEAD = 3
PEER_SELECT_PER_STAGE = 3
PEER_LC = PEER_TM // LANES
PEER_LC_STEP = PEER_LC * PEER_HEADS // PEER_NCHUNK
assert PEER_NCHUNK * PEER_LC_STEP == PEER_LC * PEER_HEADS and PEER_TM == TOPK_TK


def _peer_kernel(xn_ref, u_ref, v_ref, e0_ref, g0_ref, h1_ref, qpn_ref, sk_ref, ids_ref, pos_ref,
                 o_ref, hg_ref, et_ref, gt_ref, wa_ref, wb_ref, ebuf_ref, gbuf_ref, *topk_scratch):
    i = pl.program_id(0)
    j = pl.program_id(1)
    cur = i % 2

    @pl.when(jnp.logical_and(i == 0, j == 0))
    def _():
        ebuf_ref[0] = e0_ref[...]
        gbuf_ref[0] = g0_ref[...]

    @pl.when(j == 0)
    def _():
        for c in range(PEER_LC):
            et_ref[c * LANES:(c + 1) * LANES, :] = ebuf_ref[cur, c].T
            gt_ref[c * LANES:(c + 1) * LANES, :] = gbuf_ref[cur, c].T
        sub = lax.broadcasted_iota(jnp.int32, (N_SUB_KEYS, N_PICKS), 0)

        def scatter(gi, w_ref):
            t0 = gi * PEER_TG if isinstance(gi, int) else pl.multiple_of(gi * PEER_TG, PEER_TG)
            for tt in range(PEER_TG):
                e_row = et_ref[pl.ds(t0 + tt, 1), :]
                g_row = gt_ref[pl.ds(t0 + tt, 1), :]
                first = jnp.where(sub == (e_row >> 7), g_row, 0.0).astype(BF16)
                second = jnp.where(sub == (e_row & (N_SUB_KEYS - 1)), 1.0, 0.0).astype(BF16)
                w = _dot_nt(first, second)
                for v in range(N_SUB_KEYS // SUBLANES):
                    w_ref[v, pl.ds(tt * SUBLANES, SUBLANES), :] = w[v * SUBLANES:(v + 1) * SUBLANES, :]

        def relayout(gi, w_ref):
            t0 = gi * PEER_TG if isinstance(gi, int) else pl.multiple_of(gi * PEER_TG, PEER_TG)
            for i1 in range(N_SUB_KEYS):
                v, r = divmod(i1, SUBLANES)
                rows = w_ref[v, pl.ds(r, PEER_TG, stride=SUBLANES), :]
                c, s = divmod(i1, PEER_SLABS)
                hg_ref[c, pl.ds(t0, PEER_TG), s * N_SUB_KEYS:(s + 1) * N_SUB_KEYS] = rows.astype(BF16)

        def pair(k, carry):
            scatter(2 * k + 1, wb_ref)
            relayout(2 * k, wa_ref)
            scatter(2 * k + 2, wa_ref)
            relayout(2 * k + 1, wb_ref)
            return carry

        n_groups = PEER_TM // PEER_TG
        scatter(0, wa_ref)
        lax.fori_loop(0, n_groups // 2 - 1, pair, 0)
        scatter(n_groups - 1, wb_ref)
        relayout(n_groups - 2, wa_ref)
        relayout(n_groups - 1, wb_ref)
        o_ref[...] = h1_ref[...]

    def expert_stages():
        xn = xn_ref[...]
        sub = PEER_CHUNK // PEER_SUB
        hs = []
        for q in range(PEER_SUB):
            es = slice(q * sub, (q + 1) * sub)
            a = _dot_nt(xn, u_ref[es, :])
            hq = _gelu(a) * hg_ref[j, :, es].astype(F32)
            hs.append(hq.astype(BF16))
            yield hq[:SUBLANES, :LANES] + hq[-SUBLANES:, -LANES:]
        for q in range(PEER_SUB):
            es = slice(q * sub, (q + 1) * sub)
            r = _dot(hs[q], v_ref[es, :])
            o_ref[...] += r
            yield r[:SUBLANES, :LANES]

    head_rows = pl.ds(pl.multiple_of((j * PEER_LC_STEP // PEER_LC) * PEER_TOPK, PEER_TOPK), PEER_TOPK)

    def store(lc):
        def fn(e, g):
            ebuf_ref[1 - cur, lc, head_rows, :] = e
            gbuf_ref[1 - cur, lc, head_rows, :] = g
        return fn

    chunks = []
    for c in range(PEER_LC_STEP):
        lc = (j * PEER_LC_STEP) % PEER_LC + c
        qp = qpn_ref[pl.ds(pl.multiple_of(lc * LANES, LANES), LANES), :]
        chunks.append((qp, pl.ds(c * LANES, LANES), store(lc)))
    select = _topk_stages(sk_ref, chunks, ids_ref, pos_ref, topk_scratch)
    experts = expert_stages()
    live = []
    for _ in range(PEER_SELECT_LEAD):
        live = next(select, [])
    for result_tile in experts:
        _order_after(live, result_tile)
        for _ in range(PEER_SELECT_PER_STAGE):
            live = next(select, [])
    for _ in select:
        pass


def _peer(xn2, u_tab, v_tab, e0, g0, h1, qp, sub_keys, ids, pos):
    t, d = xn2.shape
    nblk = t // PEER_TM
    steps_per_head = PEER_LC // PEER_LC_STEP
    const3 = lambda i, j: (0, 0, 0)
    return pl.pallas_call(
        _peer_kernel,
        grid=(nblk, PEER_NCHUNK),
        in_specs=[
            pl.BlockSpec((PEER_TM, d), lambda i, j: (i, 0)),
            pl.BlockSpec((PEER_CHUNK, d), lambda i, j: (j, 0)),
            pl.BlockSpec((PEER_CHUNK, d), lambda i, j: (j, 0)),
            pl.BlockSpec((PEER_LC, N_PICKS, LANES), const3),
            pl.BlockSpec((PEER_LC, N_PICKS, LANES), const3),
            pl.BlockSpec((PEER_TM, d), lambda i, j: (i, 0)),
            pl.BlockSpec((PEER_TM, 2 * PEER_HALF),
                         lambda i, j: (jnp.minimum(i + 1, nblk - 1), j // steps_per_head)),
            pl.BlockSpec((1, 2, N_SUB_KEYS, PEER_HALF), lambda i, j: (j // steps_per_head, 0, 0, 0)),
            pl.BlockSpec((N_SUB_KEYS, LANES), lambda i, j: (0, 0)),
            pl.BlockSpec((CAND_ROWS, LANES), lambda i, j: (0, 0)),
        ],
        out_specs=pl.BlockSpec((PEER_TM, d), lambda i, j: (i, 0)),
        out_shape=jax.ShapeDtypeStruct((t, d), F32),
        scratch_shapes=[
            pltpu.VMEM((PEER_NCHUNK, PEER_TM, PEER_CHUNK), BF16),
            pltpu.VMEM((PEER_TM, N_PICKS), jnp.int32),
            pltpu.VMEM((PEER_TM, N_PICKS), F32),
            pltpu.VMEM((N_SUB_KEYS // SUBLANES, PEER_TG * SUBLANES, N_SUB_KEYS), F32),
            pltpu.VMEM((N_SUB_KEYS // SUBLANES, PEER_TG * SUBLANES, N_SUB_KEYS), F32),
            pltpu.VMEM((2, PEER_LC, N_PICKS, LANES), jnp.int32),
            pltpu.VMEM((2, PEER_LC, N_PICKS, LANES), F32),
        ] + _topk_scratch(PEER_LC_STEP),
        compiler_params=pltpu.CompilerParams(
            dimension_semantics=("arbitrary", "arbitrary"), vmem_limit_bytes=VMEM_LIMIT),
        name="peer",
    )(xn2, u_tab, v_tab, e0, g0, h1, qp, sub_keys, ids, pos)


def _s5_discretise_kernel(lr_ref, li_ref, ls_ref, br_ref, bi_ref, ar_ref, ai_ref, bbr_ref, bbi_ref):
    lr, li = lr_ref[...], li_ref[...]
    step = jnp.exp(ls_ref[...])
    mag = jnp.exp(lr * step)
    abar_re = mag * jnp.cos(li * step)
    abar_im = mag * jnp.sin(li * step)
    den = lr * lr + li * li
    nr, ni = abar_re - 1.0, abar_im
    coef_re = (nr * lr + ni * li) / den
    coef_im = (ni * lr - nr * li) / den
    ar_ref[...] = abar_re
    ai_ref[...] = abar_im
    br, bi = br_ref[...], bi_ref[...]
    bbr_ref[...] = coef_re * br - coef_im * bi
    bbi_ref[...] = coef_re * bi + coef_im * br


def _s5_params(lam_re, lam_im, log_step, b_re, b_im, c_re, c_im, bsz):
    col = lambda m: m.reshape(S5_LANES, 1)
    pole = jax.ShapeDtypeStruct((S5_LANES, 1), F32)
    pole_in = jax.ShapeDtypeStruct((S5_LANES, S5_GROUP), F32)
    abar_re, abar_im, bbar_re, bbar_im = pl.pallas_call(
        _s5_discretise_kernel, out_shape=[pole, pole, pole_in, pole_in], name="s5_discretise",
    )(col(lam_re), col(lam_im), col(jnp.repeat(log_step, S5_STATE)),
      b_re.reshape(S5_LANES, S5_GROUP), b_im.reshape(S5_LANES, S5_GROUP))
    nlc = S5_LANES // S5_LC
    gpc = S5_GROUPS // nlc
    eye = jnp.eye(gpc, dtype=F32)
    to_b = lambda m: jnp.einsum('lgpc,gh->lgchp', m.reshape(nlc, gpc, S5_STATE, S5_GROUP),
                                eye).reshape(nlc, S5_CH, S5_LC)
    to_c = lambda m: jnp.einsum('lgcp,gh->lgphc', m.reshape(nlc, gpc, S5_GROUP, S5_STATE),
                                eye).reshape(nlc, S5_LC, S5_CH)
    bblk = jnp.concatenate([to_b(bbar_re), to_b(bbar_im)], axis=2).astype(BF16)
    cblk = jnp.concatenate([to_c(c_re), -to_c(c_im)], axis=1).astype(BF16)
    a = jnp.stack([abar_re.reshape(-1), abar_im.reshape(-1)])
    a8 = jnp.broadcast_to(a[:, None, :], (2, bsz, S5_LANES))
    return a8, bblk, cblk


def kernel(x, ln1_g, w_in, b_gate, s5_lambda_re, s5_lambda_im, s5_log_step, s5_b_re, s5_b_im, s5_c_re, s5_c_im, s5_d, s5_w_glu, w_s5_branch, q_norm_g, k_norm_g, rel_bias_table, attn_sinks, w_attn_branch, w_out, ln2_g, peer_w_query, peer_sub_keys, peer_u, peer_v):
    bsz, seq, d = x.shape
    h = x
    for layer in range(ln1_g.shape[0]):
        w_uqkv = w_in[layer, :, :UQKV_WIDTH].astype(BF16)
        w_gate = w_in[layer, :, UQKV_WIDTH:].astype(BF16)
        u_tm, q, k, v = _in_proj(h, ln1_g[layer][None], w_uqkv)

        a8, bblk, cblk = _s5_params(s5_lambda_re[layer], s5_lambda_im[layer], s5_log_step[layer],
                                    s5_b_re[layer], s5_b_im[layer], s5_c_re[layer], s5_c_im[layer], bsz)
        ys5 = _s5(u_tm, a8, bblk, cblk, s5_d[layer].reshape(1, S5_WIDTH),
                  s5_w_glu[layer].astype(BF16), w_s5_branch[layer].astype(BF16), bsz)

        attn = _attention(q, k, v, q_norm_g[layer][None], k_norm_g[layer][None],
                          _t5_bucket_matrix(), rel_bias_table, attn_sinks[layer])

        h1, xn2, qp, u_bf, v_bf = _merge(
            h, attn, ys5, ln1_g[layer][None], w_gate, b_gate[layer][None],
            w_attn_branch[layer].astype(BF16), w_out[layer].astype(BF16), ln2_g[layer][None],
            peer_w_query[layer].astype(BF16), peer_u[layer], peer_v[layer])

        sub_keys = peer_sub_keys[layer].astype(BF16)
        pos = jnp.asarray(_CAND_POS)
        ids = jnp.asarray(_KEY_IDS)
        e0, g0 = _topk(qp, sub_keys, ids, pos)
        out = _peer(xn2, u_bf, v_bf, e0, g0, h1, qp, sub_keys, ids, pos)
        h = out.reshape(bsz, seq, d)
    return h
```

```python
import functools
import math

import jax
import jax.numpy as jnp
import numpy as np
from jax import lax
from jax.experimental import pallas as pl
from jax.experimental.pallas import tpu as pltpu

F32 = jnp.float32
BF16 = jnp.bfloat16

D_MODEL = 1024
RMS_EPS = 1e-6
NEG_INF = -1e30

S5_WIDTH = 512
S5_GROUP = 16
S5_GROUPS = 32
S5_STATE = 64
S5_LANES = S5_GROUPS * S5_STATE

N_Q_HEADS = 16
N_KV_HEADS = 4
HEAD_DIM = 64
Q_PER_KV = 4
ATTN_BLOCK = 128
N_BUCKETS = 32
MAX_DISTANCE = 128
Q_WIDTH = 1024
KV_WIDTH = 256
UQKV_WIDTH = S5_WIDTH + Q_WIDTH + 2 * KV_WIDTH

PEER_HEADS = 8
PEER_HALF = 64
N_SUB_KEYS = 128
SUB_KEY_BITS = N_SUB_KEYS.bit_length() - 1
PEER_TOPK = 16
N_PICKS = PEER_HEADS * PEER_TOPK

V7X_VMEM_BYTES = 64 * 1024 * 1024
VMEM_LIMIT = V7X_VMEM_BYTES * 7 // 8

SQRT_HALF = 0.7071067811865476


def _rms(x, g):
    r = lax.rsqrt(jnp.mean(x * x, axis=-1, keepdims=True) + RMS_EPS)
    return (x * r) * g


def _gelu(x):
    return 0.5 * x * (1.0 + lax.erf(x * SQRT_HALF))


def _dot(a, b):
    return jnp.dot(a, b, preferred_element_type=F32)


def _dot_nt(a, b):
    return lax.dot_general(a, b, (((1,), (1,)), ((), ())), preferred_element_type=F32)


S5_TC = 128


def _in_proj_kernel(bsz, x_ref, g_ref, w_ref, u_ref, q_ref, k_ref, v_ref):
    d = x_ref.shape[-1]
    xn = _rms(x_ref[...].reshape(bsz * S5_TC, d), g_ref[...]).astype(BF16)
    p = _dot(xn, w_ref[...])
    u = p[:, :S5_WIDTH].reshape(bsz, S5_TC, S5_WIDTH)
    u_ref[...] = jnp.transpose(u, (1, 0, 2)).reshape(S5_TC * bsz, S5_WIDTH)
    q_ref[...] = p[:, S5_WIDTH:S5_WIDTH + Q_WIDTH].astype(BF16).reshape(bsz, S5_TC, Q_WIDTH)
    k_ref[...] = p[:, S5_WIDTH + Q_WIDTH:S5_WIDTH + Q_WIDTH + KV_WIDTH].astype(BF16).reshape(
        bsz, S5_TC, KV_WIDTH)
    v_ref[...] = p[:, S5_WIDTH + Q_WIDTH + KV_WIDTH:].astype(BF16).reshape(bsz, S5_TC, KV_WIDTH)


def _in_proj(x, ln1_g, w_uqkv):
    bsz, seq, d = x.shape
    return pl.pallas_call(
        functools.partial(_in_proj_kernel, bsz),
        grid=(seq // S5_TC,),
        in_specs=[
            pl.BlockSpec((bsz, S5_TC, d), lambda i: (0, i, 0)),
            pl.BlockSpec((1, d), lambda i: (0, 0)),
            pl.BlockSpec((d, UQKV_WIDTH), lambda i: (0, 0)),
        ],
        out_specs=[
            pl.BlockSpec((S5_TC * bsz, S5_WIDTH), lambda i: (i, 0)),
            pl.BlockSpec((bsz, S5_TC, Q_WIDTH), lambda i: (0, i, 0)),
            pl.BlockSpec((bsz, S5_TC, KV_WIDTH), lambda i: (0, i, 0)),
            pl.BlockSpec((bsz, S5_TC, KV_WIDTH), lambda i: (0, i, 0)),
        ],
        out_shape=[
            jax.ShapeDtypeStruct((seq * bsz, S5_WIDTH), F32),
            jax.ShapeDtypeStruct((bsz, seq, Q_WIDTH), BF16),
            jax.ShapeDtypeStruct((bsz, seq, KV_WIDTH), BF16),
            jax.ShapeDtypeStruct((bsz, seq, KV_WIDTH), BF16),
        ],
        compiler_params=pltpu.CompilerParams(
            dimension_semantics=("arbitrary",), vmem_limit_bytes=VMEM_LIMIT),
        name="in_proj",
    )(x, ln1_g, w_uqkv)


S5_LC = 512
S5_CH = S5_LC // S5_STATE * S5_GROUP


def _s5_kernel(bsz, u_ref, a_ref, bblk_ref, cblk_ref, d_ref, wglu_ref, wbr_ref, o_ref,
               h_ref, bu_ref):
    @pl.when(pl.program_id(0) == 0)
    def _():
        h_ref[...] = jnp.zeros_like(h_ref)

    u = u_ref[...]
    ub = u.astype(BF16)
    for lc in range(S5_LANES // S5_LC):
        ch = slice(lc * S5_CH, (lc + 1) * S5_CH)
        for part in range(2):
            cols = pl.ds(part * S5_LANES + lc * S5_LC, S5_LC)
            bu_ref[:, cols] = _dot(ub[:, ch], bblk_ref[lc, :, part * S5_LC:(part + 1) * S5_LC])

    for lc in range(S5_LANES // S5_LC):
        re = pl.ds(lc * S5_LC, S5_LC)
        im = pl.ds(S5_LANES + lc * S5_LC, S5_LC)
        ar = a_ref[0, :, re]
        ai = a_ref[1, :, re]

        hr, hi = h_ref[0, :, re], h_ref[1, :, re]
        for t in range(S5_TC):
            rows = pl.ds(t * bsz, bsz)
            nr = ar * hr - ai * hi + bu_ref[rows, re]
            ni = ar * hi + ai * hr + bu_ref[rows, im]
            bu_ref[rows, re] = nr
            bu_ref[rows, im] = ni
            hr, hi = nr, ni
        h_ref[0, :, re] = hr
        h_ref[1, :, re] = hi

    ys = []
    for lc in range(S5_LANES // S5_LC):
        ch = slice(lc * S5_CH, (lc + 1) * S5_CH)
        re = pl.ds(lc * S5_LC, S5_LC)
        im = pl.ds(S5_LANES + lc * S5_LC, S5_LC)
        ys.append(_dot(bu_ref[:, re].astype(BF16), cblk_ref[lc, :S5_LC, :])
                  + _dot(bu_ref[:, im].astype(BF16), cblk_ref[lc, S5_LC:, :]))
    y = jnp.concatenate(ys, axis=1) + d_ref[...] * u
    yg = _gelu(y).astype(BF16)
    ab = _dot(yg, wglu_ref[...])
    glu = ab[:, :S5_WIDTH] * jax.nn.sigmoid(ab[:, S5_WIDTH:])
    glu = jnp.transpose(glu.reshape(S5_TC, bsz, S5_WIDTH), (1, 0, 2)).reshape(bsz * S5_TC, S5_WIDTH)
    o_ref[...] = _dot(glu.astype(BF16), wbr_ref[...]).astype(BF16).reshape(bsz, S5_TC, D_MODEL)


def _s5(u_tm, a8, bblk, cblk, d_row, w_glu, w_br, bsz):
    rows = u_tm.shape[0]
    tr = S5_TC * bsz
    const = lambda shape: pl.BlockSpec(shape, lambda i: (0,) * len(shape))
    return pl.pallas_call(
        functools.partial(_s5_kernel, bsz),
        grid=(rows // tr,),
        in_specs=[
            pl.BlockSpec((tr, S5_WIDTH), lambda i: (i, 0)),
            const((2, bsz, S5_LANES)),
            const((S5_LANES // S5_LC, S5_CH, 2 * S5_LC)),
            const((S5_LANES // S5_LC, 2 * S5_LC, S5_CH)),
            const((1, S5_WIDTH)),
            const((S5_WIDTH, 2 * S5_WIDTH)),
            const((S5_WIDTH, D_MODEL)),
        ],
        out_specs=pl.BlockSpec((bsz, S5_TC, D_MODEL), lambda i: (0, i, 0)),
        out_shape=jax.ShapeDtypeStruct((bsz, rows // bsz, D_MODEL), BF16),
        scratch_shapes=[
            pltpu.VMEM((2, bsz, S5_LANES), F32),
            pltpu.VMEM((tr, 2 * S5_LANES), F32),
        ],
        compiler_params=pltpu.CompilerParams(
            dimension_semantics=("arbitrary",), vmem_limit_bytes=VMEM_LIMIT),
        name="s5",
    )(u_tm, a8, bblk, cblk, d_row, w_glu, w_br)


def _attn_kernel(q_ref, kc_ref, kp_ref, vc_ref, vp_ref, qg_ref, kg_ref, bucket_ref,
                 table_ref, sink_ref, o_ref, bias_ref):
    first = jnp.logical_and(pl.program_id(0) == 0, pl.program_id(1) == 0)

    @pl.when(first)
    def _():
        bucket = bucket_ref[...]
        qi = lax.broadcasted_iota(jnp.int32, (ATTN_BLOCK, 2 * ATTN_BLOCK), 0)
        si = lax.broadcasted_iota(jnp.int32, (ATTN_BLOCK, 2 * ATTN_BLOCK), 1)
        dist = ATTN_BLOCK + qi - si
        band = (dist >= 0) & (dist < ATTN_BLOCK)
        for h in range(N_Q_HEADS):
            acc = jnp.zeros((ATTN_BLOCK, 2 * ATTN_BLOCK), F32)
            for b in range(N_BUCKETS):
                acc = jnp.where(bucket == b, table_ref[b, h], acc)
            bias_ref[0, h] = jnp.where(band, acc, NEG_INF)
            bias_ref[1, h] = jnp.where(band & (si >= ATTN_BLOCK), acc, NEG_INF)

    blk = pl.program_id(1)
    qg = qg_ref[...] * HEAD_DIM ** -0.5
    kg = kg_ref[...]

    grp_rows = Q_PER_KV * ATTN_BLOCK
    row_head = lax.broadcasted_iota(jnp.int32, (grp_rows, 1), 0) // ATTN_BLOCK

    ones_d = jnp.ones((HEAD_DIM, LANES), BF16)
    ones_s = jnp.ones((2 * ATTN_BLOCK, LANES), BF16)

    def rms_rows(x, g):
        ssq = _dot((x * x).astype(BF16), ones_d)[:, :HEAD_DIM]
        return (x * lax.rsqrt(ssq * (1.0 / HEAD_DIM) + RMS_EPS)) * g

    groups = [(sb, kh) for sb in range(ATTN_QB) for kh in range(N_KV_HEADS)]
    knorm, vall = [], []
    for kh in range(N_KV_HEADS):
        cols = slice(kh * HEAD_DIM, (kh + 1) * HEAD_DIM)
        kall = jnp.concatenate([kp_ref[0, :, cols], kc_ref[0, :, cols]], axis=0).astype(F32)
        knorm.append(rms_rows(kall, kg).astype(BF16))
        vall.append(jnp.concatenate([vp_ref[0, :, cols], vc_ref[0, :, cols]], axis=0))
    qss = []
    for sb, kh in groups:
        h0 = kh * Q_PER_KV
        rows = slice(sb * ATTN_BLOCK, (sb + 1) * ATTN_BLOCK)
        qs = jnp.concatenate(
            [q_ref[0, rows, (h0 + g) * HEAD_DIM:(h0 + g + 1) * HEAD_DIM] for g in range(Q_PER_KV)],
            axis=0).astype(F32)
        qss.append(rms_rows(qs, qg).astype(BF16))
    ss, sinks = [], []
    for gi, (sb, kh) in enumerate(groups):
        h0 = kh * Q_PER_KV
        keys = slice(sb * ATTN_BLOCK, (sb + 2) * ATTN_BLOCK)
        s = _dot_nt(qss[gi], knorm[kh][keys]).reshape(Q_PER_KV, ATTN_BLOCK, 2 * ATTN_BLOCK)
        variant = jnp.where(blk > 0, 0, 1) if sb == 0 else 0
        ss.append((s + bias_ref[variant, h0:h0 + Q_PER_KV]).reshape(grp_rows, 2 * ATTN_BLOCK))
        sink = jnp.full((grp_rows, 1), sink_ref[h0], F32)
        for g in range(1, Q_PER_KV):
            sink = jnp.where(row_head == g, sink_ref[h0 + g], sink)
        sinks.append(sink)
    ps, dens = [], []
    for gi in range(len(groups)):
        m = jnp.maximum(jnp.max(ss[gi], axis=-1, keepdims=True), sinks[gi])
        p = jnp.exp(ss[gi] - m).astype(BF16)
        ps.append(p)
        dens.append(_dot(p, ones_s)[:, :HEAD_DIM] + jnp.exp(sinks[gi] - m))
    for gi, (sb, kh) in enumerate(groups):
        h0 = kh * Q_PER_KV
        keys = slice(sb * ATTN_BLOCK, (sb + 2) * ATTN_BLOCK)
        o = (_dot(ps[gi], vall[kh][keys]) / dens[gi]).astype(BF16)
        for g in range(Q_PER_KV):
            o_ref[0, sb * ATTN_BLOCK:(sb + 1) * ATTN_BLOCK, (h0 + g) * HEAD_DIM:(h0 + g + 1) * HEAD_DIM] = (
                o[g * ATTN_BLOCK:(g + 1) * ATTN_BLOCK])


ATTN_QB = 2


def _attention(q, k, v, qg, kg, bucket, table, sinks):
    bsz, seq, _ = q.shape
    rows = ATTN_QB * ATTN_BLOCK
    cur = lambda b, i: (b, i, 0)
    prev = lambda b, i: (b, jnp.maximum(i * ATTN_QB - 1, 0), 0)
    const2 = lambda b, i: (0, 0)
    return pl.pallas_call(
        _attn_kernel,
        grid=(bsz, seq // rows),
        in_specs=[
            pl.BlockSpec((1, rows, Q_WIDTH), cur),
            pl.BlockSpec((1, rows, KV_WIDTH), cur),
            pl.BlockSpec((1, ATTN_BLOCK, KV_WIDTH), prev),
            pl.BlockSpec((1, rows, KV_WIDTH), cur),
            pl.BlockSpec((1, ATTN_BLOCK, KV_WIDTH), prev),
            pl.BlockSpec((1, HEAD_DIM), const2),
            pl.BlockSpec((1, HEAD_DIM), const2),
            pl.BlockSpec((ATTN_BLOCK, 2 * ATTN_BLOCK), const2),
            pl.BlockSpec(memory_space=pltpu.SMEM),
            pl.BlockSpec(memory_space=pltpu.SMEM),
        ],
        out_specs=pl.BlockSpec((1, rows, Q_WIDTH), cur),
        out_shape=jax.ShapeDtypeStruct((bsz, seq, Q_WIDTH), BF16),
        scratch_shapes=[pltpu.VMEM((2, N_Q_HEADS, ATTN_BLOCK, 2 * ATTN_BLOCK), F32)],
        compiler_params=pltpu.CompilerParams(
            dimension_semantics=("arbitrary", "arbitrary"), vmem_limit_bytes=VMEM_LIMIT),
        name="attn",
    )(q, k, k, v, v, qg, kg, bucket, table, sinks)


def _t5_bucket_matrix():
    qi = np.arange(ATTN_BLOCK)[:, None]
    si = np.arange(2 * ATTN_BLOCK)[None, :]
    dist = np.maximum(ATTN_BLOCK + qi - si, 0)
    max_exact = N_BUCKETS // 2
    d_f = np.maximum(dist, 1).astype(np.float64)
    large = max_exact + np.floor(np.log(d_f / max_exact) / math.log(MAX_DISTANCE / max_exact)
                                 * (N_BUCKETS - max_exact)).astype(np.int32)
    large = np.minimum(large, N_BUCKETS - 1)
    return jnp.asarray(np.where(dist < max_exact, dist, large).astype(np.int32))


MERGE_TM = 512


def _merge_kernel(x_ref, attn_ref, ys5_ref, g1_ref, wg_ref, bg_ref, wab_ref, wout_ref,
                  g2_ref, wpq_ref, u_ref, v_ref, h1_ref, xn2_ref, qp_ref, ub_ref, vb_ref):
    ub_ref[...] = u_ref[...].astype(BF16)
    vb_ref[...] = v_ref[...].astype(BF16)
    x = x_ref[0]
    xn = _rms(x, g1_ref[...]).astype(BF16)
    gates = jax.nn.sigmoid(_dot(xn, wg_ref[...]) + bg_ref[...])
    y_attn = _dot(attn_ref[0], wab_ref[...])
    y_s5 = ys5_ref[0].astype(F32)
    mix = gates[:, :D_MODEL] * y_s5 + gates[:, D_MODEL:] * y_attn
    h1 = x + _dot(mix.astype(BF16), wout_ref[...])
    h1_ref[...] = h1
    xn2 = _rms(h1, g2_ref[...]).astype(BF16)
    xn2_ref[...] = xn2
    qp_ref[...] = _dot(xn2, wpq_ref[...]).astype(BF16)


def _merge(x, attn, ys5_tm, ln1_g, w_gate, b_gate, w_ab, w_out, ln2_g, w_pq, u_tab, v_tab):
    bsz, seq, d = x.shape
    nt = seq // MERGE_TM
    tok = lambda b, i: (b * nt + i, 0)
    const2 = lambda b, i: (0, 0)
    n_experts = u_tab.shape[0]
    tab_rows = n_experts // (bsz * nt)
    assert tab_rows * bsz * nt == n_experts and v_tab.shape == u_tab.shape == (n_experts, d)
    return pl.pallas_call(
        _merge_kernel,
        grid=(bsz, nt),
        in_specs=[
            pl.BlockSpec((1, MERGE_TM, d), lambda b, i: (b, i, 0)),
            pl.BlockSpec((1, MERGE_TM, d), lambda b, i: (b, i, 0)),
            pl.BlockSpec((1, MERGE_TM, d), lambda b, i: (b, i, 0)),
            pl.BlockSpec((1, d), const2),
            pl.BlockSpec((d, 2 * d), const2),
            pl.BlockSpec((1, 2 * d), const2),
            pl.BlockSpec((d, d), const2),
            pl.BlockSpec((d, d), const2),
            pl.BlockSpec((1, d), const2),
            pl.BlockSpec((d, d), const2),
            pl.BlockSpec((tab_rows, d), tok),
            pl.BlockSpec((tab_rows, d), tok),
        ],
        out_specs=[
            pl.BlockSpec((MERGE_TM, d), tok),
            pl.BlockSpec((MERGE_TM, d), tok),
            pl.BlockSpec((MERGE_TM, d), tok),
            pl.BlockSpec((tab_rows, d), tok),
            pl.BlockSpec((tab_rows, d), tok),
        ],
        out_shape=[
            jax.ShapeDtypeStruct((bsz * seq, d), F32),
            jax.ShapeDtypeStruct((bsz * seq, d), BF16),
            jax.ShapeDtypeStruct((bsz * seq, d), BF16),
            jax.ShapeDtypeStruct((n_experts, d), BF16),
            jax.ShapeDtypeStruct((n_experts, d), BF16),
        ],
        compiler_params=pltpu.CompilerParams(
            dimension_semantics=("arbitrary", "arbitrary"), vmem_limit_bytes=VMEM_LIMIT),
        name="merge",
    )(x, attn, ys5_tm, ln1_g, w_gate, b_gate, w_ab, w_out, ln2_g, w_pq, u_tab, v_tab)


TOPK_TK = 512
LANES = 128
SUBLANES = 8
CAND_ROWS = 80
POS_INVALID = 1.0e9


def _cand_layout():
    pos = np.full((CAND_ROWS,), POS_INVALID, np.float32)
    blocks = [(0, 0), (0, 8), (1, 0), (2, 0), (3, 0), (4, 0), (5, 0), (6, 0), (7, 0)]
    for r, (a, b0) in enumerate(blocks):
        for j in range(8):
            b = b0 + j
            if (a + 1) * (b + 1) <= PEER_TOPK:
                pos[r * 8 + j] = a * PEER_TOPK + b
    for j in range(8):
        pos[72 + j] = (8 + j) * PEER_TOPK
    return blocks, np.broadcast_to(pos[:, None], (CAND_ROWS, LANES)).copy()


_CAND_BLOCKS, _CAND_POS = _cand_layout()
_KEY_IDS = np.broadcast_to(np.arange(N_SUB_KEYS, dtype=np.float32)[:, None], (N_SUB_KEYS, LANES)).copy()


def _extract_round(s, ids, big):
    m = jnp.max(s, axis=0, keepdims=True)
    sel = jnp.min(jnp.where(s == m, ids, big), axis=0, keepdims=True)
    return m, sel, ids == sel


def _topk_stages(sk_ref, chunks, ids_ref, pos_ref, scratch):
    v1_ref, i1_ref, v2_ref, i2_ref, best_ref, s_ref, cand_ref, expert_ref, picked_ref = scratch
    nkeys = float(N_SUB_KEYS)
    tile = slice(0, SUBLANES)
    for qp, lanes, _ in chunks:
        s_ref[0, :, lanes] = _dot_nt(sk_ref[0, 0], qp[:, :PEER_HALF])
        s_ref[1, :, lanes] = _dot_nt(sk_ref[0, 1], qp[:, PEER_HALF:])
    live = [(s_ref, (half, tile, lanes)) for _, lanes, _ in chunks for half in range(2)]
    yield live
    for j in range(PEER_TOPK):
        for _, lanes, _ in chunks:
            for half, (val_ref, id_ref) in enumerate(((v1_ref, i1_ref), (v2_ref, i2_ref))):
                s = s_ref[half, :, lanes]
                m, sel, hit = _extract_round(s, ids_ref[...], nkeys)
                val_ref[j:j + 1, lanes] = m
                id_ref[j:j + 1, lanes] = sel
                s_ref[half, :, lanes] = jnp.where(hit, -jnp.inf, s)
        yield live

    for _, lanes, _ in chunks:
        for r, (a, b0) in enumerate(_CAND_BLOCKS):
            rows = slice(r * SUBLANES, (r + 1) * SUBLANES)
            cand_ref[rows, lanes] = v1_ref[a:a + 1, lanes] + v2_ref[b0:b0 + 8, lanes]
            expert_ref[rows, lanes] = i1_ref[a:a + 1, lanes] * nkeys + i2_ref[b0:b0 + 8, lanes]
        rows = slice(CAND_ROWS - SUBLANES, CAND_ROWS)
        cand_ref[rows, lanes] = v1_ref[8:16, lanes] + v2_ref[0:1, lanes]
        expert_ref[rows, lanes] = i1_ref[8:16, lanes] * nkeys + i2_ref[0:1, lanes]
        cand_ref[:, lanes] = jnp.where(pos_ref[...] < POS_INVALID, cand_ref[:, lanes], -jnp.inf)
    live = [(cand_ref, (tile, lanes)) for _, lanes, _ in chunks]
    yield live
    for j in range(PEER_TOPK):
        for _, lanes, _ in chunks:
            cand = cand_ref[:, lanes]
            m, _, hit = _extract_round(cand, pos_ref[...], POS_INVALID)
            best_ref[j:j + 1, lanes] = m
            picked_ref[j:j + 1, lanes] = jnp.sum(jnp.where(hit, expert_ref[:, lanes], 0.0), axis=0,
                                                 keepdims=True)
            cand_ref[:, lanes] = jnp.where(hit, -jnp.inf, cand)
        if j % 2 == 1:
            yield live
    for _, lanes, store in chunks:
        best = best_ref[:, lanes]
        p = jnp.exp(best - best_ref[0:1, lanes])
        store(picked_ref[:, lanes].astype(jnp.int32), p / jnp.sum(p, axis=0, keepdims=True))
    yield []


def _order_after(live, value_tile):
    zero = jnp.minimum(jnp.abs(value_tile), 0.0)
    for ref, idx in live:
        ref[idx] = ref[idx] + zero


def _topk_scratch(n_chunks):
    w = n_chunks * LANES
    return ([pltpu.VMEM((PEER_TOPK, w), F32) for _ in range(5)]
            + [pltpu.VMEM((2, N_SUB_KEYS, w), F32), pltpu.VMEM((CAND_ROWS, w), F32),
               pltpu.VMEM((CAND_ROWS, w), F32), pltpu.VMEM((PEER_TOPK, w), F32)])


def _topk_kernel(qp_ref, sk_ref, ids_ref, pos_ref, e_ref, g_ref, *scratch):
    def store(c):
        def fn(e, g):
            e_ref[c] = e
            g_ref[c] = g
        return fn

    chunks = [(qp_ref[pl.ds(c * LANES, LANES), :], pl.ds(c * LANES, LANES), store(c))
              for c in range(TOPK_TK // LANES)]
    for _ in _topk_stages(sk_ref, chunks, ids_ref, pos_ref, scratch):
        pass


def _topk(qp, sub_keys, ids, pos):
    n_lc = TOPK_TK // LANES
    return pl.pallas_call(
        _topk_kernel,
        grid=(1, PEER_HEADS),
        in_specs=[
            pl.BlockSpec((TOPK_TK, 2 * PEER_HALF), lambda i, h: (i, h)),
            pl.BlockSpec((1, 2, N_SUB_KEYS, PEER_HALF), lambda i, h: (h, 0, 0, 0)),
            pl.BlockSpec((N_SUB_KEYS, LANES), lambda i, h: (0, 0)),
            pl.BlockSpec((CAND_ROWS, LANES), lambda i, h: (0, 0)),
        ],
        out_specs=[
            pl.BlockSpec((n_lc, PEER_TOPK, LANES), lambda i, h: (0, h, 0)),
            pl.BlockSpec((n_lc, PEER_TOPK, LANES), lambda i, h: (0, h, 0)),
        ],
        out_shape=[
            jax.ShapeDtypeStruct((n_lc, N_PICKS, LANES), jnp.int32),
            jax.ShapeDtypeStruct((n_lc, N_PICKS, LANES), F32),
        ],
        scratch_shapes=_topk_scratch(n_lc),
        compiler_params=pltpu.CompilerParams(
            dimension_semantics=("arbitrary", "arbitrary"), vmem_limit_bytes=VMEM_LIMIT),
        name="topk",
    )(qp, sub_keys, ids, pos)


PEER_TM = 512
PEER_CHUNK = 2048
PEER_SLABS = PEER_CHUNK // N_SUB_KEYS
PEER_NCHUNK = (N_SUB_KEYS * N_SUB_KEYS) // PEER_CHUNK
PEER_TG = 16


PEER_SUB = 4
PEER_SELECT_LEAD = 2
PEER_SELECT_PER_STAGE = (4,) + (3,) * (2 * PEER_SUB - 1)
PEER_LC = PEER_TM // LANES
PEER_LC_STEP = PEER_LC * PEER_HEADS // PEER_NCHUNK
assert PEER_NCHUNK * PEER_LC_STEP == PEER_LC * PEER_HEADS and PEER_TM == TOPK_TK


def _peer_kernel(xn_ref, u_ref, v_ref, e0_ref, g0_ref, h1_ref, qpn_ref, sk_ref, ids_ref, pos_ref,
                 o_ref, hg_ref, et_ref, gt_ref, wa_ref, wb_ref, ebuf_ref, gbuf_ref, *topk_scratch):
    i = pl.program_id(0)
    j = pl.program_id(1)
    cur = i % 2

    @pl.when(jnp.logical_and(i == 0, j == 0))
    def _():
        ebuf_ref[0] = e0_ref[...]
        gbuf_ref[0] = g0_ref[...]

    @pl.when(j == 0)
    def _():
        for c in range(PEER_LC):
            et_ref[c * LANES:(c + 1) * LANES, :] = ebuf_ref[cur, c].T
            gt_ref[c * LANES:(c + 1) * LANES, :] = gbuf_ref[cur, c].T
        sub = lax.broadcasted_iota(jnp.int32, (N_SUB_KEYS, N_PICKS), 0)

        def scatter(gi, w_ref):
            t0 = gi * PEER_TG if isinstance(gi, int) else pl.multiple_of(gi * PEER_TG, PEER_TG)
            for tt in range(PEER_TG):
                e_row = et_ref[pl.ds(t0 + tt, 1), :]
                g_row = gt_ref[pl.ds(t0 + tt, 1), :]
                first = jnp.where(sub == (e_row >> SUB_KEY_BITS), g_row, 0.0).astype(BF16)
                second = jnp.where(sub == (e_row & (N_SUB_KEYS - 1)), 1.0, 0.0).astype(BF16)
                w = _dot_nt(first, second)
                for v in range(N_SUB_KEYS // SUBLANES):
                    w_ref[v, pl.ds(tt * SUBLANES, SUBLANES), :] = w[v * SUBLANES:(v + 1) * SUBLANES, :]

        def relayout(gi, w_ref):
            t0 = gi * PEER_TG if isinstance(gi, int) else pl.multiple_of(gi * PEER_TG, PEER_TG)
            for i1 in range(N_SUB_KEYS):
                v, r = divmod(i1, SUBLANES)
                rows = w_ref[v, pl.ds(r, PEER_TG, stride=SUBLANES), :]
                c, s = divmod(i1, PEER_SLABS)
                hg_ref[c, pl.ds(t0, PEER_TG), s * N_SUB_KEYS:(s + 1) * N_SUB_KEYS] = rows.astype(BF16)

        def pair(k, carry):
            scatter(2 * k + 1, wb_ref)
            relayout(2 * k, wa_ref)
            scatter(2 * k + 2, wa_ref)
            relayout(2 * k + 1, wb_ref)
            return carry

        n_groups = PEER_TM // PEER_TG
        scatter(0, wa_ref)
        lax.fori_loop(0, n_groups // 2 - 1, pair, 0)
        scatter(n_groups - 1, wb_ref)
        relayout(n_groups - 2, wa_ref)
        relayout(n_groups - 1, wb_ref)
        o_ref[...] = h1_ref[...]

    def expert_stages():
        xn = xn_ref[...]
        sub = PEER_CHUNK // PEER_SUB
        hs = []
        for q in range(PEER_SUB):
            es = slice(q * sub, (q + 1) * sub)
            a = _dot_nt(xn, u_ref[es, :])
            hq = _gelu(a) * hg_ref[j, :, es].astype(F32)
            hs.append(hq.astype(BF16))
            yield hq[:SUBLANES, :LANES] + hq[-SUBLANES:, -LANES:]
        for q in range(PEER_SUB):
            es = slice(q * sub, (q + 1) * sub)
            r = _dot(hs[q], v_ref[es, :])
            o_ref[...] += r
            yield r[:SUBLANES, :LANES]

    head_rows = pl.ds(pl.multiple_of((j * PEER_LC_STEP // PEER_LC) * PEER_TOPK, PEER_TOPK), PEER_TOPK)

    def store(lc):
        def fn(e, g):
            ebuf_ref[1 - cur, lc, head_rows, :] = e
            gbuf_ref[1 - cur, lc, head_rows, :] = g
        return fn

    chunks = []
    for c in range(PEER_LC_STEP):
        lc = (j * PEER_LC_STEP) % PEER_LC + c
        qp = qpn_ref[pl.ds(pl.multiple_of(lc * LANES, LANES), LANES), :]
        chunks.append((qp, pl.ds(c * LANES, LANES), store(lc)))
    select = _topk_stages(sk_ref, chunks, ids_ref, pos_ref, topk_scratch)
    experts = expert_stages()
    live = []
    for _ in range(PEER_SELECT_LEAD):
        live = next(select, [])
    for n_stages, result_tile in zip(PEER_SELECT_PER_STAGE, experts, strict=True):
        _order_after(live, result_tile)
        for _ in range(n_stages):
            live = next(select, [])
    for _ in select:
        pass


def _peer(xn2, u_tab, v_tab, e0, g0, h1, qp, sub_keys, ids, pos):
    t, d = xn2.shape
    nblk = t // PEER_TM
    steps_per_head = PEER_LC // PEER_LC_STEP
    const3 = lambda i, j: (0, 0, 0)
    return pl.pallas_call(
        _peer_kernel,
        grid=(nblk, PEER_NCHUNK),
        in_specs=[
            pl.BlockSpec((PEER_TM, d), lambda i, j: (i, 0)),
            pl.BlockSpec((PEER_CHUNK, d), lambda i, j: (j, 0)),
            pl.BlockSpec((PEER_CHUNK, d), lambda i, j: (j, 0)),
            pl.BlockSpec((PEER_LC, N_PICKS, LANES), const3),
            pl.BlockSpec((PEER_LC, N_PICKS, LANES), const3),
            pl.BlockSpec((PEER_TM, d), lambda i, j: (i, 0)),
            pl.BlockSpec((PEER_TM, 2 * PEER_HALF),
                         lambda i, j: (jnp.minimum(i + 1, nblk - 1), j // steps_per_head)),
            pl.BlockSpec((1, 2, N_SUB_KEYS, PEER_HALF), lambda i, j: (j // steps_per_head, 0, 0, 0)),
            pl.BlockSpec((N_SUB_KEYS, LANES), lambda i, j: (0, 0)),
            pl.BlockSpec((CAND_ROWS, LANES), lambda i, j: (0, 0)),
        ],
        out_specs=pl.BlockSpec((PEER_TM, d), lambda i, j: (i, 0)),
        out_shape=jax.ShapeDtypeStruct((t, d), F32),
        scratch_shapes=[
            pltpu.VMEM((PEER_NCHUNK, PEER_TM, PEER_CHUNK), BF16),
            pltpu.VMEM((PEER_TM, N_PICKS), jnp.int32),
            pltpu.VMEM((PEER_TM, N_PICKS), F32),
            pltpu.VMEM((N_SUB_KEYS // SUBLANES, PEER_TG * SUBLANES, N_SUB_KEYS), F32),
            pltpu.VMEM((N_SUB_KEYS // SUBLANES, PEER_TG * SUBLANES, N_SUB_KEYS), F32),
            pltpu.VMEM((2, PEER_LC, N_PICKS, LANES), jnp.int32),
            pltpu.VMEM((2, PEER_LC, N_PICKS, LANES), F32),
        ] + _topk_scratch(PEER_LC_STEP),
        compiler_params=pltpu.CompilerParams(
            dimension_semantics=("arbitrary", "arbitrary"), vmem_limit_bytes=VMEM_LIMIT),
        name="peer",
    )(xn2, u_tab, v_tab, e0, g0, h1, qp, sub_keys, ids, pos)


def _s5_discretise_kernel(lr_ref, li_ref, ls_ref, br_ref, bi_ref, ar_ref, ai_ref, bbr_ref, bbi_ref):
    lr, li = lr_ref[...], li_ref[...]
    step = jnp.exp(ls_ref[...])
    mag = jnp.exp(lr * step)
    abar_re = mag * jnp.cos(li * step)
    abar_im = mag * jnp.sin(li * step)
    den = lr * lr + li * li
    nr, ni = abar_re - 1.0, abar_im
    coef_re = (nr * lr + ni * li) / den
    coef_im = (ni * lr - nr * li) / den
    ar_ref[...] = abar_re
    ai_ref[...] = abar_im
    br, bi = br_ref[...], bi_ref[...]
    bbr_ref[...] = coef_re * br - coef_im * bi
    bbi_ref[...] = coef_re * bi + coef_im * br


def _s5_params(lam_re, lam_im, log_step, b_re, b_im, c_re, c_im, bsz):
    col = lambda m: m.reshape(S5_LANES, 1)
    pole = jax.ShapeDtypeStruct((S5_LANES, 1), F32)
    pole_in = jax.ShapeDtypeStruct((S5_LANES, S5_GROUP), F32)
    abar_re, abar_im, bbar_re, bbar_im = pl.pallas_call(
        _s5_discretise_kernel, out_shape=[pole, pole, pole_in, pole_in], name="s5_discretise",
    )(col(lam_re), col(lam_im), col(jnp.repeat(log_step, S5_STATE)),
      b_re.reshape(S5_LANES, S5_GROUP), b_im.reshape(S5_LANES, S5_GROUP))
    nlc = S5_LANES // S5_LC
    gpc = S5_GROUPS // nlc
    eye = jnp.eye(gpc, dtype=F32)
    to_b = lambda m: jnp.einsum('lgpc,gh->lgchp', m.reshape(nlc, gpc, S5_STATE, S5_GROUP),
                                eye).reshape(nlc, S5_CH, S5_LC)
    to_c = lambda m: jnp.einsum('lgcp,gh->lgphc', m.reshape(nlc, gpc, S5_GROUP, S5_STATE),
                                eye).reshape(nlc, S5_LC, S5_CH)
    bblk = jnp.concatenate([to_b(bbar_re), to_b(bbar_im)], axis=2).astype(BF16)
    cblk = jnp.concatenate([to_c(c_re), -to_c(c_im)], axis=1).astype(BF16)
    a = jnp.stack([abar_re.reshape(-1), abar_im.reshape(-1)])
    a8 = jnp.broadcast_to(a[:, None, :], (2, bsz, S5_LANES))
    return a8, bblk, cblk


def kernel(x, ln1_g, w_in, b_gate, s5_lambda_re, s5_lambda_im, s5_log_step, s5_b_re, s5_b_im, s5_c_re, s5_c_im, s5_d, s5_w_glu, w_s5_branch, q_norm_g, k_norm_g, rel_bias_table, attn_sinks, w_attn_branch, w_out, ln2_g, peer_w_query, peer_sub_keys, peer_u, peer_v):
    bsz, seq, d = x.shape
    h = x
    for layer in range(ln1_g.shape[0]):
        w_uqkv = w_in[layer, :, :UQKV_WIDTH].astype(BF16)
        w_gate = w_in[layer, :, UQKV_WIDTH:].astype(BF16)
        u_tm, q, k, v = _in_proj(h, ln1_g[layer][None], w_uqkv)

        a8, bblk, cblk = _s5_params(s5_lambda_re[layer], s5_lambda_im[layer], s5_log_step[layer],
                                    s5_b_re[layer], s5_b_im[layer], s5_c_re[layer], s5_c_im[layer], bsz)
        ys5 = _s5(u_tm, a8, bblk, cblk, s5_d[layer].reshape(1, S5_WIDTH),
                  s5_w_glu[layer].astype(BF16), w_s5_branch[layer].astype(BF16), bsz)

        attn = _attention(q, k, v, q_norm_g[layer][None], k_norm_g[layer][None],
                          _t5_bucket_matrix(), rel_bias_table, attn_sinks[layer])

        h1, xn2, qp, u_bf, v_bf = _merge(
            h, attn, ys5, ln1_g[layer][None], w_gate, b_gate[layer][None],
            w_attn_branch[layer].astype(BF16), w_out[layer].astype(BF16), ln2_g[layer][None],
            peer_w_query[layer].astype(BF16), peer_u[layer], peer_v[layer])

        sub_keys = peer_sub_keys[layer].astype(BF16)
        pos = jnp.asarray(_CAND_POS)
        ids = jnp.asarray(_KEY_IDS)
        e0, g0 = _topk(qp, sub_keys, ids, pos)
        out = _peer(xn2, u_bf, v_bf, e0, g0, h1, qp, sub_keys, ids, pos)
        h = out.reshape(bsz, seq, d)
    return h
```
